```python
import math
import jax
import jax.numpy as jnp
from jax import lax
import numpy as np

D_MODEL = 1024
BATCH = 32
SEQ = 2048
DEPTH = 2
DEC_BATCH = 16
DEC_SEQ = 32
PAST_LEN = 2048

CHUNK = 64
Q_BLOCK = 128

N_HEADS_A = 8
HEAD_DIM_A = 64
N_IDX_HEADS = 8
IDX_DIM = 32
TOPK_MAX = 256
NUM_BUCKETS = 32
MAX_DISTANCE = 128

SSM_WIDTH = 512
GROUP_SIZE = 16
N_GROUPS = SSM_WIDTH // GROUP_SIZE
STATE_DIM = 64
DT_MIN = 0.001
DT_MAX = 0.1

N_HEADS_C = 8
HEAD_DIM_C = 64
FORGET_BIAS_INIT = 2.0

N_BRANCH = 3

D_FF = 2816
N_EXPERTS = 8
TOP_K_EXPERTS = 2
D_FF_EXPERT = 1408
N_DENSE = (DEPTH + 1) // 2
N_MOE = DEPTH // 2

ALPHA = (2 * DEPTH) ** 0.25
BETA = (8 * DEPTH) ** -0.25
LN_EPS = 1e-5
N_STATE_KINDS = 8

PROJ_SIZES = (
    N_HEADS_A * HEAD_DIM_A,
    HEAD_DIM_A,
    HEAD_DIM_A,
    N_IDX_HEADS * IDX_DIM,
    IDX_DIM,
    N_IDX_HEADS,
    SSM_WIDTH,
    N_HEADS_C * HEAD_DIM_C,
    N_HEADS_C * HEAD_DIM_C,
    N_HEADS_C * HEAD_DIM_C,
    N_HEADS_C,
    N_BRANCH * D_MODEL,
)
PROJ_COLS = sum(PROJ_SIZES)

kernel_name = 'streaming_hybrid_dsa_s5_fox_step'


def layer_norm(x, g, b):
    xf = x.astype(jnp.float32)
    mu = jnp.mean(xf, axis=-1, keepdims=True)
    xc = xf - mu
    var = jnp.mean(xc * xc, axis=-1, keepdims=True)
    return (xc * lax.rsqrt(var + LN_EPS) * g.astype(jnp.float32) + b.astype(jnp.float32)).astype(x.dtype)


def t5_bucket(rel):
    half = NUM_BUCKETS // 2
    max_exact = half // 2
    n = jnp.abs(rel)
    large = max_exact + (jnp.log(jnp.maximum(n, 1).astype(jnp.float32) / max_exact)
                         / math.log(MAX_DISTANCE / max_exact) * (half - max_exact)).astype(jnp.int32)
    large = jnp.minimum(large, half - 1)
    return jnp.where(rel > 0, half, 0) + jnp.where(n < max_exact, n, large)


def dsa_attend(q, qi, wi, q_pos, k, v, ki, n_top, t5_table):
    f32 = jnp.float32
    idx = jax.nn.relu(jnp.einsum('bqhd,bsd->bqhs', qi.astype(f32), ki.astype(f32)))
    score = jnp.einsum('bqh,bqhs->bqs', wi.astype(f32), idx)
    k_pos = jnp.arange(k.shape[1], dtype=jnp.int32)
    admissible = (k_pos[None, :] // CHUNK) <= (q_pos[:, None] // CHUNK)
    score = jnp.where(admissible[None], score, -jnp.inf)
    top_val, top_idx = lax.top_k(score, n_top)
    valid = jnp.isfinite(top_val)
    gather = jax.vmap(lambda rows, ids: rows[ids])
    k_sel = gather(k, top_idx)
    v_sel = gather(v, top_idx)
    logits = jnp.einsum('bqhd,bqkd->bhqk', q, k_sel).astype(f32) * (HEAD_DIM_A ** -0.5)
    bias = t5_table[t5_bucket(top_idx - q_pos[None, :, None])]
    logits = logits + jnp.moveaxis(bias, -1, 1).astype(f32)
    logits = jnp.where(valid[:, None], logits, -jnp.inf)
    p = jax.nn.softmax(logits, axis=-1).astype(v.dtype)
    out = jnp.einsum('bhqk,bqkd->bqhd', p, v_sel)
    return out.reshape(out.shape[0], out.shape[1], -1)


def fox_attend(q, dq, q_pos, k, v, dk):
    f32 = jnp.float32
    logits = jnp.einsum('bqhd,bkhd->bhqk', q, k).astype(f32) * (HEAD_DIM_C ** -0.5)
    logits = logits + jnp.moveaxis(dq, -1, 1)[..., None] - jnp.moveaxis(dk, -1, 1)[:, :, None, :]
    k_pos = jnp.arange(k.shape[1], dtype=jnp.int32)
    logits = jnp.where(k_pos[None, :] <= q_pos[:, None], logits, -jnp.inf)
    p = jax.nn.softmax(logits, axis=-1).astype(v.dtype)
    out = jnp.einsum('bhqk,bkhd->bqhd', p, v)
    return out.reshape(out.shape[0], out.shape[1], -1)


def sweep_query_blocks(fn, q_arrays, T):
    n_blk = T // Q_BLOCK

    def one(i):
        start = i * Q_BLOCK
        sl = [lax.dynamic_slice_in_dim(a, start, Q_BLOCK, axis=1) for a in q_arrays]
        q_pos = start + jnp.arange(Q_BLOCK, dtype=jnp.int32)
        return fn(*sl, q_pos)

    out = lax.map(one, jnp.arange(n_blk, dtype=jnp.int32))
    return jnp.moveaxis(out, 0, 1).reshape(out.shape[1], T, -1)


def s5_discretize(lam_re, lam_im, log_step, b_re, b_im):
    f32 = jnp.float32
    dt = jnp.exp(log_step.astype(f32))[:, None]
    lr, li = lam_re.astype(f32), lam_im.astype(f32)
    mag = jnp.exp(lr * dt)
    ab_re = mag * jnp.cos(li * dt)
    ab_im = mag * jnp.sin(li * dt)
    den = lr * lr + li * li
    fr = ((ab_re - 1.0) * lr + ab_im * li) / den
    fi = (ab_im * lr - (ab_re - 1.0) * li) / den
    br, bi = b_re.astype(f32), b_im.astype(f32)
    bb_re = fr[..., None] * br - fi[..., None] * bi
    bb_im = fr[..., None] * bi + fi[..., None] * br
    return ab_re, ab_im, bb_re, bb_im


def _ssm_combine(e1, e2):
    a1r, a1i, b1r, b1i = e1
    a2r, a2i, b2r, b2i = e2
    return (a1r * a2r - a1i * a2i, a1r * a2i + a1i * a2r,
            a2r * b1r - a2i * b1i + b2r, a2r * b1i + a2i * b1r + b2i)


def s5_mixer(u, h0_re, h0_im, lam_re, lam_im, log_step, b_re, b_im, c_re, c_im, d_skip):
    f32 = jnp.float32
    Bn, T, _ = u.shape
    ab_re, ab_im, bb_re, bb_im = s5_discretize(lam_re, lam_im, log_step, b_re, b_im)
    ug = u.astype(f32).reshape(Bn, T, N_GROUPS, GROUP_SIZE)
    bu_re = jnp.einsum('btgc,gpc->btgp', ug, bb_re)
    bu_im = jnp.einsum('btgc,gpc->btgp', ug, bb_im)
    h0r, h0i = h0_re.astype(f32), h0_im.astype(f32)
    bu_re = bu_re.at[:, 0].add(ab_re * h0r - ab_im * h0i)
    bu_im = bu_im.at[:, 0].add(ab_re * h0i + ab_im * h0r)
    a_re = jnp.broadcast_to(ab_re, (1, T) + ab_re.shape)
    a_im = jnp.broadcast_to(ab_im, (1, T) + ab_im.shape)
    _, _, h_re, h_im = lax.associative_scan(_ssm_combine, (a_re, a_im, bu_re, bu_im), axis=1)
    y = (jnp.einsum('btgp,gcp->btgc', h_re, c_re.astype(f32))
         - jnp.einsum('btgp,gcp->btgc', h_im, c_im.astype(f32)))
    y = y.reshape(Bn, T, SSM_WIDTH) + d_skip.astype(f32) * u.astype(f32)
    return y.astype(u.dtype), h_re[:, -1], h_im[:, -1]


def token_mixer(x, past, w_in, b_forget, lam_re, lam_im, log_step, ssm_b_re, ssm_b_im,
                ssm_c_re, ssm_c_im, ssm_d, w_a_out, w_b_glu, w_c_out, w_o, t5_table):
    f32 = jnp.float32
    Bn, T, _ = x.shape
    proj = x @ w_in
    (a_q, a_k, a_v, a_qi, a_ki, a_w, b_u, c_q, c_k, c_v, c_f, gates) = jnp.split(
        proj, np.cumsum(PROJ_SIZES)[:-1].tolist(), axis=-1)
    a_q = a_q.reshape(Bn, T, N_HEADS_A, HEAD_DIM_A)
    a_qi = a_qi.reshape(Bn, T, N_IDX_HEADS, IDX_DIM)
    c_q = c_q.reshape(Bn, T, N_HEADS_C, HEAD_DIM_C)
    c_k = c_k.reshape(Bn, T, N_HEADS_C, HEAD_DIM_C)
    c_v = c_v.reshape(Bn, T, N_HEADS_C, HEAD_DIM_C)
    logf = jax.nn.log_sigmoid(c_f.astype(f32) + b_forget.astype(f32))
    if past is None:
        n_top = min(TOPK_MAX, T // 4)
        y_a = sweep_query_blocks(
            lambda q, qi, wi, qp: dsa_attend(q, qi, wi, qp, a_k, a_v, a_ki, n_top, t5_table),
            (a_q, a_qi, a_w), T)
        h0_re = jnp.zeros((Bn, N_GROUPS, STATE_DIM), f32)
        h0_im = jnp.zeros((Bn, N_GROUPS, STATE_DIM), f32)
        dcum = jnp.cumsum(logf, axis=1)
        y_c = sweep_query_blocks(
            lambda q, dq, qp: fox_attend(q, dq, qp, c_k, c_v, dcum), (c_q, dcum), T)
    else:
        p_ak, p_av, p_aki, h0_re, h0_im, p_ck, p_cv, p_logf = past
        past_len = p_ak.shape[1]
        q_pos = past_len + jnp.arange(T, dtype=jnp.int32)
        ak_all = jnp.concatenate([p_ak.astype(a_k.dtype), a_k], axis=1)
        av_all = jnp.concatenate([p_av.astype(a_v.dtype), a_v], axis=1)
        aki_all = jnp.concatenate([p_aki.astype(a_ki.dtype), a_ki], axis=1)
        n_top = min(TOPK_MAX, (past_len + T) // 4)
        y_a = dsa_attend(a_q, a_qi, a_w, q_pos, ak_all, av_all, aki_all, n_top, t5_table)
        ck_all = jnp.concatenate([p_ck.astype(c_k.dtype), c_k], axis=1)
        cv_all = jnp.concatenate([p_cv.astype(c_v.dtype), c_v], axis=1)
        dcum = jnp.cumsum(jnp.concatenate([p_logf.astype(f32), logf], axis=1), axis=1)
        y_c = fox_attend(c_q, dcum[:, past_len:], q_pos, ck_all, cv_all, dcum)
    y_b, h_re, h_im = s5_mixer(b_u, h0_re, h0_im, lam_re, lam_im, log_step,
                               ssm_b_re, ssm_b_im, ssm_c_re, ssm_c_im, ssm_d)
    glu = jax.nn.gelu(y_b) @ w_b_glu
    branch_b = glu[..., :D_MODEL] * jax.nn.sigmoid(glu[..., D_MODEL:])
    branch_a = y_a @ w_a_out
    branch_c = y_c @ w_c_out
    g = jax.nn.sigmoid(gates.reshape(Bn, T, N_BRANCH, D_MODEL))
    merged = g[..., 0, :] * branch_a + g[..., 1, :] * branch_b + g[..., 2, :] * branch_c
    out = merged @ w_o
    new_state = (a_k, a_v, a_ki, h_re, h_im, c_k, c_v, logf)
    return out, new_state


def swiglu(x, w_gu, w_down):
    h = x @ w_gu
    d = w_down.shape[0]
    return (jax.nn.silu(h[..., :d]) * h[..., d:]) @ w_down


def moe_ffn(x, w_router, b_router, w_exp_gu, w_exp_down):
    f32 = jnp.float32
    logits = (x @ w_router).astype(f32) + b_router.astype(f32)
    top_val, top_idx = lax.top_k(logits, TOP_K_EXPERTS)
    top_w = jax.nn.softmax(top_val, axis=-1)
    gate = jnp.einsum('btk,btke->bte', top_w,
                      jax.nn.one_hot(top_idx, N_EXPERTS, dtype=f32)).astype(x.dtype)
    y = jnp.zeros_like(x)
    for e in range(N_EXPERTS):
        y = y + gate[..., e:e + 1] * swiglu(x, w_exp_gu[e], w_exp_down[e])
    return y


def setup_inputs(seed: int = 0) -> dict:
    key = jax.random.key(seed)
    ks = iter(jax.random.split(key, 48))
    f32 = jnp.float32

    def nrm(shape, scale=1.0):
        return jax.random.normal(next(ks), shape, f32) * scale

    state_ids = jnp.arange(STATE_DIM, dtype=f32)
    return {
        'x_prompt': nrm((BATCH, SEQ, D_MODEL)),
        'x_sample': nrm((DEC_BATCH, DEC_SEQ, D_MODEL)),
        'cache_a_k': nrm((DEPTH, DEC_BATCH, PAST_LEN, HEAD_DIM_A)),
        'cache_a_v': nrm((DEPTH, DEC_BATCH, PAST_LEN, HEAD_DIM_A)),
        'cache_a_kidx': nrm((DEPTH, DEC_BATCH, PAST_LEN, IDX_DIM)),
        'state_ssm_re': nrm((DEPTH, DEC_BATCH, N_GROUPS, STATE_DIM), 0.1),
        'state_ssm_im': nrm((DEPTH, DEC_BATCH, N_GROUPS, STATE_DIM), 0.1),
        'cache_c_k': nrm((DEPTH, DEC_BATCH, PAST_LEN, N_HEADS_C, HEAD_DIM_C)),
        'cache_c_v': nrm((DEPTH, DEC_BATCH, PAST_LEN, N_HEADS_C, HEAD_DIM_C)),
        'cache_c_logf': jax.nn.log_sigmoid(FORGET_BIAS_INIT + nrm((DEPTH, DEC_BATCH, PAST_LEN, N_HEADS_C))),
        'w_in': nrm((DEPTH, D_MODEL, PROJ_COLS), D_MODEL ** -0.5),
        'b_forget': FORGET_BIAS_INIT + nrm((DEPTH, N_HEADS_C), 0.1),
        'ssm_lam_re': -0.5 + nrm((DEPTH, N_GROUPS, STATE_DIM), 0.01),
        'ssm_lam_im': math.pi * state_ids + nrm((DEPTH, N_GROUPS, STATE_DIM), 0.01),
        'ssm_log_step': jax.random.uniform(next(ks), (DEPTH, N_GROUPS), f32,
                                           math.log(DT_MIN), math.log(DT_MAX)),
        'ssm_b_re': nrm((DEPTH, N_GROUPS, STATE_DIM, GROUP_SIZE), (2 * GROUP_SIZE) ** -0.5),
        'ssm_b_im': nrm((DEPTH, N_GROUPS, STATE_DIM, GROUP_SIZE), (2 * GROUP_SIZE) ** -0.5),
        'ssm_c_re': nrm((DEPTH, N_GROUPS, GROUP_SIZE, STATE_DIM), STATE_DIM ** -0.5),
        'ssm_c_im': nrm((DEPTH, N_GROUPS, GROUP_SIZE, STATE_DIM), STATE_DIM ** -0.5),
        'ssm_d': nrm((DEPTH, SSM_WIDTH)),
        'w_a_out': nrm((DEPTH, N_HEADS_A * HEAD_DIM_A, D_MODEL), (N_HEADS_A * HEAD_DIM_A) ** -0.5),
        'w_b_glu': nrm((DEPTH, SSM_WIDTH, 2 * D_MODEL), SSM_WIDTH ** -0.5),
        'w_c_out': nrm((DEPTH, N_HEADS_C * HEAD_DIM_C, D_MODEL), (N_HEADS_C * HEAD_DIM_C) ** -0.5),
        'w_o': nrm((DEPTH, D_MODEL, D_MODEL), BETA * D_MODEL ** -0.5),
        'ln1_g': 1.0 + nrm((DEPTH, D_MODEL), 0.02),
        'ln1_b': nrm((DEPTH, D_MODEL), 0.02),
        'ln2_g': 1.0 + nrm((DEPTH, D_MODEL), 0.02),
        'ln2_b': nrm((DEPTH, D_MODEL), 0.02),
        't5_table': nrm((NUM_BUCKETS, N_HEADS_A), 0.5),
        'w_ffn_gu': nrm((N_DENSE, D_MODEL, 2 * D_FF), D_MODEL ** -0.5),
        'w_ffn_down': nrm((N_DENSE, D_FF, D_MODEL), BETA * D_FF ** -0.5),
        'w_router': nrm((N_MOE, D_MODEL, N_EXPERTS), D_MODEL ** -0.5),
        'b_router': nrm((N_MOE, N_EXPERTS), 0.01),
        'w_exp_gu': nrm((N_MOE, N_EXPERTS, D_MODEL, 2 * D_FF_EXPERT), D_MODEL ** -0.5),
        'w_exp_down': nrm((N_MOE, N_EXPERTS, D_FF_EXPERT, D_MODEL), BETA * D_FF_EXPERT ** -0.5),
    }


def reference(x_prompt, x_sample, cache_a_k, cache_a_v, cache_a_kidx, state_ssm_re, state_ssm_im,
              cache_c_k, cache_c_v, cache_c_logf, w_in, b_forget, ssm_lam_re, ssm_lam_im, ssm_log_step,
              ssm_b_re, ssm_b_im, ssm_c_re, ssm_c_im, ssm_d, w_a_out, w_b_glu, w_c_out, w_o,
              ln1_g, ln1_b, ln2_g, ln2_b, t5_table, w_ffn_gu, w_ffn_down, w_router, b_router,
              w_exp_gu, w_exp_down):
    xp, xs = x_prompt, x_sample
    rows_p = [[] for _ in range(N_STATE_KINDS)]
    rows_s = [[] for _ in range(N_STATE_KINDS)]
    for layer in range(DEPTH):
        mixer_w = (w_in[layer], b_forget[layer], ssm_lam_re[layer], ssm_lam_im[layer], ssm_log_step[layer],
                   ssm_b_re[layer], ssm_b_im[layer], ssm_c_re[layer], ssm_c_im[layer], ssm_d[layer],
                   w_a_out[layer], w_b_glu[layer], w_c_out[layer], w_o[layer], t5_table)
        past = (cache_a_k[layer], cache_a_v[layer], cache_a_kidx[layer], state_ssm_re[layer],
                state_ssm_im[layer], cache_c_k[layer], cache_c_v[layer], cache_c_logf[layer])
        mp, st_p = token_mixer(xp, None, *mixer_w)
        ms, st_s = token_mixer(xs, past, *mixer_w)
        xp = layer_norm(ALPHA * xp + mp, ln1_g[layer], ln1_b[layer])
        xs = layer_norm(ALPHA * xs + ms, ln1_g[layer], ln1_b[layer])
        i = layer // 2
        if layer % 2 == 0:
            fp = swiglu(xp, w_ffn_gu[i], w_ffn_down[i])
            fs = swiglu(xs, w_ffn_gu[i], w_ffn_down[i])
        else:
            fp = moe_ffn(xp, w_router[i], b_router[i], w_exp_gu[i], w_exp_down[i])
            fs = moe_ffn(xs, w_router[i], b_router[i], w_exp_gu[i], w_exp_down[i])
        xp = layer_norm(ALPHA * xp + fp, ln2_g[layer], ln2_b[layer])
        xs = layer_norm(ALPHA * xs + fs, ln2_g[layer], ln2_b[layer])
        for j in range(N_STATE_KINDS):
            rows_p[j].append(st_p[j])
            rows_s[j].append(st_s[j])
    (a_k_p, a_v_p, a_kidx_p, ssm_re_p, ssm_im_p, c_k_p, c_v_p, c_logf_p) = [jnp.stack(r) for r in rows_p]
    (a_k_s, a_v_s, a_kidx_s, ssm_re_s, ssm_im_s, c_k_s, c_v_s, c_logf_s) = [jnp.stack(r) for r in rows_s]
    return (xp, xs, a_k_p, a_k_s, a_v_p, a_v_s, a_kidx_p, a_kidx_s, ssm_re_p, ssm_re_s,
            ssm_im_p, ssm_im_s, c_k_p, c_k_s, c_v_p, c_v_s, c_logf_p, c_logf_s)
```

```python
import functools
import math

import jax
import jax.numpy as jnp
import numpy as np
from jax import lax
from jax.experimental import pallas as pl
from jax.experimental.pallas import tpu as pltpu

F32 = jnp.float32
BF16 = jnp.bfloat16
I32 = jnp.int32

D_MODEL = 1024
CHUNK = 64
N_HEADS_A = 8
HEAD_DIM_A = 64
N_IDX_HEADS = 8
IDX_DIM = 32
TOPK_MAX = 256
NUM_BUCKETS = 32
MAX_DISTANCE = 128
SSM_WIDTH = 512
GROUP_SIZE = 16
N_GROUPS = SSM_WIDTH // GROUP_SIZE
STATE_DIM = 64
SSM_STATES = N_GROUPS * STATE_DIM
N_HEADS_C = 8
HEAD_DIM_C = 64
N_BRANCH = 3
D_FF = 2816
N_EXPERTS = 8
D_FF_EXPERT = 1408
LN_EPS = 1e-5
PROJ_SIZES = (512, 64, 64, 256, 32, 8, 512, 512, 512, 512, 8, 3072)

LANES = 128
SUBLANES = 8
VMEM_LIMIT_BYTES = 56 * 1024 * 1024

PROJ_TM = 256
S5_ROWS = 512
S5_CW = 256
DSA_TQ = 128
DSA_KC = 256
FOX_TQ_PROMPT = 256
FOX_KC = 256
OUT_TM = 256
FFN_TM = 1024
FFN_FC = 256
MOE_TM = 512

INT_MIN = -(2 ** 31)
SKEY_NEG_INF = -2139095041


def _cparams(sem):
    return pltpu.CompilerParams(dimension_semantics=sem, vmem_limit_bytes=VMEM_LIMIT_BYTES)


def _mm(a, b):
    return jnp.dot(a, b, preferred_element_type=F32)


def _mm_nt(a, b):
    return lax.dot_general(a, b, (((1,), (1,)), ((), ())), preferred_element_type=F32)


def _layer_norm(z, g, b):
    mu = jnp.mean(z, axis=-1, keepdims=True)
    zc = z - mu
    var = jnp.mean(zc * zc, axis=-1, keepdims=True)
    return zc * lax.rsqrt(var + LN_EPS) * g + b


def _pack_w_in(w):
    offs = np.cumsum((0,) + PROJ_SIZES)
    a_q, a_k, a_v, a_qi, a_ki, a_w, b_u, c_q, c_k, c_v, c_f, gates = [
        w[:, offs[i]:offs[i + 1]] for i in range(len(PROJ_SIZES))]
    pad = jnp.zeros((w.shape[0], LANES - IDX_DIM - N_IDX_HEADS - N_HEADS_C), w.dtype)
    return jnp.concatenate([a_q, a_k, a_v, a_qi, a_ki, a_w, c_f, pad, b_u, c_q, c_k, c_v, gates],
                           axis=1).astype(BF16)


PROJ_OUT = (("qa", 0, 512), ("kv", 512, 128), ("qi", 640, 256), ("misc", 896, 128), ("u", 1024, 512),
            ("cq", 1536, 512), ("ck", 2048, 512), ("cv", 2560, 512), ("gates", 3072, 3072))
PROJ_COLS_PACKED = 6144


def _proj_kernel(x_ref, w_ref, *out_refs):
    xb = x_ref[...].astype(BF16)
    for (_, lo, width), o_ref in zip(PROJ_OUT, out_refs):
        for c in range(0, width, 512):
            cw = min(512, width - c)
            o_ref[:, c:c + cw] = _mm(xb, w_ref[:, lo + c:lo + c + cw])


def _project(x2d, w_packed, n_batch, n_time, time_major_u):
    m = x2d.shape[0]
    tm = min(PROJ_TM, m)
    n_t = n_time // tm if time_major_u else 1
    out_shape, out_specs = [], []
    for name, _, width in PROJ_OUT:
        if name == "u" and time_major_u:
            out_shape.append(jax.ShapeDtypeStruct((n_time, n_batch * width), F32))
            out_specs.append(pl.BlockSpec((tm, width), lambda i: (i % n_t, i // n_t)))
        else:
            out_shape.append(jax.ShapeDtypeStruct((m, width), F32))
            out_specs.append(pl.BlockSpec((tm, width), lambda i: (i, 0)))
    outs = pl.pallas_call(
        _proj_kernel,
        grid=(m // tm,),
        in_specs=[pl.BlockSpec((tm, D_MODEL), lambda i: (i, 0)),
                  pl.BlockSpec((D_MODEL, PROJ_COLS_PACKED), lambda i: (0, 0))],
        out_specs=out_specs,
        out_shape=out_shape,
        compiler_params=_cparams(("arbitrary",)),
        name="proj",
    )(x2d, w_packed)
    return dict(zip([p[0] for p in PROJ_OUT], outs))


def _s5_disc_kernel(lr_ref, li_ref, ls_ref, br_ref, bi_ref, ar_ref, ai_ref, bbr_ref, bbi_ref):
    lr, li = lr_ref[...], li_ref[...]
    dt = jnp.exp(ls_ref[...])
    mag = jnp.exp(lr * dt)
    ab_re = mag * jnp.cos(li * dt)
    ab_im = mag * jnp.sin(li * dt)
    den = lr * lr + li * li
    fr = ((ab_re - 1.0) * lr + ab_im * li) / den
    fi = (ab_im * lr - (ab_re - 1.0) * li) / den
    ar_ref[...] = ab_re
    ai_ref[...] = ab_im
    br, bi = br_ref[...], bi_ref[...]
    bbr_ref[...] = fr[:, None, :] * br - fi[:, None, :] * bi
    bbi_ref[...] = fr[:, None, :] * bi + fi[:, None, :] * br


def _s5_discretize(lam_re, lam_im, log_step, b_re, b_im):
    g, p, gs = b_re.shape
    shp = lambda *s: jax.ShapeDtypeStruct(s, F32)
    return pl.pallas_call(
        _s5_disc_kernel,
        out_shape=[shp(g, p), shp(g, p), shp(g, gs, p), shp(g, gs, p)],
        name="s5_disc",
    )(lam_re, lam_im, log_step.reshape(g, 1), jnp.swapaxes(b_re, 1, 2), jnp.swapaxes(b_im, 1, 2))


def _block_diag(blocks):
    g, r, c = blocks.shape
    eye = jnp.eye(g, dtype=bool)
    return jnp.where(eye[:, None, :, None], blocks[:, :, None, :], 0.0).reshape(g * r, g * c)


def _s5_kernel(u_ref, h0r_ref, h0i_ref, ar_ref, ai_ref, bd_ref, cr_ref, ci_ref, d_ref,
               y_ref, hr_ref, hi_ref, bur_ref, bui_ref, *, tc, bn):
    rows = tc * bn

    @pl.when(pl.program_id(0) == 0)
    def _():
        hr_ref[...] = h0r_ref[...]
        hi_ref[...] = h0i_ref[...]

    u = u_ref[...].reshape(rows, SSM_WIDTH)
    ub = u.astype(BF16)
    bur_ref[...] = _mm(ub, bd_ref[:, :SSM_STATES])
    bui_ref[...] = _mm(ub, bd_ref[:, SSM_STATES:])

    for cc in range(SSM_STATES // S5_CW):
        sl = slice(cc * S5_CW, (cc + 1) * S5_CW)
        ar = jnp.broadcast_to(ar_ref[:, sl], (bn, S5_CW))
        ai = jnp.broadcast_to(ai_ref[:, sl], (bn, S5_CW))

        def step(t, carry, sl=sl, ar=ar, ai=ai):
            hr, hi = carry
            r0 = pl.multiple_of(t * bn, bn)
            nr = ar * hr - ai * hi + bur_ref[pl.ds(r0, bn), sl]
            ni = ar * hi + ai * hr + bui_ref[pl.ds(r0, bn), sl]
            bur_ref[pl.ds(r0, bn), sl] = nr
            bui_ref[pl.ds(r0, bn), sl] = ni
            return nr, ni

        hr, hi = lax.fori_loop(0, tc, step, (hr_ref[:, sl], hi_ref[:, sl]))
        hr_ref[:, sl] = hr
        hi_ref[:, sl] = hi

    y = (_mm(bur_ref[...].astype(BF16), cr_ref[...]) - _mm(bui_ref[...].astype(BF16), ci_ref[...])
         + d_ref[...] * u)
    y_ref[...] = y.reshape(tc, bn, SSM_WIDTH)


def _s5(u_tm, h0_re, h0_im, ab_re, ab_im, bd, cr, ci, d_skip):
    n_time, bn, _ = u_tm.shape
    tc = min(S5_ROWS // bn, n_time)
    const = lambda *s: pl.BlockSpec(s, lambda i: (0,) * len(s))
    y, hr, hi = pl.pallas_call(
        functools.partial(_s5_kernel, tc=tc, bn=bn),
        grid=(n_time // tc,),
        in_specs=[pl.BlockSpec((tc, bn, SSM_WIDTH), lambda i: (i, 0, 0)),
                  const(bn, SSM_STATES), const(bn, SSM_STATES),
                  const(1, SSM_STATES), const(1, SSM_STATES),
                  const(SSM_WIDTH, 2 * SSM_STATES),
                  const(SSM_STATES, SSM_WIDTH), const(SSM_STATES, SSM_WIDTH),
                  const(1, SSM_WIDTH)],
        out_specs=[pl.BlockSpec((tc, bn, SSM_WIDTH), lambda i: (i, 0, 0)),
                   const(bn, SSM_STATES), const(bn, SSM_STATES)],
        out_shape=[jax.ShapeDtypeStruct((n_time, bn, SSM_WIDTH), F32),
                   jax.ShapeDtypeStruct((bn, SSM_STATES), F32),
                   jax.ShapeDtypeStruct((bn, SSM_STATES), F32)],
        scratch_shapes=[pltpu.VMEM((tc * bn, SSM_STATES), F32), pltpu.VMEM((tc * bn, SSM_STATES), F32)],
        compiler_params=_cparams(("arbitrary",)),
        name="s5",
    )(u_tm, h0_re, h0_im, ab_re, ab_im, bd, cr, ci, d_skip)
    return y, hr, hi


def _t5_bucket(rel):
    half = NUM_BUCKETS // 2
    max_exact = half // 2
    n = jnp.abs(rel)
    large = max_exact + (jnp.log(jnp.maximum(n, 1).astype(F32) / max_exact)
                         / math.log(MAX_DISTANCE / max_exact) * (half - max_exact)).astype(I32)
    large = jnp.minimum(large, half - 1)
    return jnp.where(rel > 0, half, 0) + jnp.where(n < max_exact, n, large)


DSA_BIAS_TILES = 4


def _dsa_bias_tiles(t5_table):
    q = jnp.arange(DSA_TQ, dtype=I32)[:, None]
    k = jnp.arange(DSA_KC, dtype=I32)[None, :]
    tiles = []
    for e in range(DSA_BIAS_TILES):
        rel = k - q - LANES * e if e < DSA_BIAS_TILES - 1 else jnp.full((DSA_TQ, DSA_KC), -(1 << 20), I32)
        tiles.append(jnp.moveaxis(t5_table[_t5_bucket(rel)], -1, 0))
    return jnp.stack(tiles).reshape(DSA_BIAS_TILES, N_HEADS_A * DSA_TQ, DSA_KC).astype(F32)


def _sort_key(s):
    bits = lax.bitcast_convert_type(s, I32)
    return bits ^ (lax.shift_right_arithmetic(bits, 31) & 0x7FFFFFFF)


def _dsa_kernel(qa_ref, qit_ref, wt_ref, k2_ref, v2_ref, ki_ref, bias_ref, o_ref, skey_ref, tie_ref,
                *, q_off, kv_len, n_top, idx_bits):
    tq, kc = DSA_TQ, DSA_KC
    qb = pl.program_id(1)
    q_base = q_off + qb * tq
    adm_end = jnp.minimum(((q_base + tq - 1) // CHUNK + 1) * CHUNK, kv_len)
    n_c = (adm_end + kc - 1) // kc

    q_pos_t = q_base + lax.broadcasted_iota(I32, (kc, tq), 1)
    k_loc_t = lax.broadcasted_iota(I32, (kc, tq), 0)

    qit = qit_ref[0].astype(BF16)
    w = wt_ref[0]

    def score_chunk(c, _):
        r0 = pl.multiple_of(c * kc, kc)
        kic = ki_ref[0, pl.ds(r0, kc), :]
        acc = jnp.zeros((kc, tq), F32)
        for h in range(N_IDX_HEADS):
            d = _mm(kic, qit[h * IDX_DIM:(h + 1) * IDX_DIM, :])
            acc = acc + w[h:h + 1, :] * jnp.maximum(d, 0.0)
        acc = jnp.where(acc == 0.0, 0.0, acc)
        k_pos = r0 + k_loc_t
        adm = ((k_pos // CHUNK) <= (q_pos_t // CHUNK)) & (k_pos < kv_len)
        skey_ref[pl.ds(r0, kc), :] = _sort_key(jnp.where(adm, acc, -jnp.inf))
        return 0

    lax.fori_loop(0, n_c, score_chunk, 0)

    def count(pred_fn):
        def body(c, cnt):
            r0 = pl.multiple_of(c * kc, kc)
            sk = skey_ref[pl.ds(r0, kc), :]
            return cnt + jnp.sum(jnp.where(pred_fn(sk, r0 + k_loc_t), 1.0, 0.0), axis=0, keepdims=True)
        return lax.fori_loop(0, n_c, body, jnp.zeros((1, tq), F32))

    def value_bit(i, t):
        cand = t ^ lax.shift_left(jnp.int32(1), 31 - i)
        cnt = count(lambda sk, _: sk >= cand)
        return jnp.where(cnt >= n_top, cand, t)

    thr = lax.fori_loop(0, 32, value_bit, jnp.full((1, tq), INT_MIN, I32))
    cnt_gt = count(lambda sk, _: sk > thr)
    cnt_ge = count(lambda sk, _: sk >= thr)
    need = n_top - cnt_gt

    tie_ref[...] = jnp.full((1, tq), 1 << 30, I32)
    has_tie = (cnt_ge > n_top) & (thr > SKEY_NEG_INF)

    @pl.when(jnp.max(jnp.where(has_tie, 1.0, 0.0)) > 0.5)
    def _():
        def index_bit(i, j):
            cand = j | lax.shift_left(jnp.int32(1), idx_bits - 1 - i)
            cnt = count(lambda sk, kp: (sk == thr) & (kp < cand))
            return jnp.where(cnt <= need - 1.0, cand, j)
        tie_ref[...] = lax.fori_loop(0, idx_bits, index_bit, jnp.zeros((1, tq), I32))

    tie_idx = tie_ref[...]

    lane = lax.broadcasted_iota(I32, (tq, LANES), 1)
    q_rows = []
    for h in range(N_HEADS_A):
        pair = qa_ref[:, (h // 2) * LANES:(h // 2 + 1) * LANES]
        q_rows.append(jnp.where((lane < HEAD_DIM_A) == (h % 2 == 0), pair, 0.0))
    q_all = (jnp.concatenate(q_rows, axis=0) * (HEAD_DIM_A ** -0.5)).astype(BF16)

    def attend(c, carry):
        m, l, acc = carry
        r0 = pl.multiple_of(c * kc, kc)
        sk = skey_ref[pl.ds(r0, kc), :]
        k_pos = r0 + k_loc_t
        sel_t = ((sk > thr) | ((sk == thr) & (k_pos <= tie_idx))) & (sk > SKEY_NEG_INF)
        sel = jnp.where(sel_t, 1.0, 0.0).T
        s = _mm_nt(q_all, k2_ref[0, pl.ds(r0, kc), :])
        tile = jnp.clip(q_base // LANES - c * (kc // LANES), 0, DSA_BIAS_TILES - 1)
        s = s + bias_ref[tile]
        s = jnp.where(sel[None] > 0.5, s.reshape(N_HEADS_A, tq, kc), -jnp.inf).reshape(N_HEADS_A * tq, kc)
        m_new = jnp.maximum(m, jnp.max(s, axis=1, keepdims=True))
        m_safe = jnp.where(m_new == -jnp.inf, 0.0, m_new)
        alpha = jnp.exp(m - m_safe)
        p = jnp.exp(s - m_safe)
        l = alpha * l + jnp.sum(p, axis=1, keepdims=True)
        acc = alpha * acc + _mm(p.astype(BF16), v2_ref[0, pl.ds(r0, kc), :])
        return m_new, l, acc

    rows = N_HEADS_A * tq
    m, l, acc = lax.fori_loop(
        0, n_c, attend,
        (jnp.full((rows, 1), -jnp.inf, F32), jnp.zeros((rows, 1), F32), jnp.zeros((rows, LANES), F32)))
    out = acc / l
    for j in range(N_HEADS_A // 2):
        o_ref[:, j * LANES:(j + 1) * LANES] = jnp.where(
            lane < HEAD_DIM_A, out[(2 * j) * tq:(2 * j + 1) * tq], out[(2 * j + 1) * tq:(2 * j + 2) * tq])


def _dsa(qa, qit, wt, k2, v2, ki, bias_tiles, *, q_off, kv_len, n_top):
    bn, tq_total, _ = qa.shape
    lp = k2.shape[1]
    nq = tq_total // DSA_TQ
    idx_bits = max(1, int(lp - 1).bit_length())
    return pl.pallas_call(
        functools.partial(_dsa_kernel, q_off=q_off, kv_len=kv_len, n_top=float(n_top), idx_bits=idx_bits),
        grid=(bn, nq),
        in_specs=[pl.BlockSpec((None, DSA_TQ, N_HEADS_A * HEAD_DIM_A), lambda b, q: (b, q, 0)),
                  pl.BlockSpec((1, N_IDX_HEADS * IDX_DIM, DSA_TQ), lambda b, q: (b, 0, q)),
                  pl.BlockSpec((1, N_IDX_HEADS, DSA_TQ), lambda b, q: (b, 0, q)),
                  pl.BlockSpec((1, lp, LANES), lambda b, q: (b, 0, 0)),
                  pl.BlockSpec((1, lp, LANES), lambda b, q: (b, 0, 0)),
                  pl.BlockSpec((1, lp, IDX_DIM), lambda b, q: (b, 0, 0)),
                  pl.BlockSpec((DSA_BIAS_TILES, N_HEADS_A * DSA_TQ, DSA_KC), lambda b, q: (0, 0, 0))],
        out_specs=pl.BlockSpec((None, DSA_TQ, N_HEADS_A * HEAD_DIM_A), lambda b, q: (b, q, 0)),
        out_shape=jax.ShapeDtypeStruct((bn, tq_total, N_HEADS_A * HEAD_DIM_A), F32),
        scratch_shapes=[pltpu.VMEM((lp, DSA_TQ), I32), pltpu.VMEM((1, DSA_TQ), I32)],
        compiler_params=_cparams(("arbitrary", "arbitrary")),
        name="dsa",
    )(qa, qit, wt, k2, v2, ki, bias_tiles)


def _logf_kernel(cf_ref, b_ref, o_ref):
    o_ref[...] = jax.nn.log_sigmoid(cf_ref[...] + b_ref[...])


def _log_forget(cf2d, b_forget):
    m = cf2d.shape[0]
    tm = min(2048, m)
    return pl.pallas_call(
        _logf_kernel,
        grid=(m // tm,),
        in_specs=[pl.BlockSpec((tm, N_HEADS_C), lambda i: (i, 0)), pl.BlockSpec((1, N_HEADS_C), lambda i: (0, 0))],
        out_specs=pl.BlockSpec((tm, N_HEADS_C), lambda i: (i, 0)),
        out_shape=jax.ShapeDtypeStruct((m, N_HEADS_C), F32),
        compiler_params=_cparams(("arbitrary",)),
        name="logf",
    )(cf2d, b_forget.reshape(1, N_HEADS_C))


def _cumsum_kernel(lf_ref, dc_ref, dct_ref, carry_ref):
    tm = lf_ref.shape[1]

    @pl.when(pl.program_id(1) == 0)
    def _():
        carry_ref[...] = jnp.zeros_like(carry_ref)

    lf = lf_ref[0]
    tri = (lax.broadcasted_iota(I32, (tm, tm), 1) <= lax.broadcasted_iota(I32, (tm, tm), 0)).astype(F32)
    dc = jnp.dot(tri, lf, preferred_element_type=F32, precision=lax.Precision.HIGHEST) + carry_ref[...]
    dc_ref[0] = dc
    eye = (lax.broadcasted_iota(I32, (N_HEADS_C, N_HEADS_C), 0)
           == lax.broadcasted_iota(I32, (N_HEADS_C, N_HEADS_C), 1)).astype(F32)
    dct_ref[0, 0] = lax.dot_general(eye, dc, (((1,), (1,)), ((), ())), preferred_element_type=F32,
                                    precision=lax.Precision.HIGHEST)
    carry_ref[...] = dc[tm - 1:tm, :]


def _forget_cumsum(logf_all, kc):
    bn, lp, _ = logf_all.shape
    return pl.pallas_call(
        _cumsum_kernel,
        grid=(bn, lp // kc),
        in_specs=[pl.BlockSpec((1, kc, N_HEADS_C), lambda b, t: (b, t, 0))],
        out_specs=[pl.BlockSpec((1, kc, N_HEADS_C), lambda b, t: (b, t, 0)),
                   pl.BlockSpec((1, 1, N_HEADS_C, kc), lambda b, t: (b, t, 0, 0))],
        out_shape=[jax.ShapeDtypeStruct((bn, lp, N_HEADS_C), F32),
                   jax.ShapeDtypeStruct((bn, lp // kc, N_HEADS_C, kc), F32)],
        scratch_shapes=[pltpu.VMEM((1, N_HEADS_C), F32)],
        compiler_params=_cparams(("arbitrary", "arbitrary")),
        name="forget_cumsum",
    )(logf_all)


def _fox_kernel(q_ref, dq_ref, k_ref, v_ref, dkt_ref, o_ref, *, tq, kc, q_off):
    qb = pl.program_id(1)
    q_base = q_off + qb * tq
    n_c = (q_base + tq + kc - 1) // kc
    lane = lax.broadcasted_iota(I32, (tq, LANES), 1)
    q_pos = q_base + lax.broadcasted_iota(I32, (tq, kc), 0)
    k_loc = lax.broadcasted_iota(I32, (tq, kc), 1)
    dq = dq_ref[0]
    for j in range(N_HEADS_C // 2):
        ls = slice(j * LANES, (j + 1) * LANES)
        pair = q_ref[:, ls] * (HEAD_DIM_C ** -0.5)
        outs = []
        for half in range(2):
            h = 2 * j + half
            qh = jnp.where((lane < HEAD_DIM_C) == (half == 0), pair, 0.0).astype(BF16)
            dqh = dq[:, h:h + 1]

            def body(c, carry, qh=qh, dqh=dqh, h=h, ls=ls):
                m, l, acc = carry
                r0 = pl.multiple_of(c * kc, kc)
                s = _mm_nt(qh, k_ref[0, pl.ds(r0, kc), ls]) + dqh - dkt_ref[0, c, h:h + 1, :]
                s = jnp.where(r0 + k_loc <= q_pos, s, -jnp.inf)
                m_new = jnp.maximum(m, jnp.max(s, axis=1, keepdims=True))
                alpha = jnp.exp(m - m_new)
                p = jnp.exp(s - m_new)
                l = alpha * l + jnp.sum(p, axis=1, keepdims=True)
                acc = alpha * acc + _mm(p.astype(BF16), v_ref[0, pl.ds(r0, kc), ls])
                return m_new, l, acc

            m, l, acc = lax.fori_loop(
                0, n_c, body,
                (jnp.full((tq, 1), -jnp.inf, F32), jnp.zeros((tq, 1), F32), jnp.zeros((tq, LANES), F32)))
            outs.append(acc / l)
        o_ref[:, ls] = jnp.where(lane < HEAD_DIM_C, outs[0], outs[1])


def _fox(cq, dq, k_all, v_all, dkt, *, tq, q_off):
    bn, tq_total, width = cq.shape
    lp = k_all.shape[1]
    kc = dkt.shape[-1]
    return pl.pallas_call(
        functools.partial(_fox_kernel, tq=tq, kc=kc, q_off=q_off),
        grid=(bn, tq_total // tq),
        in_specs=[pl.BlockSpec((None, tq, width), lambda b, q: (b, q, 0)),
                  pl.BlockSpec((1, tq, N_HEADS_C), lambda b, q: (b, q, 0)),
                  pl.BlockSpec((1, lp, width), lambda b, q: (b, 0, 0)),
                  pl.BlockSpec((1, lp, width), lambda b, q: (b, 0, 0)),
                  pl.BlockSpec((1, lp // kc, N_HEADS_C, kc), lambda b, q: (b, 0, 0, 0))],
        out_specs=pl.BlockSpec((None, tq, width), lambda b, q: (b, q, 0)),
        out_shape=jax.ShapeDtypeStruct((bn, tq_total, width), F32),
        compiler_params=_cparams(("arbitrary", "arbitrary")),
        name="fox",
    )(cq, dq, k_all, v_all, dkt)


def _merge_kernel(x_ref, ya_ref, yb_ref, yc_ref, g_ref, wa_ref, wb_ref, wc_ref, wo_ref, lng_ref, lnb_ref,
                  o_ref, *, alpha):
    ba = _mm(ya_ref[...].astype(BF16), wa_ref[...])
    bc = _mm(yc_ref[...].astype(BF16), wc_ref[...])
    glu = _mm(jax.nn.gelu(yb_ref[...]).astype(BF16), wb_ref[...])
    bb = glu[:, :D_MODEL] * jax.nn.sigmoid(glu[:, D_MODEL:])
    g = jax.nn.sigmoid(g_ref[...])
    merged = g[:, :D_MODEL] * ba + g[:, D_MODEL:2 * D_MODEL] * bb + g[:, 2 * D_MODEL:] * bc
    out = _mm(merged.astype(BF16), wo_ref[...])
    o_ref[...] = _layer_norm(alpha * x_ref[...] + out, lng_ref[...], lnb_ref[...])


def _merge(x2d, ya, yb, yb_time_major, yc, gates, wa, wb, wc, wo, ln_g, ln_b, alpha, n_time):
    m = x2d.shape[0]
    tm = min(OUT_TM, m)
    row = lambda w: pl.BlockSpec((tm, w), lambda i: (i, 0))
    const = lambda *s: pl.BlockSpec(s, lambda i: (0,) * len(s))
    if yb_time_major:
        n_t = n_time // tm
        yb_spec = pl.BlockSpec((tm, SSM_WIDTH), lambda i: (i % n_t, i // n_t))
    else:
        yb_spec = row(SSM_WIDTH)
    return pl.pallas_call(
        functools.partial(_merge_kernel, alpha=alpha),
        grid=(m // tm,),
        in_specs=[row(D_MODEL), row(512), yb_spec, row(512), row(N_BRANCH * D_MODEL),
                  const(512, D_MODEL), const(SSM_WIDTH, 2 * D_MODEL), const(512, D_MODEL),
                  const(D_MODEL, D_MODEL), const(1, D_MODEL), const(1, D_MODEL)],
        out_specs=row(D_MODEL),
        out_shape=jax.ShapeDtypeStruct((m, D_MODEL), F32),
        compiler_params=_cparams(("arbitrary",)),
        name="merge",
    )(x2d, ya, yb, yc, gates, wa, wb, wc, wo, ln_g.reshape(1, -1), ln_b.reshape(1, -1))


def _ffn_kernel(x_ref, wg_ref, wu_ref, wd_ref, lng_ref, lnb_ref, o_ref, acc_ref, *, alpha):
    c = pl.program_id(1)

    @pl.when(c == 0)
    def _():
        acc_ref[...] = jnp.zeros_like(acc_ref)

    xb = x_ref[...].astype(BF16)
    a = jax.nn.silu(_mm(xb, wg_ref[...])) * _mm(xb, wu_ref[...])
    acc_ref[...] += _mm(a.astype(BF16), wd_ref[...])

    @pl.when(c == pl.num_programs(1) - 1)
    def _():
        o_ref[...] = _layer_norm(alpha * x_ref[...] + acc_ref[...], lng_ref[...], lnb_ref[...])


def _ffn(x2d, w_gu, w_down, ln_g, ln_b, alpha):
    m = x2d.shape[0]
    tm = min(FFN_TM, m)
    d_ff = w_down.shape[0]
    n_c = d_ff // FFN_FC
    return pl.pallas_call(
        functools.partial(_ffn_kernel, alpha=alpha),
        grid=(m // tm, n_c),
        in_specs=[pl.BlockSpec((tm, D_MODEL), lambda i, c: (i, 0)),
                  pl.BlockSpec((D_MODEL, FFN_FC), lambda i, c: (0, c)),
                  pl.BlockSpec((D_MODEL, FFN_FC), lambda i, c: (0, n_c + c)),
                  pl.BlockSpec((FFN_FC, D_MODEL), lambda i, c: (c, 0)),
                  pl.BlockSpec((1, D_MODEL), lambda i, c: (0, 0)),
                  pl.BlockSpec((1, D_MODEL), lambda i, c: (0, 0))],
        out_specs=pl.BlockSpec((tm, D_MODEL), lambda i, c: (i, 0)),
        out_shape=jax.ShapeDtypeStruct((m, D_MODEL), F32),
        scratch_shapes=[pltpu.VMEM((tm, D_MODEL), F32)],
        compiler_params=_cparams(("arbitrary", "arbitrary")),
        name="ffn",
    )(x2d, w_gu, w_gu, w_down, ln_g.reshape(1, -1), ln_b.reshape(1, -1))


def _moe_kernel(x_ref, wr_ref, br_ref, wgu_ref, wd_ref, lng_ref, lnb_ref, o_ref, acc_ref, gate_ref, *, alpha):
    e = pl.program_id(1)
    tm = x_ref.shape[0]
    xb = x_ref[...].astype(BF16)
    lane = lax.broadcasted_iota(I32, (tm, N_EXPERTS), 1).astype(F32)

    @pl.when(e == 0)
    def _():
        acc_ref[...] = jnp.zeros_like(acc_ref)
        logits = _mm(xb, wr_ref[...]) + br_ref[...]
        m1 = jnp.max(logits, axis=1, keepdims=True)
        i1 = jnp.min(jnp.where(logits == m1, lane, float(N_EXPERTS)), axis=1, keepdims=True)
        rest = jnp.where(lane == i1, -jnp.inf, logits)
        m2 = jnp.max(rest, axis=1, keepdims=True)
        i2 = jnp.min(jnp.where(rest == m2, lane, float(N_EXPERTS)), axis=1, keepdims=True)
        ex = jnp.exp(m2 - m1)
        gate_ref[...] = jnp.where(lane == i1, 1.0 / (1.0 + ex), 0.0) + jnp.where(lane == i2, ex / (1.0 + ex), 0.0)

    ge = jnp.sum(jnp.where(lane == e.astype(F32), gate_ref[...], 0.0), axis=1, keepdims=True)
    h = _mm(xb, wgu_ref[0])
    a = jax.nn.silu(h[:, :D_FF_EXPERT]) * h[:, D_FF_EXPERT:]
    acc_ref[...] += ge * _mm(a.astype(BF16), wd_ref[0])

    @pl.when(e == pl.num_programs(1) - 1)
    def _():
        o_ref[...] = _layer_norm(alpha * x_ref[...] + acc_ref[...], lng_ref[...], lnb_ref[...])


def _moe(x2d, w_router, b_router, w_exp_gu, w_exp_down, ln_g, ln_b, alpha):
    m = x2d.shape[0]
    tm = min(MOE_TM, m)
    return pl.pallas_call(
        functools.partial(_moe_kernel, alpha=alpha),
        grid=(m // tm, N_EXPERTS),
        in_specs=[pl.BlockSpec((tm, D_MODEL), lambda i, e: (i, 0)),
                  pl.BlockSpec((D_MODEL, N_EXPERTS), lambda i, e: (0, 0)),
                  pl.BlockSpec((1, N_EXPERTS), lambda i, e: (0, 0)),
                  pl.BlockSpec((1, D_MODEL, 2 * D_FF_EXPERT), lambda i, e: (e, 0, 0)),
                  pl.BlockSpec((1, D_FF_EXPERT, D_MODEL), lambda i, e: (e, 0, 0)),
                  pl.BlockSpec((1, D_MODEL), lambda i, e: (0, 0)),
                  pl.BlockSpec((1, D_MODEL), lambda i, e: (0, 0))],
        out_specs=pl.BlockSpec((tm, D_MODEL), lambda i, e: (i, 0)),
        out_shape=jax.ShapeDtypeStruct((m, D_MODEL), F32),
        scratch_shapes=[pltpu.VMEM((tm, D_MODEL), F32), pltpu.VMEM((tm, N_EXPERTS), F32)],
        compiler_params=_cparams(("arbitrary", "arbitrary")),
        name="moe",
    )(x2d, w_router, b_router.reshape(1, -1), w_exp_gu, w_exp_down, ln_g.reshape(1, -1), ln_b.reshape(1, -1))


def _pad_rows(a, rows):
    if a.shape[1] == rows:
        return a
    pad = jnp.zeros((a.shape[0], rows - a.shape[1]) + a.shape[2:], a.dtype)
    return jnp.concatenate([a, pad], axis=1)


def _round_up(n, mult):
    return -(-n // mult) * mult


def _token_mixer(x, past, lw):
    bn, n_time, _ = x.shape
    m = bn * n_time
    x2d = x.reshape(m, D_MODEL)
    prompt = past is None
    proj = _project(x2d, lw["w_in"], bn, n_time, time_major_u=prompt)

    kv = proj["kv"].reshape(bn, n_time, 2 * HEAD_DIM_A)
    a_k, a_v = kv[..., :HEAD_DIM_A], kv[..., HEAD_DIM_A:]
    misc = proj["misc"].reshape(bn, n_time, LANES)
    a_ki = misc[..., :IDX_DIM]
    a_w = misc[..., IDX_DIM:IDX_DIM + N_IDX_HEADS]
    c_f = misc[..., IDX_DIM + N_IDX_HEADS:IDX_DIM + N_IDX_HEADS + N_HEADS_C]
    c_k = proj["ck"].reshape(bn, n_time, D_MODEL // 2)
    c_v = proj["cv"].reshape(bn, n_time, D_MODEL // 2)
    logf = _log_forget(c_f.reshape(m, N_HEADS_C), lw["b_forget"]).reshape(bn, n_time, N_HEADS_C)

    if prompt:
        past_len = 0
        ak_all, av_all, aki_all, ck_all, cv_all, logf_all = a_k, a_v, a_ki, c_k, c_v, logf
        h0_re = jnp.zeros((bn, SSM_STATES), F32)
        h0_im = jnp.zeros((bn, SSM_STATES), F32)
        u_tm = proj["u"].reshape(n_time, bn, SSM_WIDTH)
    else:
        p_ak, p_av, p_aki, p_hr, p_hi, p_ck, p_cv, p_logf = past
        past_len = p_ak.shape[1]
        cat = lambda p, n: jnp.concatenate([p.reshape(bn, past_len, -1), n], axis=1)
        ak_all, av_all, aki_all = cat(p_ak, a_k), cat(p_av, a_v), cat(p_aki, a_ki)
        ck_all, cv_all, logf_all = cat(p_ck, c_k), cat(p_cv, c_v), cat(p_logf, logf)
        h0_re = p_hr.reshape(bn, SSM_STATES)
        h0_im = p_hi.reshape(bn, SSM_STATES)
        u_tm = jnp.swapaxes(proj["u"].reshape(bn, n_time, SSM_WIDTH), 0, 1)
    kv_len = past_len + n_time
    n_top = min(TOPK_MAX, kv_len // 4)

    y_b, h_re, h_im = _s5(u_tm, h0_re, h0_im, lw["ab_re"], lw["ab_im"], lw["bd"], lw["cr"], lw["ci"], lw["ssm_d"])
    if prompt:
        y_b = y_b.reshape(n_time, bn * SSM_WIDTH)
    else:
        y_b = jnp.swapaxes(y_b, 0, 1).reshape(m, SSM_WIDTH)

    tq_pad = _round_up(n_time, DSA_TQ)
    lp = _round_up(kv_len, DSA_KC)
    dup = lambda a: _pad_rows(jnp.concatenate([a, a], axis=-1).astype(BF16), lp)
    qa = _pad_rows(proj["qa"].reshape(bn, n_time, -1), tq_pad)
    qit = jnp.swapaxes(_pad_rows(proj["qi"].reshape(bn, n_time, -1), tq_pad), 1, 2)
    wt = jnp.swapaxes(_pad_rows(a_w, tq_pad), 1, 2)
    y_a = _dsa(qa, qit, wt, dup(ak_all), dup(av_all), _pad_rows(aki_all.astype(BF16), lp), lw["bias_tiles"],
               q_off=past_len, kv_len=kv_len, n_top=n_top)[:, :n_time].reshape(m, -1)

    lpc = _round_up(kv_len, FOX_KC)
    dcum, dkt = _forget_cumsum(_pad_rows(logf_all, lpc), FOX_KC)
    y_c = _fox(proj["cq"].reshape(bn, n_time, -1), dcum[:, past_len:kv_len],
               _pad_rows(ck_all.astype(BF16), lpc), _pad_rows(cv_all.astype(BF16), lpc), dkt,
               tq=min(FOX_TQ_PROMPT, n_time), q_off=past_len).reshape(m, -1)

    x1 = _merge(x2d, y_a, y_b, prompt, y_c, proj["gates"], lw["w_a_out"], lw["w_b_glu"], lw["w_c_out"], lw["w_o"],
                lw["ln1_g"], lw["ln1_b"], lw["alpha"], n_time)
    new_state = (a_k, a_v, a_ki, h_re.reshape(bn, N_GROUPS, STATE_DIM), h_im.reshape(bn, N_GROUPS, STATE_DIM),
                 c_k.reshape(bn, n_time, N_HEADS_C, HEAD_DIM_C), c_v.reshape(bn, n_time, N_HEADS_C, HEAD_DIM_C), logf)
    return x1, new_state


def kernel(x_prompt, x_sample, cache_a_k, cache_a_v, cache_a_kidx, state_ssm_re, state_ssm_im, cache_c_k, cache_c_v, cache_c_logf, w_in, b_forget, ssm_lam_re, ssm_lam_im, ssm_log_step, ssm_b_re, ssm_b_im, ssm_c_re, ssm_c_im, ssm_d, w_a_out, w_b_glu, w_c_out, w_o, ln1_g, ln1_b, ln2_g, ln2_b, t5_table, w_ffn_gu, w_ffn_down, w_router, b_router, w_exp_gu, w_exp_down):
    depth = w_in.shape[0]
    alpha = float((2 * depth) ** 0.25)
    bias_tiles = _dsa_bias_tiles(t5_table)
    xp, xs = x_prompt, x_sample
    rows_p = [[] for _ in range(8)]
    rows_s = [[] for _ in range(8)]
    for layer in range(depth):
        ab_re, ab_im, bb_re, bb_im = _s5_discretize(ssm_lam_re[layer], ssm_lam_im[layer], ssm_log_step[layer],
                                                    ssm_b_re[layer], ssm_b_im[layer])
        lw = dict(
            w_in=_pack_w_in(w_in[layer]), b_forget=b_forget[layer],
            ab_re=ab_re.reshape(1, SSM_STATES), ab_im=ab_im.reshape(1, SSM_STATES),
            bd=jnp.concatenate([_block_diag(bb_re), _block_diag(bb_im)], axis=1).astype(BF16),
            cr=_block_diag(jnp.swapaxes(ssm_c_re[layer], 1, 2)).astype(BF16),
            ci=_block_diag(jnp.swapaxes(ssm_c_im[layer], 1, 2)).astype(BF16),
            ssm_d=ssm_d[layer].reshape(1, SSM_WIDTH),
            w_a_out=w_a_out[layer].astype(BF16), w_b_glu=w_b_glu[layer].astype(BF16),
            w_c_out=w_c_out[layer].astype(BF16), w_o=w_o[layer].astype(BF16),
            ln1_g=ln1_g[layer], ln1_b=ln1_b[layer], bias_tiles=bias_tiles, alpha=alpha)
        past = (cache_a_k[layer], cache_a_v[layer], cache_a_kidx[layer], state_ssm_re[layer],
                state_ssm_im[layer], cache_c_k[layer], cache_c_v[layer], cache_c_logf[layer])
        xp1, st_p = _token_mixer(xp, None, lw)
        xs1, st_s = _token_mixer(xs, past, lw)
        i = layer // 2
        if layer % 2 == 0:
            wgu, wdn = w_ffn_gu[i].astype(BF16), w_ffn_down[i].astype(BF16)
            xp2 = _ffn(xp1, wgu, wdn, ln2_g[layer], ln2_b[layer], alpha)
            xs2 = _ffn(xs1, wgu, wdn, ln2_g[layer], ln2_b[layer], alpha)
        else:
            wr, wgu, wdn = w_router[i].astype(BF16), w_exp_gu[i].astype(BF16), w_exp_down[i].astype(BF16)
            xp2 = _moe(xp1, wr, b_router[i], wgu, wdn, ln2_g[layer], ln2_b[layer], alpha)
            xs2 = _moe(xs1, wr, b_router[i], wgu, wdn, ln2_g[layer], ln2_b[layer], alpha)
        xp = xp2.reshape(x_prompt.shape)
        xs = xs2.reshape(x_sample.shape)
        for j in range(8):
            rows_p[j].append(st_p[j])
            rows_s[j].append(st_s[j])
    (a_k_p, a_v_p, a_kidx_p, ssm_re_p, ssm_im_p, c_k_p, c_v_p, c_logf_p) = [jnp.stack(r) for r in rows_p]
    (a_k_s, a_v_s, a_kidx_s, ssm_re_s, ssm_im_s, c_k_s, c_v_s, c_logf_s) = [jnp.stack(r) for r in rows_s]
    return (xp, xs, a_k_p, a_k_s, a_v_p, a_v_s, a_kidx_p, a_kidx_s, ssm_re_p, ssm_re_s,
            ssm_im_p, ssm_im_s, c_k_p, c_k_s, c_v_p, c_v_s, c_logf_p, c_logf_s)
```

```python
import functools
import math

import jax
import jax.numpy as jnp
import numpy as np
from jax import lax
from jax.experimental import pallas as pl
from jax.experimental.pallas import tpu as pltpu

F32 = jnp.float32
BF16 = jnp.bfloat16
I32 = jnp.int32

D_MODEL = 1024
CHUNK = 64
N_HEADS_A = 8
HEAD_DIM_A = 64
N_IDX_HEADS = 8
IDX_DIM = 32
TOPK_MAX = 256
NUM_BUCKETS = 32
MAX_DISTANCE = 128
SSM_WIDTH = 512
GROUP_SIZE = 16
N_GROUPS = SSM_WIDTH // GROUP_SIZE
STATE_DIM = 64
SSM_STATES = N_GROUPS * STATE_DIM
N_HEADS_C = 8
HEAD_DIM_C = 64
N_BRANCH = 3
D_FF = 2816
N_EXPERTS = 8
D_FF_EXPERT = 1408
LN_EPS = 1e-5
PROJ_SIZES = (512, 64, 64, 256, 32, 8, 512, 512, 512, 512, 8, 3072)

LANES = 128
SUBLANES = 8
VMEM_LIMIT_BYTES = 56 * 1024 * 1024

PROJ_TM = 256
S5_ROWS = 512
S5_CW = 256
DSA_TQ = 128
DSA_KC = 256
FOX_TQ_PROMPT = 256
FOX_KC = 256
OUT_TM = 256
FFN_TM = 1024
FFN_FC = 256
MOE_TM = 512

INT_MIN = -(2 ** 31)
SKEY_NEG_INF = -2139095041


def _cparams(sem):
    return pltpu.CompilerParams(dimension_semantics=sem, vmem_limit_bytes=VMEM_LIMIT_BYTES)


def _mm(a, b):
    return jnp.dot(a, b, preferred_element_type=F32)


def _mm_nt(a, b):
    return lax.dot_general(a, b, (((1,), (1,)), ((), ())), preferred_element_type=F32)


def _layer_norm(z, g, b):
    mu = jnp.mean(z, axis=-1, keepdims=True)
    zc = z - mu
    var = jnp.mean(zc * zc, axis=-1, keepdims=True)
    return zc * lax.rsqrt(var + LN_EPS) * g + b


def _pack_w_in(w):
    offs = np.cumsum((0,) + PROJ_SIZES)
    a_q, a_k, a_v, a_qi, a_ki, a_w, b_u, c_q, c_k, c_v, c_f, gates = [
        w[:, offs[i]:offs[i + 1]] for i in range(len(PROJ_SIZES))]
    pad = jnp.zeros((w.shape[0], LANES - IDX_DIM - N_IDX_HEADS - N_HEADS_C), w.dtype)
    return jnp.concatenate([a_q, a_k, a_v, a_qi, a_ki, a_w, c_f, pad, b_u, c_q, c_k, c_v, gates],
                           axis=1).astype(BF16)


PROJ_OUT = (("qa", 0, 512), ("kv", 512, 128), ("qi", 640, 256), ("misc", 896, 128), ("u", 1024, 512),
            ("cq", 1536, 512), ("ck", 2048, 512), ("cv", 2560, 512), ("gates", 3072, 3072))
PROJ_COLS_PACKED = 6144


def _proj_kernel(x_ref, w_ref, *out_refs):
    xb = x_ref[...].astype(BF16)
    for (_, lo, width), o_ref in zip(PROJ_OUT, out_refs):
        for c in range(0, width, 512):
            cw = min(512, width - c)
            o_ref[:, c:c + cw] = _mm(xb, w_ref[:, lo + c:lo + c + cw])


def _project(x2d, w_packed, n_batch, n_time, time_major_u):
    m = x2d.shape[0]
    tm = min(PROJ_TM, m)
    n_t = n_time // tm if time_major_u else 1
    out_shape, out_specs = [], []
    for name, _, width in PROJ_OUT:
        if name == "u" and time_major_u:
            out_shape.append(jax.ShapeDtypeStruct((n_time, n_batch * width), F32))
            out_specs.append(pl.BlockSpec((tm, width), lambda i: (i % n_t, i // n_t)))
        else:
            out_shape.append(jax.ShapeDtypeStruct((m, width), F32))
            out_specs.append(pl.BlockSpec((tm, width), lambda i: (i, 0)))
    outs = pl.pallas_call(
        _proj_kernel,
        grid=(m // tm,),
        in_specs=[pl.BlockSpec((tm, D_MODEL), lambda i: (i, 0)),
                  pl.BlockSpec((D_MODEL, PROJ_COLS_PACKED), lambda i: (0, 0))],
        out_specs=out_specs,
        out_shape=out_shape,
        compiler_params=_cparams(("arbitrary",)),
        name="proj",
    )(x2d, w_packed)
    return dict(zip([p[0] for p in PROJ_OUT], outs))


def _s5_disc_kernel(lr_ref, li_ref, ls_ref, br_ref, bi_ref, ar_ref, ai_ref, bbr_ref, bbi_ref):
    lr, li = lr_ref[...], li_ref[...]
    dt = jnp.exp(ls_ref[...])
    mag = jnp.exp(lr * dt)
    ab_re = mag * jnp.cos(li * dt)
    ab_im = mag * jnp.sin(li * dt)
    den = lr * lr + li * li
    fr = ((ab_re - 1.0) * lr + ab_im * li) / den
    fi = (ab_im * lr - (ab_re - 1.0) * li) / den
    ar_ref[...] = ab_re
    ai_ref[...] = ab_im
    br, bi = br_ref[...], bi_ref[...]
    bbr_ref[...] = fr[:, None, :] * br - fi[:, None, :] * bi
    bbi_ref[...] = fr[:, None, :] * bi + fi[:, None, :] * br


def _s5_discretize(lam_re, lam_im, log_step, b_re, b_im):
    g, p, gs = b_re.shape
    shp = lambda *s: jax.ShapeDtypeStruct(s, F32)
    return pl.pallas_call(
        _s5_disc_kernel,
        out_shape=[shp(g, p), shp(g, p), shp(g, gs, p), shp(g, gs, p)],
        name="s5_disc",
    )(lam_re, lam_im, log_step.reshape(g, 1), jnp.swapaxes(b_re, 1, 2), jnp.swapaxes(b_im, 1, 2))


def _block_diag(blocks):
    g, r, c = blocks.shape
    eye = jnp.eye(g, dtype=bool)
    return jnp.where(eye[:, None, :, None], blocks[:, :, None, :], 0.0).reshape(g * r, g * c)


def _s5_kernel(u_ref, h0r_ref, h0i_ref, ar_ref, ai_ref, bd_ref, cr_ref, ci_ref, d_ref,
               y_ref, hr_ref, hi_ref, bur_ref, bui_ref, *, tc, bn):
    rows = tc * bn

    @pl.when(pl.program_id(0) == 0)
    def _():
        hr_ref[...] = h0r_ref[...]
        hi_ref[...] = h0i_ref[...]

    u = u_ref[...].reshape(rows, SSM_WIDTH)
    ub = u.astype(BF16)
    bur_ref[...] = _mm(ub, bd_ref[:, :SSM_STATES])
    bui_ref[...] = _mm(ub, bd_ref[:, SSM_STATES:])

    for cc in range(SSM_STATES // S5_CW):
        sl = slice(cc * S5_CW, (cc + 1) * S5_CW)
        ar = jnp.broadcast_to(ar_ref[:, sl], (bn, S5_CW))
        ai = jnp.broadcast_to(ai_ref[:, sl], (bn, S5_CW))

        def step(t, carry, sl=sl, ar=ar, ai=ai):
            hr, hi = carry
            r0 = pl.multiple_of(t * bn, bn)
            nr = ar * hr - ai * hi + bur_ref[pl.ds(r0, bn), sl]
            ni = ar * hi + ai * hr + bui_ref[pl.ds(r0, bn), sl]
            bur_ref[pl.ds(r0, bn), sl] = nr
            bui_ref[pl.ds(r0, bn), sl] = ni
            return nr, ni

        hr, hi = lax.fori_loop(0, tc, step, (hr_ref[:, sl], hi_ref[:, sl]))
        hr_ref[:, sl] = hr
        hi_ref[:, sl] = hi

    y = (_mm(bur_ref[...].astype(BF16), cr_ref[...]) - _mm(bui_ref[...].astype(BF16), ci_ref[...])
         + d_ref[...] * u)
    y_ref[...] = y.reshape(tc, bn, SSM_WIDTH)


def _s5(u_tm, h0_re, h0_im, ab_re, ab_im, bd, cr, ci, d_skip):
    n_time, bn, _ = u_tm.shape
    tc = min(S5_ROWS // bn, n_time)
    const = lambda *s: pl.BlockSpec(s, lambda i: (0,) * len(s))
    y, hr, hi = pl.pallas_call(
        functools.partial(_s5_kernel, tc=tc, bn=bn),
        grid=(n_time // tc,),
        in_specs=[pl.BlockSpec((tc, bn, SSM_WIDTH), lambda i: (i, 0, 0)),
                  const(bn, SSM_STATES), const(bn, SSM_STATES),
                  const(1, SSM_STATES), const(1, SSM_STATES),
                  const(SSM_WIDTH, 2 * SSM_STATES),
                  const(SSM_STATES, SSM_WIDTH), const(SSM_STATES, SSM_WIDTH),
                  const(1, SSM_WIDTH)],
        out_specs=[pl.BlockSpec((tc, bn, SSM_WIDTH), lambda i: (i, 0, 0)),
                   const(bn, SSM_STATES), const(bn, SSM_STATES)],
        out_shape=[jax.ShapeDtypeStruct((n_time, bn, SSM_WIDTH), F32),
                   jax.ShapeDtypeStruct((bn, SSM_STATES), F32),
                   jax.ShapeDtypeStruct((bn, SSM_STATES), F32)],
        scratch_shapes=[pltpu.VMEM((tc * bn, SSM_STATES), F32), pltpu.VMEM((tc * bn, SSM_STATES), F32)],
        compiler_params=_cparams(("arbitrary",)),
        name="s5",
    )(u_tm, h0_re, h0_im, ab_re, ab_im, bd, cr, ci, d_skip)
    return y, hr, hi


def _t5_bucket(rel):
    half = NUM_BUCKETS // 2
    max_exact = half // 2
    n = jnp.abs(rel)
    large = max_exact + (jnp.log(jnp.maximum(n, 1).astype(F32) / max_exact)
                         / math.log(MAX_DISTANCE / max_exact) * (half - max_exact)).astype(I32)
    large = jnp.minimum(large, half - 1)
    return jnp.where(rel > 0, half, 0) + jnp.where(n < max_exact, n, large)


DSA_BIAS_TILES = 4


def _dsa_bias_tiles(t5_table):
    q = jnp.arange(DSA_TQ, dtype=I32)[:, None]
    k = jnp.arange(DSA_KC, dtype=I32)[None, :]
    tiles = []
    for e in range(DSA_BIAS_TILES):
        rel = k - q - LANES * e if e < DSA_BIAS_TILES - 1 else jnp.full((DSA_TQ, DSA_KC), -(1 << 20), I32)
        tiles.append(jnp.moveaxis(t5_table[_t5_bucket(rel)], -1, 0))
    return jnp.stack(tiles).reshape(DSA_BIAS_TILES, N_HEADS_A * DSA_TQ, DSA_KC).astype(F32)


def _sort_key(s):
    bits = lax.bitcast_convert_type(s, I32)
    return bits ^ (lax.shift_right_arithmetic(bits, 31) & 0x7FFFFFFF)


def _dsa_kernel(qa_ref, qit_ref, wt_ref, k2_ref, v2_ref, ki_ref, bias_ref, o_ref, skey_ref, tie_ref,
                *, q_off, kv_len, n_top, idx_bits):
    tq, kc = DSA_TQ, DSA_KC
    qb = pl.program_id(1)
    q_base = q_off + qb * tq
    adm_end = jnp.minimum(((q_base + tq - 1) // CHUNK + 1) * CHUNK, kv_len)
    n_c = (adm_end + kc - 1) // kc

    q_pos_t = q_base + lax.broadcasted_iota(I32, (kc, tq), 1)
    k_loc_t = lax.broadcasted_iota(I32, (kc, tq), 0)

    qit = qit_ref[0].astype(BF16)
    w = wt_ref[0]

    def score_chunk(c, _):
        r0 = pl.multiple_of(c * kc, kc)
        kic = ki_ref[0, pl.ds(r0, kc), :]
        acc = jnp.zeros((kc, tq), F32)
        for h in range(N_IDX_HEADS):
            d = _mm(kic, qit[h * IDX_DIM:(h + 1) * IDX_DIM, :])
            acc = acc + w[h:h + 1, :] * jnp.maximum(d, 0.0)
        acc = jnp.where(acc == 0.0, 0.0, acc)
        k_pos = r0 + k_loc_t
        adm = ((k_pos // CHUNK) <= (q_pos_t // CHUNK)) & (k_pos < kv_len)
        skey_ref[pl.ds(r0, kc), :] = _sort_key(jnp.where(adm, acc, -jnp.inf))
        return 0

    lax.fori_loop(0, n_c, score_chunk, 0)

    def count(pred_fn):
        def body(c, cnt):
            r0 = pl.multiple_of(c * kc, kc)
            sk = skey_ref[pl.ds(r0, kc), :]
            return cnt + jnp.sum(jnp.where(pred_fn(sk, r0 + k_loc_t), 1.0, 0.0), axis=0, keepdims=True)
        return lax.fori_loop(0, n_c, body, jnp.zeros((1, tq), F32))

    def value_bit(i, t):
        cand = t ^ lax.shift_left(jnp.int32(1), 31 - i)
        cnt = count(lambda sk, _: sk >= cand)
        return jnp.where(cnt >= n_top, cand, t)

    thr = lax.fori_loop(0, 32, value_bit, jnp.full((1, tq), INT_MIN, I32))
    cnt_gt = count(lambda sk, _: sk > thr)
    cnt_ge = count(lambda sk, _: sk >= thr)
    need = n_top - cnt_gt

    tie_ref[...] = jnp.full((1, tq), 1 << 30, I32)
    has_tie = (cnt_ge > n_top) & (thr > SKEY_NEG_INF)

    @pl.when(jnp.max(jnp.where(has_tie, 1.0, 0.0)) > 0.5)
    def _():
        def index_bit(i, j):
            cand = j | lax.shift_left(jnp.int32(1), idx_bits - 1 - i)
            cnt = count(lambda sk, kp: (sk == thr) & (kp < cand))
            return jnp.where(cnt <= need - 1.0, cand, j)
        tie_ref[...] = lax.fori_loop(0, idx_bits, index_bit, jnp.zeros((1, tq), I32))

    tie_idx = tie_ref[...]

    lane = lax.broadcasted_iota(I32, (tq, LANES), 1)
    q_rows = []
    for h in range(N_HEADS_A):
        pair = qa_ref[:, (h // 2) * LANES:(h // 2 + 1) * LANES]
        q_rows.append(jnp.where((lane < HEAD_DIM_A) == (h % 2 == 0), pair, 0.0))
    q_all = (jnp.concatenate(q_rows, axis=0) * (HEAD_DIM_A ** -0.5)).astype(BF16)

    def attend(c, carry):
        m, l, acc = carry
        r0 = pl.multiple_of(c * kc, kc)
        sk = skey_ref[pl.ds(r0, kc), :]
        k_pos = r0 + k_loc_t
        sel_t = ((sk > thr) | ((sk == thr) & (k_pos <= tie_idx))) & (sk > SKEY_NEG_INF)
        sel = jnp.where(sel_t, 1.0, 0.0).T
        s = _mm_nt(q_all, k2_ref[0, pl.ds(r0, kc), :])
        tile = jnp.clip(q_base // LANES - c * (kc // LANES), 0, DSA_BIAS_TILES - 1)
        s = s + bias_ref[tile]
        s = jnp.where(sel[None] > 0.5, s.reshape(N_HEADS_A, tq, kc), -jnp.inf).reshape(N_HEADS_A * tq, kc)
        m_new = jnp.maximum(m, jnp.max(s, axis=1, keepdims=True))
        m_safe = jnp.where(m_new == -jnp.inf, 0.0, m_new)
        alpha = jnp.exp(m - m_safe)
        p = jnp.exp(s - m_safe)
        l = alpha * l + jnp.sum(p, axis=1, keepdims=True)
        acc = alpha * acc + _mm(p.astype(BF16), v2_ref[0, pl.ds(r0, kc), :])
        return m_new, l, acc

    rows = N_HEADS_A * tq
    m, l, acc = lax.fori_loop(
        0, n_c, attend,
        (jnp.full((rows, 1), -jnp.inf, F32), jnp.zeros((rows, 1), F32), jnp.zeros((rows, LANES), F32)))
    out = acc / l
    for j in range(N_HEADS_A // 2):
        o_ref[:, j * LANES:(j + 1) * LANES] = jnp.where(
            lane < HEAD_DIM_A, out[(2 * j) * tq:(2 * j + 1) * tq], out[(2 * j + 1) * tq:(2 * j + 2) * tq])


def _dsa(qa, qit, wt, k2, v2, ki, bias_tiles, *, q_off, kv_len, n_top):
    bn, tq_total, _ = qa.shape
    lp = k2.shape[1]
    nq = tq_total // DSA_TQ
    idx_bits = max(1, int(lp - 1).bit_length())
    return pl.pallas_call(
        functools.partial(_dsa_kernel, q_off=q_off, kv_len=kv_len, n_top=float(n_top), idx_bits=idx_bits),
        grid=(bn, nq),
        in_specs=[pl.BlockSpec((None, DSA_TQ, N_HEADS_A * HEAD_DIM_A), lambda b, q: (b, q, 0)),
                  pl.BlockSpec((1, N_IDX_HEADS * IDX_DIM, DSA_TQ), lambda b, q: (b, 0, q)),
                  pl.BlockSpec((1, N_IDX_HEADS, DSA_TQ), lambda b, q: (b, 0, q)),
                  pl.BlockSpec((1, lp, LANES), lambda b, q: (b, 0, 0)),
                  pl.BlockSpec((1, lp, LANES), lambda b, q: (b, 0, 0)),
                  pl.BlockSpec((1, lp, IDX_DIM), lambda b, q: (b, 0, 0)),
                  pl.BlockSpec((DSA_BIAS_TILES, N_HEADS_A * DSA_TQ, DSA_KC), lambda b, q: (0, 0, 0))],
        out_specs=pl.BlockSpec((None, DSA_TQ, N_HEADS_A * HEAD_DIM_A), lambda b, q: (b, q, 0)),
        out_shape=jax.ShapeDtypeStruct((bn, tq_total, N_HEADS_A * HEAD_DIM_A), F32),
        scratch_shapes=[pltpu.VMEM((lp, DSA_TQ), I32), pltpu.VMEM((1, DSA_TQ), I32)],
        compiler_params=_cparams(("arbitrary", "arbitrary")),
        name="dsa",
    )(qa, qit, wt, k2, v2, ki, bias_tiles)


def _logf_kernel(cf_ref, b_ref, o_ref):
    o_ref[...] = jax.nn.log_sigmoid(cf_ref[...] + b_ref[...])


def _log_forget(cf2d, b_forget):
    m = cf2d.shape[0]
    tm = min(2048, m)
    return pl.pallas_call(
        _logf_kernel,
        grid=(m // tm,),
        in_specs=[pl.BlockSpec((tm, N_HEADS_C), lambda i: (i, 0)), pl.BlockSpec((1, N_HEADS_C), lambda i: (0, 0))],
        out_specs=pl.BlockSpec((tm, N_HEADS_C), lambda i: (i, 0)),
        out_shape=jax.ShapeDtypeStruct((m, N_HEADS_C), F32),
        compiler_params=_cparams(("arbitrary",)),
        name="logf",
    )(cf2d, b_forget.reshape(1, N_HEADS_C))


FOX_EXT_ROWS = 16
FOX_SPLIT = 3


def _split_bf16(x):
    parts = []
    rem = x
    for _ in range(FOX_SPLIT):
        p = rem.astype(BF16).astype(F32)
        parts.append(p)
        rem = rem - p
    return parts


def _cumsum_kernel(lf_ref, dc_ref, kext_ref, carry_ref):
    tm = lf_ref.shape[1]

    @pl.when(pl.program_id(1) == 0)
    def _():
        carry_ref[...] = jnp.zeros_like(carry_ref)

    lf = lf_ref[0]
    tri = (lax.broadcasted_iota(I32, (tm, tm), 1) <= lax.broadcasted_iota(I32, (tm, tm), 0)).astype(F32)
    dc = jnp.dot(tri, lf, preferred_element_type=F32, precision=lax.Precision.HIGHEST) + carry_ref[...]
    dc_ref[0] = dc
    carry_ref[...] = dc[tm - 1:tm, :]
    eye = (lax.broadcasted_iota(I32, (N_HEADS_C, N_HEADS_C), 0)
           == lax.broadcasted_iota(I32, (N_HEADS_C, N_HEADS_C), 1)).astype(F32)
    dct = lax.dot_general(eye, dc, (((1,), (1,)), ((), ())), preferred_element_type=F32,
                          precision=lax.Precision.HIGHEST)
    neg = _split_bf16(-dct)
    row = lax.broadcasted_iota(I32, (FOX_EXT_ROWS, tm), 0)
    for h in range(N_HEADS_C):
        tile = jnp.where(row < FOX_SPLIT, 1.0, 0.0)
        for i in range(FOX_SPLIT):
            tile = jnp.where(row == FOX_SPLIT + i, neg[i][h:h + 1, :], tile)
        kext_ref[0, 0, h] = tile.astype(BF16)


def _forget_cumsum(logf_all, kc):
    bn, lp, _ = logf_all.shape
    return pl.pallas_call(
        _cumsum_kernel,
        grid=(bn, lp // kc),
        in_specs=[pl.BlockSpec((1, kc, N_HEADS_C), lambda b, t: (b, t, 0))],
        out_specs=[pl.BlockSpec((1, kc, N_HEADS_C), lambda b, t: (b, t, 0)),
                   pl.BlockSpec((1, 1, N_HEADS_C, FOX_EXT_ROWS, kc), lambda b, t: (b, t, 0, 0, 0))],
        out_shape=[jax.ShapeDtypeStruct((bn, lp, N_HEADS_C), F32),
                   jax.ShapeDtypeStruct((bn, lp // kc, N_HEADS_C, FOX_EXT_ROWS, kc), BF16)],
        scratch_shapes=[pltpu.VMEM((1, N_HEADS_C), F32)],
        compiler_params=_cparams(("arbitrary", "arbitrary")),
        name="forget_cumsum",
    )(logf_all)


def _fox_kernel(q_ref, dq_ref, kt_ref, kext_ref, v_ref, o_ref, qx_ref, m_ref, l_ref, acc_ref, *, tq, kc, q_off):
    qb = pl.program_id(1)
    q_base = q_off + qb * tq
    n_c = (q_base + tq + kc - 1) // kc
    n_full = (q_base + 1) // kc
    lane = lax.broadcasted_iota(I32, (tq, LANES), 1)

    dq = dq_ref[0]
    for h in range(N_HEADS_C):
        pair = q_ref[:, (h // 2) * LANES:(h // 2 + 1) * LANES] * (HEAD_DIM_C ** -0.5)
        if h % 2:
            pair = pltpu.roll(pair, HEAD_DIM_C, axis=1)
        ext = jnp.where((lane >= HEAD_DIM_C + FOX_SPLIT) & (lane < HEAD_DIM_C + 2 * FOX_SPLIT), 1.0, 0.0)
        for i, part in enumerate(_split_bf16(dq[:, h:h + 1])):
            ext = jnp.where(lane == HEAD_DIM_C + i, part, ext)
        qx_ref[h] = jnp.where(lane < HEAD_DIM_C, pair, ext).astype(BF16)
    m_ref[...] = jnp.full(m_ref.shape, -jnp.inf, F32)
    l_ref[...] = jnp.zeros(l_ref.shape, F32)
    acc_ref[...] = jnp.zeros(acc_ref.shape, F32)

    q_pos = q_base + lax.broadcasted_iota(I32, (tq, kc), 0)
    k_loc = lax.broadcasted_iota(I32, (tq, kc), 1)
    zrows = jnp.zeros((LANES - HEAD_DIM_C - FOX_EXT_ROWS, kc), BF16)

    def chunk(c, masked):
        r0 = pl.multiple_of(c * kc, kc)
        for h in range(N_HEADS_C):
            kx = jnp.concatenate([kt_ref[0, c, h * HEAD_DIM_C:(h + 1) * HEAD_DIM_C, :], kext_ref[0, c, h], zrows],
                                 axis=0)
            s = _mm(qx_ref[h], kx)
            if masked:
                s = jnp.where(r0 + k_loc <= q_pos, s, -jnp.inf)
            m_old = m_ref[h]
            m_new = jnp.maximum(m_old, jnp.max(s, axis=1, keepdims=True))
            alpha = jnp.exp(m_old - m_new)
            p = jnp.exp(s - m_new)
            l_ref[h] = alpha * l_ref[h] + jnp.sum(p, axis=1, keepdims=True)
            pv = _mm(p.astype(BF16), v_ref[0, pl.ds(r0, kc), (h // 2) * LANES:(h // 2 + 1) * LANES])
            acc_ref[h] = alpha * acc_ref[h] + pv
            m_ref[h] = m_new

    def full_body(c, carry):
        chunk(c, False)
        return carry

    def diag_body(c, carry):
        chunk(c, True)
        return carry

    lax.fori_loop(0, n_full, full_body, 0)
    lax.fori_loop(n_full, n_c, diag_body, 0)

    for j in range(N_HEADS_C // 2):
        even = acc_ref[2 * j] / l_ref[2 * j]
        odd = acc_ref[2 * j + 1] / l_ref[2 * j + 1]
        o_ref[:, j * LANES:(j + 1) * LANES] = jnp.where(lane < HEAD_DIM_C, even, odd)


def _fox(cq, dq, kt, kext, v_all, *, tq, q_off):
    bn, tq_total, width = cq.shape
    _, n_chunks, _, kc = kt.shape
    lp = n_chunks * kc
    return pl.pallas_call(
        functools.partial(_fox_kernel, tq=tq, kc=kc, q_off=q_off),
        grid=(bn, tq_total // tq),
        in_specs=[pl.BlockSpec((None, tq, width), lambda b, q: (b, q, 0)),
                  pl.BlockSpec((1, tq, N_HEADS_C), lambda b, q: (b, q, 0)),
                  pl.BlockSpec((1, n_chunks, width, kc), lambda b, q: (b, 0, 0, 0)),
                  pl.BlockSpec((1, n_chunks, N_HEADS_C, FOX_EXT_ROWS, kc), lambda b, q: (b, 0, 0, 0, 0)),
                  pl.BlockSpec((1, lp, width), lambda b, q: (b, 0, 0))],
        out_specs=pl.BlockSpec((None, tq, width), lambda b, q: (b, q, 0)),
        out_shape=jax.ShapeDtypeStruct((bn, tq_total, width), F32),
        scratch_shapes=[pltpu.VMEM((N_HEADS_C, tq, LANES), BF16),
                        pltpu.VMEM((N_HEADS_C, tq, 1), F32),
                        pltpu.VMEM((N_HEADS_C, tq, 1), F32),
                        pltpu.VMEM((N_HEADS_C, tq, LANES), F32)],
        compiler_params=_cparams(("arbitrary", "arbitrary")),
        name="fox",
    )(cq, dq, kt, kext, v_all)


def _merge_kernel(x_ref, ya_ref, yb_ref, yc_ref, g_ref, wa_ref, wb_ref, wc_ref, wo_ref, lng_ref, lnb_ref,
                  o_ref, *, alpha):
    ba = _mm(ya_ref[...].astype(BF16), wa_ref[...])
    bc = _mm(yc_ref[...].astype(BF16), wc_ref[...])
    glu = _mm(jax.nn.gelu(yb_ref[...]).astype(BF16), wb_ref[...])
    bb = glu[:, :D_MODEL] * jax.nn.sigmoid(glu[:, D_MODEL:])
    g = jax.nn.sigmoid(g_ref[...])
    merged = g[:, :D_MODEL] * ba + g[:, D_MODEL:2 * D_MODEL] * bb + g[:, 2 * D_MODEL:] * bc
    out = _mm(merged.astype(BF16), wo_ref[...])
    o_ref[...] = _layer_norm(alpha * x_ref[...] + out, lng_ref[...], lnb_ref[...])


def _merge(x2d, ya, yb, yb_time_major, yc, gates, wa, wb, wc, wo, ln_g, ln_b, alpha, n_time):
    m = x2d.shape[0]
    tm = min(OUT_TM, m)
    row = lambda w: pl.BlockSpec((tm, w), lambda i: (i, 0))
    const = lambda *s: pl.BlockSpec(s, lambda i: (0,) * len(s))
    if yb_time_major:
        n_t = n_time // tm
        yb_spec = pl.BlockSpec((tm, SSM_WIDTH), lambda i: (i % n_t, i // n_t))
    else:
        yb_spec = row(SSM_WIDTH)
    return pl.pallas_call(
        functools.partial(_merge_kernel, alpha=alpha),
        grid=(m // tm,),
        in_specs=[row(D_MODEL), row(512), yb_spec, row(512), row(N_BRANCH * D_MODEL),
                  const(512, D_MODEL), const(SSM_WIDTH, 2 * D_MODEL), const(512, D_MODEL),
                  const(D_MODEL, D_MODEL), const(1, D_MODEL), const(1, D_MODEL)],
        out_specs=row(D_MODEL),
        out_shape=jax.ShapeDtypeStruct((m, D_MODEL), F32),
        compiler_params=_cparams(("arbitrary",)),
        name="merge",
    )(x2d, ya, yb, yc, gates, wa, wb, wc, wo, ln_g.reshape(1, -1), ln_b.reshape(1, -1))


def _ffn_kernel(x_ref, wg_ref, wu_ref, wd_ref, lng_ref, lnb_ref, o_ref, acc_ref, *, alpha):
    c = pl.program_id(1)

    @pl.when(c == 0)
    def _():
        acc_ref[...] = jnp.zeros_like(acc_ref)

    xb = x_ref[...].astype(BF16)
    a = jax.nn.silu(_mm(xb, wg_ref[...])) * _mm(xb, wu_ref[...])
    acc_ref[...] += _mm(a.astype(BF16), wd_ref[...])

    @pl.when(c == pl.num_programs(1) - 1)
    def _():
        o_ref[...] = _layer_norm(alpha * x_ref[...] + acc_ref[...], lng_ref[...], lnb_ref[...])


def _ffn(x2d, w_gu, w_down, ln_g, ln_b, alpha):
    m = x2d.shape[0]
    tm = min(FFN_TM, m)
    d_ff = w_down.shape[0]
    n_c = d_ff // FFN_FC
    return pl.pallas_call(
        functools.partial(_ffn_kernel, alpha=alpha),
        grid=(m // tm, n_c),
        in_specs=[pl.BlockSpec((tm, D_MODEL), lambda i, c: (i, 0)),
                  pl.BlockSpec((D_MODEL, FFN_FC), lambda i, c: (0, c)),
                  pl.BlockSpec((D_MODEL, FFN_FC), lambda i, c: (0, n_c + c)),
                  pl.BlockSpec((FFN_FC, D_MODEL), lambda i, c: (c, 0)),
                  pl.BlockSpec((1, D_MODEL), lambda i, c: (0, 0)),
                  pl.BlockSpec((1, D_MODEL), lambda i, c: (0, 0))],
        out_specs=pl.BlockSpec((tm, D_MODEL), lambda i, c: (i, 0)),
        out_shape=jax.ShapeDtypeStruct((m, D_MODEL), F32),
        scratch_shapes=[pltpu.VMEM((tm, D_MODEL), F32)],
        compiler_params=_cparams(("arbitrary", "arbitrary")),
        name="ffn",
    )(x2d, w_gu, w_gu, w_down, ln_g.reshape(1, -1), ln_b.reshape(1, -1))


def _moe_kernel(x_ref, wr_ref, br_ref, wgu_ref, wd_ref, lng_ref, lnb_ref, o_ref, acc_ref, gate_ref, *, alpha):
    e = pl.program_id(1)
    tm = x_ref.shape[0]
    xb = x_ref[...].astype(BF16)
    lane = lax.broadcasted_iota(I32, (tm, N_EXPERTS), 1).astype(F32)

    @pl.when(e == 0)
    def _():
        acc_ref[...] = jnp.zeros_like(acc_ref)
        logits = _mm(xb, wr_ref[...]) + br_ref[...]
        m1 = jnp.max(logits, axis=1, keepdims=True)
        i1 = jnp.min(jnp.where(logits == m1, lane, float(N_EXPERTS)), axis=1, keepdims=True)
        rest = jnp.where(lane == i1, -jnp.inf, logits)
        m2 = jnp.max(rest, axis=1, keepdims=True)
        i2 = jnp.min(jnp.where(rest == m2, lane, float(N_EXPERTS)), axis=1, keepdims=True)
        ex = jnp.exp(m2 - m1)
        gate_ref[...] = jnp.where(lane == i1, 1.0 / (1.0 + ex), 0.0) + jnp.where(lane == i2, ex / (1.0 + ex), 0.0)

    ge = jnp.sum(jnp.where(lane == e.astype(F32), gate_ref[...], 0.0), axis=1, keepdims=True)
    h = _mm(xb, wgu_ref[0])
    a = jax.nn.silu(h[:, :D_FF_EXPERT]) * h[:, D_FF_EXPERT:]
    acc_ref[...] += ge * _mm(a.astype(BF16), wd_ref[0])

    @pl.when(e == pl.num_programs(1) - 1)
    def _():
        o_ref[...] = _layer_norm(alpha * x_ref[...] + acc_ref[...], lng_ref[...], lnb_ref[...])


def _moe(x2d, w_router, b_router, w_exp_gu, w_exp_down, ln_g, ln_b, alpha):
    m = x2d.shape[0]
    tm = min(MOE_TM, m)
    return pl.pallas_call(
        functools.partial(_moe_kernel, alpha=alpha),
        grid=(m // tm, N_EXPERTS),
        in_specs=[pl.BlockSpec((tm, D_MODEL), lambda i, e: (i, 0)),
                  pl.BlockSpec((D_MODEL, N_EXPERTS), lambda i, e: (0, 0)),
                  pl.BlockSpec((1, N_EXPERTS), lambda i, e: (0, 0)),
                  pl.BlockSpec((1, D_MODEL, 2 * D_FF_EXPERT), lambda i, e: (e, 0, 0)),
                  pl.BlockSpec((1, D_FF_EXPERT, D_MODEL), lambda i, e: (e, 0, 0)),
                  pl.BlockSpec((1, D_MODEL), lambda i, e: (0, 0)),
                  pl.BlockSpec((1, D_MODEL), lambda i, e: (0, 0))],
        out_specs=pl.BlockSpec((tm, D_MODEL), lambda i, e: (i, 0)),
        out_shape=jax.ShapeDtypeStruct((m, D_MODEL), F32),
        scratch_shapes=[pltpu.VMEM((tm, D_MODEL), F32), pltpu.VMEM((tm, N_EXPERTS), F32)],
        compiler_params=_cparams(("arbitrary", "arbitrary")),
        name="moe",
    )(x2d, w_router, b_router.reshape(1, -1), w_exp_gu, w_exp_down, ln_g.reshape(1, -1), ln_b.reshape(1, -1))


def _pad_rows(a, rows):
    if a.shape[1] == rows:
        return a
    pad = jnp.zeros((a.shape[0], rows - a.shape[1]) + a.shape[2:], a.dtype)
    return jnp.concatenate([a, pad], axis=1)


def _round_up(n, mult):
    return -(-n // mult) * mult


def _token_mixer(x, past, lw):
    bn, n_time, _ = x.shape
    m = bn * n_time
    x2d = x.reshape(m, D_MODEL)
    prompt = past is None
    proj = _project(x2d, lw["w_in"], bn, n_time, time_major_u=prompt)

    kv = proj["kv"].reshape(bn, n_time, 2 * HEAD_DIM_A)
    a_k, a_v = kv[..., :HEAD_DIM_A], kv[..., HEAD_DIM_A:]
    misc = proj["misc"].reshape(bn, n_time, LANES)
    a_ki = misc[..., :IDX_DIM]
    a_w = misc[..., IDX_DIM:IDX_DIM + N_IDX_HEADS]
    c_f = misc[..., IDX_DIM + N_IDX_HEADS:IDX_DIM + N_IDX_HEADS + N_HEADS_C]
    c_k = proj["ck"].reshape(bn, n_time, D_MODEL // 2)
    c_v = proj["cv"].reshape(bn, n_time, D_MODEL // 2)
    logf = _log_forget(c_f.reshape(m, N_HEADS_C), lw["b_forget"]).reshape(bn, n_time, N_HEADS_C)

    if prompt:
        past_len = 0
        ak_all, av_all, aki_all, ck_all, cv_all, logf_all = a_k, a_v, a_ki, c_k, c_v, logf
        h0_re = jnp.zeros((bn, SSM_STATES), F32)
        h0_im = jnp.zeros((bn, SSM_STATES), F32)
        u_tm = proj["u"].reshape(n_time, bn, SSM_WIDTH)
    else:
        p_ak, p_av, p_aki, p_hr, p_hi, p_ck, p_cv, p_logf = past
        past_len = p_ak.shape[1]
        cat = lambda p, n: jnp.concatenate([p.reshape(bn, past_len, -1), n], axis=1)
        ak_all, av_all, aki_all = cat(p_ak, a_k), cat(p_av, a_v), cat(p_aki, a_ki)
        ck_all, cv_all, logf_all = cat(p_ck, c_k), cat(p_cv, c_v), cat(p_logf, logf)
        h0_re = p_hr.reshape(bn, SSM_STATES)
        h0_im = p_hi.reshape(bn, SSM_STATES)
        u_tm = jnp.swapaxes(proj["u"].reshape(bn, n_time, SSM_WIDTH), 0, 1)
    kv_len = past_len + n_time
    n_top = min(TOPK_MAX, kv_len // 4)

    y_b, h_re, h_im = _s5(u_tm, h0_re, h0_im, lw["ab_re"], lw["ab_im"], lw["bd"], lw["cr"], lw["ci"], lw["ssm_d"])
    if prompt:
        y_b = y_b.reshape(n_time, bn * SSM_WIDTH)
    else:
        y_b = jnp.swapaxes(y_b, 0, 1).reshape(m, SSM_WIDTH)

    tq_pad = _round_up(n_time, DSA_TQ)
    lp = _round_up(kv_len, DSA_KC)
    dup = lambda a: _pad_rows(jnp.concatenate([a, a], axis=-1).astype(BF16), lp)
    qa = _pad_rows(proj["qa"].reshape(bn, n_time, -1), tq_pad)
    qit = jnp.swapaxes(_pad_rows(proj["qi"].reshape(bn, n_time, -1), tq_pad), 1, 2)
    wt = jnp.swapaxes(_pad_rows(a_w, tq_pad), 1, 2)
    y_a = _dsa(qa, qit, wt, dup(ak_all), dup(av_all), _pad_rows(aki_all.astype(BF16), lp), lw["bias_tiles"],
               q_off=past_len, kv_len=kv_len, n_top=n_top)[:, :n_time].reshape(m, -1)

    lpc = _round_up(kv_len, FOX_KC)
    dcum, kext = _forget_cumsum(_pad_rows(logf_all, lpc), FOX_KC)
    kt = jnp.swapaxes(_pad_rows(ck_all.astype(BF16), lpc).reshape(bn, lpc // FOX_KC, FOX_KC, -1), 2, 3)
    y_c = _fox(proj["cq"].reshape(bn, n_time, -1), dcum[:, past_len:kv_len], kt, kext,
               _pad_rows(cv_all.astype(BF16), lpc), tq=min(FOX_TQ_PROMPT, n_time), q_off=past_len).reshape(m, -1)

    x1 = _merge(x2d, y_a, y_b, prompt, y_c, proj["gates"], lw["w_a_out"], lw["w_b_glu"], lw["w_c_out"], lw["w_o"],
                lw["ln1_g"], lw["ln1_b"], lw["alpha"], n_time)
    new_state = (a_k, a_v, a_ki, h_re.reshape(bn, N_GROUPS, STATE_DIM), h_im.reshape(bn, N_GROUPS, STATE_DIM),
                 c_k.reshape(bn, n_time, N_HEADS_C, HEAD_DIM_C), c_v.reshape(bn, n_time, N_HEADS_C, HEAD_DIM_C), logf)
    return x1, new_state


def kernel(x_prompt, x_sample, cache_a_k, cache_a_v, cache_a_kidx, state_ssm_re, state_ssm_im, cache_c_k, cache_c_v, cache_c_logf, w_in, b_forget, ssm_lam_re, ssm_lam_im, ssm_log_step, ssm_b_re, ssm_b_im, ssm_c_re, ssm_c_im, ssm_d, w_a_out, w_b_glu, w_c_out, w_o, ln1_g, ln1_b, ln2_g, ln2_b, t5_table, w_ffn_gu, w_ffn_down, w_router, b_router, w_exp_gu, w_exp_down):
    depth = w_in.shape[0]
    alpha = float((2 * depth) ** 0.25)
    bias_tiles = _dsa_bias_tiles(t5_table)
    xp, xs = x_prompt, x_sample
    rows_p = [[] for _ in range(8)]
    rows_s = [[] for _ in range(8)]
    for layer in range(depth):
        ab_re, ab_im, bb_re, bb_im = _s5_discretize(ssm_lam_re[layer], ssm_lam_im[layer], ssm_log_step[layer],
                                                    ssm_b_re[layer], ssm_b_im[layer])
        lw = dict(
            w_in=_pack_w_in(w_in[layer]), b_forget=b_forget[layer],
            ab_re=ab_re.reshape(1, SSM_STATES), ab_im=ab_im.reshape(1, SSM_STATES),
            bd=jnp.concatenate([_block_diag(bb_re), _block_diag(bb_im)], axis=1).astype(BF16),
            cr=_block_diag(jnp.swapaxes(ssm_c_re[layer], 1, 2)).astype(BF16),
            ci=_block_diag(jnp.swapaxes(ssm_c_im[layer], 1, 2)).astype(BF16),
            ssm_d=ssm_d[layer].reshape(1, SSM_WIDTH),
            w_a_out=w_a_out[layer].astype(BF16), w_b_glu=w_b_glu[layer].astype(BF16),
            w_c_out=w_c_out[layer].astype(BF16), w_o=w_o[layer].astype(BF16),
            ln1_g=ln1_g[layer], ln1_b=ln1_b[layer], bias_tiles=bias_tiles, alpha=alpha)
        past = (cache_a_k[layer], cache_a_v[layer], cache_a_kidx[layer], state_ssm_re[layer],
                state_ssm_im[layer], cache_c_k[layer], cache_c_v[layer], cache_c_logf[layer])
        xp1, st_p = _token_mixer(xp, None, lw)
        xs1, st_s = _token_mixer(xs, past, lw)
        i = layer // 2
        if layer % 2 == 0:
            wgu, wdn = w_ffn_gu[i].astype(BF16), w_ffn_down[i].astype(BF16)
            xp2 = _ffn(xp1, wgu, wdn, ln2_g[layer], ln2_b[layer], alpha)
            xs2 = _ffn(xs1, wgu, wdn, ln2_g[layer], ln2_b[layer], alpha)
        else:
            wr, wgu, wdn = w_router[i].astype(BF16), w_exp_gu[i].astype(BF16), w_exp_down[i].astype(BF16)
            xp2 = _moe(xp1, wr, b_router[i], wgu, wdn, ln2_g[layer], ln2_b[layer], alpha)
            xs2 = _moe(xs1, wr, b_router[i], wgu, wdn, ln2_g[layer], ln2_b[layer], alpha)
        xp = xp2.reshape(x_prompt.shape)
        xs = xs2.reshape(x_sample.shape)
        for j in range(8):
            rows_p[j].append(st_p[j])
            rows_s[j].append(st_s[j])
    (a_k_p, a_v_p, a_kidx_p, ssm_re_p, ssm_im_p, c_k_p, c_v_p, c_logf_p) = [jnp.stack(r) for r in rows_p]
    (a_k_s, a_v_s, a_kidx_s, ssm_re_s, ssm_im_s, c_k_s, c_v_s, c_logf_s) = [jnp.stack(r) for r in rows_s]
    return (xp, xs, a_k_p, a_k_s, a_v_p, a_v_s, a_kidx_p, a_kidx_s, ssm_re_p, ssm_re_s,
            ssm_im_p, ssm_im_s, c_k_p, c_k_s, c_v_p, c_v_s, c_logf_p, c_logf_s)
```

```python
import functools
import math

import jax
import jax.numpy as jnp
import numpy as np
from jax import lax
from jax.experimental import pallas as pl
from jax.experimental.pallas import tpu as pltpu

F32 = jnp.float32
BF16 = jnp.bfloat16
I32 = jnp.int32

D_MODEL = 1024
CHUNK = 64
N_HEADS_A = 8
HEAD_DIM_A = 64
N_IDX_HEADS = 8
IDX_DIM = 32
TOPK_MAX = 256
NUM_BUCKETS = 32
MAX_DISTANCE = 128
SSM_WIDTH = 512
GROUP_SIZE = 16
N_GROUPS = SSM_WIDTH // GROUP_SIZE
STATE_DIM = 64
SSM_STATES = N_GROUPS * STATE_DIM
N_HEADS_C = 8
HEAD_DIM_C = 64
N_BRANCH = 3
D_FF = 2816
N_EXPERTS = 8
D_FF_EXPERT = 1408
LN_EPS = 1e-5
PROJ_SIZES = (512, 64, 64, 256, 32, 8, 512, 512, 512, 512, 8, 3072)

LANES = 128
SUBLANES = 8
VMEM_LIMIT_BYTES = 56 * 1024 * 1024

PROJ_TM = 256
S5_ROWS = 512
S5_CW = 256
DSA_TQ = 128
DSA_KC = 256
FOX_TQ_PROMPT = 256
FOX_KC = 256
OUT_TM = 256
FFN_TM = 1024
FFN_FC = 256
MOE_TM = 512

INT_MIN = -(2 ** 31)
SKEY_NEG_INF = -2139095041


def _cparams(sem):
    return pltpu.CompilerParams(dimension_semantics=sem, vmem_limit_bytes=VMEM_LIMIT_BYTES)


def _mm(a, b):
    return jnp.dot(a, b, preferred_element_type=F32)


def _mm_nt(a, b):
    return lax.dot_general(a, b, (((1,), (1,)), ((), ())), preferred_element_type=F32)


def _layer_norm(z, g, b):
    mu = jnp.mean(z, axis=-1, keepdims=True)
    zc = z - mu
    var = jnp.mean(zc * zc, axis=-1, keepdims=True)
    return zc * lax.rsqrt(var + LN_EPS) * g + b


def _pack_w_in(w):
    offs = np.cumsum((0,) + PROJ_SIZES)
    a_q, a_k, a_v, a_qi, a_ki, a_w, b_u, c_q, c_k, c_v, c_f, gates = [
        w[:, offs[i]:offs[i + 1]] for i in range(len(PROJ_SIZES))]
    pad = jnp.zeros((w.shape[0], LANES - IDX_DIM - N_IDX_HEADS - N_HEADS_C), w.dtype)
    return jnp.concatenate([a_q, a_k, a_v, a_qi, a_ki, a_w, c_f, pad, b_u, c_q, c_k, c_v, gates],
                           axis=1).astype(BF16)


PROJ_OUT = (("qa", 0, 512), ("kv", 512, 128), ("qi", 640, 256), ("misc", 896, 128), ("u", 1024, 512),
            ("cq", 1536, 512), ("ck", 2048, 512), ("cv", 2560, 512), ("gates", 3072, 3072))
PROJ_COLS_PACKED = 6144


def _proj_kernel(x_ref, w_ref, *out_refs):
    xb = x_ref[...].astype(BF16)
    for (_, lo, width), o_ref in zip(PROJ_OUT, out_refs):
        for c in range(0, width, 512):
            cw = min(512, width - c)
            o_ref[:, c:c + cw] = _mm(xb, w_ref[:, lo + c:lo + c + cw])


def _project(x2d, w_packed, n_batch, n_time, time_major_u):
    m = x2d.shape[0]
    tm = min(PROJ_TM, m)
    n_t = n_time // tm if time_major_u else 1
    out_shape, out_specs = [], []
    for name, _, width in PROJ_OUT:
        if name == "u" and time_major_u:
            out_shape.append(jax.ShapeDtypeStruct((n_time, n_batch * width), F32))
            out_specs.append(pl.BlockSpec((tm, width), lambda i: (i % n_t, i // n_t)))
        else:
            out_shape.append(jax.ShapeDtypeStruct((m, width), F32))
            out_specs.append(pl.BlockSpec((tm, width), lambda i: (i, 0)))
    outs = pl.pallas_call(
        _proj_kernel,
        grid=(m // tm,),
        in_specs=[pl.BlockSpec((tm, D_MODEL), lambda i: (i, 0)),
                  pl.BlockSpec((D_MODEL, PROJ_COLS_PACKED), lambda i: (0, 0))],
        out_specs=out_specs,
        out_shape=out_shape,
        compiler_params=_cparams(("arbitrary",)),
        name="proj",
    )(x2d, w_packed)
    return dict(zip([p[0] for p in PROJ_OUT], outs))


def _s5_disc_kernel(lr_ref, li_ref, ls_ref, br_ref, bi_ref, ar_ref, ai_ref, bbr_ref, bbi_ref):
    lr, li = lr_ref[...], li_ref[...]
    dt = jnp.exp(ls_ref[...])
    mag = jnp.exp(lr * dt)
    ab_re = mag * jnp.cos(li * dt)
    ab_im = mag * jnp.sin(li * dt)
    den = lr * lr + li * li
    fr = ((ab_re - 1.0) * lr + ab_im * li) / den
    fi = (ab_im * lr - (ab_re - 1.0) * li) / den
    ar_ref[...] = ab_re
    ai_ref[...] = ab_im
    br, bi = br_ref[...], bi_ref[...]
    bbr_ref[...] = fr[:, None, :] * br - fi[:, None, :] * bi
    bbi_ref[...] = fr[:, None, :] * bi + fi[:, None, :] * br


def _s5_discretize(lam_re, lam_im, log_step, b_re, b_im):
    g, p, gs = b_re.shape
    shp = lambda *s: jax.ShapeDtypeStruct(s, F32)
    return pl.pallas_call(
        _s5_disc_kernel,
        out_shape=[shp(g, p), shp(g, p), shp(g, gs, p), shp(g, gs, p)],
        name="s5_disc",
    )(lam_re, lam_im, log_step.reshape(g, 1), jnp.swapaxes(b_re, 1, 2), jnp.swapaxes(b_im, 1, 2))


def _block_diag(blocks):
    g, r, c = blocks.shape
    eye = jnp.eye(g, dtype=bool)
    return jnp.where(eye[:, None, :, None], blocks[:, :, None, :], 0.0).reshape(g * r, g * c)


def _s5_kernel(u_ref, h0r_ref, h0i_ref, ar_ref, ai_ref, bd_ref, cr_ref, ci_ref, d_ref,
               y_ref, hr_ref, hi_ref, bur_ref, bui_ref, *, tc, bn):
    rows = tc * bn

    @pl.when(pl.program_id(0) == 0)
    def _():
        hr_ref[...] = h0r_ref[...]
        hi_ref[...] = h0i_ref[...]

    u = u_ref[...].reshape(rows, SSM_WIDTH)
    ub = u.astype(BF16)
    bur_ref[...] = _mm(ub, bd_ref[:, :SSM_STATES])
    bui_ref[...] = _mm(ub, bd_ref[:, SSM_STATES:])

    for cc in range(SSM_STATES // S5_CW):
        sl = slice(cc * S5_CW, (cc + 1) * S5_CW)
        ar = jnp.broadcast_to(ar_ref[:, sl], (bn, S5_CW))
        ai = jnp.broadcast_to(ai_ref[:, sl], (bn, S5_CW))

        def step(t, carry, sl=sl, ar=ar, ai=ai):
            hr, hi = carry
            r0 = pl.multiple_of(t * bn, bn)
            nr = ar * hr - ai * hi + bur_ref[pl.ds(r0, bn), sl]
            ni = ar * hi + ai * hr + bui_ref[pl.ds(r0, bn), sl]
            bur_ref[pl.ds(r0, bn), sl] = nr
            bui_ref[pl.ds(r0, bn), sl] = ni
            return nr, ni

        hr, hi = lax.fori_loop(0, tc, step, (hr_ref[:, sl], hi_ref[:, sl]))
        hr_ref[:, sl] = hr
        hi_ref[:, sl] = hi

    y = (_mm(bur_ref[...].astype(BF16), cr_ref[...]) - _mm(bui_ref[...].astype(BF16), ci_ref[...])
         + d_ref[...] * u)
    y_ref[...] = y.reshape(tc, bn, SSM_WIDTH)


def _s5(u_tm, h0_re, h0_im, ab_re, ab_im, bd, cr, ci, d_skip):
    n_time, bn, _ = u_tm.shape
    tc = min(S5_ROWS // bn, n_time)
    const = lambda *s: pl.BlockSpec(s, lambda i: (0,) * len(s))
    y, hr, hi = pl.pallas_call(
        functools.partial(_s5_kernel, tc=tc, bn=bn),
        grid=(n_time // tc,),
        in_specs=[pl.BlockSpec((tc, bn, SSM_WIDTH), lambda i: (i, 0, 0)),
                  const(bn, SSM_STATES), const(bn, SSM_STATES),
                  const(1, SSM_STATES), const(1, SSM_STATES),
                  const(SSM_WIDTH, 2 * SSM_STATES),
                  const(SSM_STATES, SSM_WIDTH), const(SSM_STATES, SSM_WIDTH),
                  const(1, SSM_WIDTH)],
        out_specs=[pl.BlockSpec((tc, bn, SSM_WIDTH), lambda i: (i, 0, 0)),
                   const(bn, SSM_STATES), const(bn, SSM_STATES)],
        out_shape=[jax.ShapeDtypeStruct((n_time, bn, SSM_WIDTH), F32),
                   jax.ShapeDtypeStruct((bn, SSM_STATES), F32),
                   jax.ShapeDtypeStruct((bn, SSM_STATES), F32)],
        scratch_shapes=[pltpu.VMEM((tc * bn, SSM_STATES), F32), pltpu.VMEM((tc * bn, SSM_STATES), F32)],
        compiler_params=_cparams(("arbitrary",)),
        name="s5",
    )(u_tm, h0_re, h0_im, ab_re, ab_im, bd, cr, ci, d_skip)
    return y, hr, hi


def _t5_bucket(rel):
    half = NUM_BUCKETS // 2
    max_exact = half // 2
    n = jnp.abs(rel)
    large = max_exact + (jnp.log(jnp.maximum(n, 1).astype(F32) / max_exact)
                         / math.log(MAX_DISTANCE / max_exact) * (half - max_exact)).astype(I32)
    large = jnp.minimum(large, half - 1)
    return jnp.where(rel > 0, half, 0) + jnp.where(n < max_exact, n, large)


DSA_NEAR_TILES = 3
DSA_SPLIT = 3
MASK_OFF = -(2.0 ** 100)


def _split3(x):
    parts, rem = [], x
    for _ in range(DSA_SPLIT):
        p = rem.astype(BF16).astype(F32)
        parts.append(p)
        rem = rem - p
    return parts


def _dsa_bias_consts(t5_table):
    q = jnp.arange(DSA_TQ, dtype=I32)[:, None]
    k = jnp.arange(DSA_KC, dtype=I32)[None, :]
    far = t5_table[_t5_bucket(jnp.full((), -(1 << 20), I32))].astype(F32)
    near = []
    for e in range(DSA_NEAR_TILES):
        tile = jnp.moveaxis(t5_table[_t5_bucket(k - q - LANES * e)], -1, 0)
        near.append(tile.astype(F32) - far[:, None, None])
    near.append(jnp.zeros_like(near[0]))
    near = jnp.stack(near).reshape(DSA_NEAR_TILES + 1, N_HEADS_A * DSA_TQ, DSA_KC)
    lane = jnp.arange(LANES)[None, None, :]
    left = jnp.zeros((N_HEADS_A, DSA_TQ, LANES), F32)
    for i, part in enumerate(_split3(far)):
        left = jnp.where(lane == HEAD_DIM_A + i, part[:, None, None], left)
    eye = jnp.broadcast_to(jnp.eye(DSA_TQ, dtype=F32)[None], (N_HEADS_A, DSA_TQ, DSA_TQ))
    lhs_static = jnp.concatenate([left, eye], axis=-1).reshape(N_HEADS_A * DSA_TQ, 2 * LANES).astype(BF16)
    return near, lhs_static


def _sort_key(s):
    bits = lax.bitcast_convert_type(s, I32)
    return bits ^ (lax.shift_right_arithmetic(bits, 31) & 0x7FFFFFFF)


def _dsa_kernel(qa_ref, qit_ref, wt_ref, kx_ref, vx_ref, ki_ref, near_ref, lhs_ref, o_ref,
                skey_ref, mask_ref, thr_ref, tie_ref, *, q_off, kv_len, n_top, idx_bits, chunk_counts):
    tq, kc = DSA_TQ, DSA_KC
    qb = pl.program_id(1)
    q_base = q_off + qb * tq
    adm_end = jnp.minimum(((q_base + tq - 1) // CHUNK + 1) * CHUNK, kv_len)
    n_c = (adm_end + kc - 1) // kc

    q_pos_t = q_base + lax.broadcasted_iota(I32, (kc, tq), 1)
    k_loc_t = lax.broadcasted_iota(I32, (kc, tq), 0)

    qit = qit_ref[0].astype(BF16)
    w = wt_ref[0]
    rhs_pairs = [jnp.concatenate([qit[(2 * j) * IDX_DIM:(2 * j + 1) * IDX_DIM, :],
                                  qit[(2 * j + 1) * IDX_DIM:(2 * j + 2) * IDX_DIM, :]], axis=1)
                 for j in range(N_IDX_HEADS // 2)]

    def score_chunk(c, _):
        r0 = pl.multiple_of(c * kc, kc)
        kic = ki_ref[0, pl.ds(r0, kc), :]
        acc = jnp.zeros((kc, tq), F32)
        for j in range(N_IDX_HEADS // 2):
            d = jnp.maximum(_mm(kic, rhs_pairs[j]), 0.0)
            acc = acc + w[2 * j:2 * j + 1, :] * d[:, :tq] + w[2 * j + 1:2 * j + 2, :] * d[:, tq:]
        acc = jnp.where(acc == 0.0, 0.0, acc)
        k_pos = r0 + k_loc_t
        adm = ((k_pos // CHUNK) <= (q_pos_t // CHUNK)) & (k_pos < kv_len)
        skey_ref[pl.ds(r0, kc), :] = _sort_key(jnp.where(adm, acc, -jnp.inf))
        return 0

    lax.fori_loop(0, n_c, score_chunk, 0)

    def fold(ind):
        return jnp.sum(ind.reshape(kc // SUBLANES, SUBLANES, tq), axis=0)

    for nc in chunk_counts:
        @pl.when(n_c == nc)
        def _(nc=nc):
            def value_bit(i, t):
                cand = t ^ lax.shift_left(jnp.int32(1), 31 - i)
                part = jnp.zeros((SUBLANES, tq), F32)
                for c in range(nc):
                    part = part + fold(jnp.where(skey_ref[c * kc:(c + 1) * kc, :] >= cand, 1.0, 0.0))
                cnt = jnp.sum(part, axis=0, keepdims=True)
                return jnp.where(cnt >= n_top, cand, t)
            thr_ref[...] = lax.fori_loop(0, 32, value_bit, jnp.full((1, tq), INT_MIN, I32))

    thr = thr_ref[...]

    def count(pred_fn):
        def body(c, part):
            r0 = pl.multiple_of(c * kc, kc)
            return part + fold(jnp.where(pred_fn(skey_ref[pl.ds(r0, kc), :], r0 + k_loc_t), 1.0, 0.0))
        return jnp.sum(lax.fori_loop(0, n_c, body, jnp.zeros((SUBLANES, tq), F32)), axis=0, keepdims=True)

    cnt_gt = count(lambda sk, _: sk > thr)
    cnt_ge = count(lambda sk, _: sk >= thr)
    need = n_top - cnt_gt

    tie_ref[...] = jnp.full((1, tq), 1 << 30, I32)
    has_tie = (cnt_ge > n_top) & (thr > SKEY_NEG_INF)

    @pl.when(jnp.max(jnp.where(has_tie, 1.0, 0.0)) > 0.5)
    def _():
        def index_bit(i, j):
            cand = j | lax.shift_left(jnp.int32(1), idx_bits - 1 - i)
            cnt = count(lambda sk, kp: (sk == thr) & (kp < cand))
            return jnp.where(cnt <= need - 1.0, cand, j)
        tie_ref[...] = lax.fori_loop(0, idx_bits, index_bit, jnp.zeros((1, tq), I32))

    tie_idx = tie_ref[...]

    def mask_chunk(c, _):
        r0 = pl.multiple_of(c * kc, kc)
        sk = skey_ref[pl.ds(r0, kc), :]
        sel = ((sk > thr) | ((sk == thr) & (r0 + k_loc_t <= tie_idx))) & (sk > SKEY_NEG_INF)
        mask_ref[pl.ds(r0, kc), :] = jnp.where(sel, 0.0, MASK_OFF).astype(BF16)
        return 0

    lax.fori_loop(0, n_c, mask_chunk, 0)

    lane = lax.broadcasted_iota(I32, (tq, LANES), 1)
    q_rows = []
    for h in range(N_HEADS_A):
        pair = qa_ref[:, (h // 2) * LANES:(h // 2 + 1) * LANES] * (HEAD_DIM_A ** -0.5)
        if h % 2:
            pair = pltpu.roll(pair, HEAD_DIM_A, axis=1)
        q_rows.append(pair)
    q_left = jnp.concatenate(q_rows, axis=0).astype(BF16)
    lane_all = lax.broadcasted_iota(I32, (N_HEADS_A * tq, LANES), 1)
    lhs = jnp.concatenate([jnp.where(lane_all < HEAD_DIM_A, q_left, lhs_ref[:, :LANES]), lhs_ref[:, LANES:]],
                          axis=1)

    def logits(c):
        r0 = pl.multiple_of(c * kc, kc)
        rhs = jnp.concatenate([kx_ref[0, pl.ds(r0, kc), :], mask_ref[pl.ds(r0, kc), :]], axis=1)
        return _mm_nt(lhs, rhs) + near_ref[jnp.clip(q_base // LANES - c * (kc // LANES), 0, DSA_NEAR_TILES)]

    def attend(c, carry):
        s, m, acc = carry
        s_next = logits(jnp.minimum(c + 1, n_c - 1))
        r0 = pl.multiple_of(c * kc, kc)
        m_new = jnp.maximum(m, jnp.max(s, axis=1, keepdims=True))
        p = jnp.exp(s - m_new)
        acc = jnp.exp(m - m_new) * acc + _mm(p.astype(BF16), vx_ref[0, pl.ds(r0, kc), :])
        return s_next, m_new, acc

    rows = N_HEADS_A * tq
    _, m, acc = lax.fori_loop(
        0, n_c, attend,
        (logits(0), jnp.full((rows, 1), -jnp.inf, F32), jnp.zeros((rows, LANES), F32)))
    out = acc / acc[:, HEAD_DIM_A:HEAD_DIM_A + 1]
    for j in range(N_HEADS_A // 2):
        even = out[(2 * j) * tq:(2 * j + 1) * tq]
        odd = pltpu.roll(out[(2 * j + 1) * tq:(2 * j + 2) * tq], HEAD_DIM_A, axis=1)
        o_ref[:, j * LANES:(j + 1) * LANES] = jnp.where(lane < HEAD_DIM_A, even, odd)


def _dsa(qa, qit, wt, kx, vx, ki, near, lhs_static, *, q_off, kv_len, n_top):
    bn, tq_total, _ = qa.shape
    lp = kx.shape[1]
    nq = tq_total // DSA_TQ
    idx_bits = max(1, int(lp - 1).bit_length())
    counts = sorted({-(-min(((q_off + (qb + 1) * DSA_TQ - 1) // CHUNK + 1) * CHUNK, kv_len) // DSA_KC)
                     for qb in range(nq)})
    return pl.pallas_call(
        functools.partial(_dsa_kernel, q_off=q_off, kv_len=kv_len, n_top=float(n_top), idx_bits=idx_bits,
                          chunk_counts=tuple(counts)),
        grid=(bn, nq),
        in_specs=[pl.BlockSpec((None, DSA_TQ, N_HEADS_A * HEAD_DIM_A), lambda b, q: (b, q, 0)),
                  pl.BlockSpec((1, N_IDX_HEADS * IDX_DIM, DSA_TQ), lambda b, q: (b, 0, q)),
                  pl.BlockSpec((1, N_IDX_HEADS, DSA_TQ), lambda b, q: (b, 0, q)),
                  pl.BlockSpec((1, lp, LANES), lambda b, q: (b, 0, 0)),
                  pl.BlockSpec((1, lp, LANES), lambda b, q: (b, 0, 0)),
                  pl.BlockSpec((1, lp, IDX_DIM), lambda b, q: (b, 0, 0)),
                  pl.BlockSpec((DSA_NEAR_TILES + 1, N_HEADS_A * DSA_TQ, DSA_KC), lambda b, q: (0, 0, 0)),
                  pl.BlockSpec((N_HEADS_A * DSA_TQ, 2 * LANES), lambda b, q: (0, 0))],
        out_specs=pl.BlockSpec((None, DSA_TQ, N_HEADS_A * HEAD_DIM_A), lambda b, q: (b, q, 0)),
        out_shape=jax.ShapeDtypeStruct((bn, tq_total, N_HEADS_A * HEAD_DIM_A), F32),
        scratch_shapes=[pltpu.VMEM((lp, DSA_TQ), I32), pltpu.VMEM((lp, DSA_TQ), BF16),
                        pltpu.VMEM((1, DSA_TQ), I32), pltpu.VMEM((1, DSA_TQ), I32)],
        compiler_params=_cparams(("arbitrary", "arbitrary")),
        name="dsa",
    )(qa, qit, wt, kx, vx, ki, near, lhs_static)


def _logf_kernel(cf_ref, b_ref, o_ref):
    o_ref[...] = jax.nn.log_sigmoid(cf_ref[...] + b_ref[...])


def _log_forget(cf2d, b_forget):
    m = cf2d.shape[0]
    tm = min(2048, m)
    return pl.pallas_call(
        _logf_kernel,
        grid=(m // tm,),
        in_specs=[pl.BlockSpec((tm, N_HEADS_C), lambda i: (i, 0)), pl.BlockSpec((1, N_HEADS_C), lambda i: (0, 0))],
        out_specs=pl.BlockSpec((tm, N_HEADS_C), lambda i: (i, 0)),
        out_shape=jax.ShapeDtypeStruct((m, N_HEADS_C), F32),
        compiler_params=_cparams(("arbitrary",)),
        name="logf",
    )(cf2d, b_forget.reshape(1, N_HEADS_C))


FOX_EXT_ROWS = 16
FOX_SPLIT = 3


def _split_bf16(x):
    parts = []
    rem = x
    for _ in range(FOX_SPLIT):
        p = rem.astype(BF16).astype(F32)
        parts.append(p)
        rem = rem - p
    return parts


def _cumsum_kernel(lf_ref, dc_ref, kext_ref, carry_ref):
    tm = lf_ref.shape[1]

    @pl.when(pl.program_id(1) == 0)
    def _():
        carry_ref[...] = jnp.zeros_like(carry_ref)

    lf = lf_ref[0]
    tri = (lax.broadcasted_iota(I32, (tm, tm), 1) <= lax.broadcasted_iota(I32, (tm, tm), 0)).astype(F32)
    dc = jnp.dot(tri, lf, preferred_element_type=F32, precision=lax.Precision.HIGHEST) + carry_ref[...]
    dc_ref[0] = dc
    carry_ref[...] = dc[tm - 1:tm, :]
    eye = (lax.broadcasted_iota(I32, (N_HEADS_C, N_HEADS_C), 0)
           == lax.broadcasted_iota(I32, (N_HEADS_C, N_HEADS_C), 1)).astype(F32)
    dct = lax.dot_general(eye, dc, (((1,), (1,)), ((), ())), preferred_element_type=F32,
                          precision=lax.Precision.HIGHEST)
    neg = _split_bf16(-dct)
    row = lax.broadcasted_iota(I32, (FOX_EXT_ROWS, tm), 0)
    for h in range(N_HEADS_C):
        tile = jnp.where(row < FOX_SPLIT, 1.0, 0.0)
        for i in range(FOX_SPLIT):
            tile = jnp.where(row == FOX_SPLIT + i, neg[i][h:h + 1, :], tile)
        kext_ref[0, 0, h] = tile.astype(BF16)


def _forget_cumsum(logf_all, kc):
    bn, lp, _ = logf_all.shape
    return pl.pallas_call(
        _cumsum_kernel,
        grid=(bn, lp // kc),
        in_specs=[pl.BlockSpec((1, kc, N_HEADS_C), lambda b, t: (b, t, 0))],
        out_specs=[pl.BlockSpec((1, kc, N_HEADS_C), lambda b, t: (b, t, 0)),
                   pl.BlockSpec((1, 1, N_HEADS_C, FOX_EXT_ROWS, kc), lambda b, t: (b, t, 0, 0, 0))],
        out_shape=[jax.ShapeDtypeStruct((bn, lp, N_HEADS_C), F32),
                   jax.ShapeDtypeStruct((bn, lp // kc, N_HEADS_C, FOX_EXT_ROWS, kc), BF16)],
        scratch_shapes=[pltpu.VMEM((1, N_HEADS_C), F32)],
        compiler_params=_cparams(("arbitrary", "arbitrary")),
        name="forget_cumsum",
    )(logf_all)


def _fox_kernel(q_ref, dq_ref, kt_ref, kext_ref, vx_ref, o_ref, qx_ref, m_ref, acc_ref, *, tq, kc, q_off):
    qb = pl.program_id(1)
    q_base = q_off + qb * tq
    n_c = (q_base + tq + kc - 1) // kc
    n_full = (q_base + 1) // kc
    lane = lax.broadcasted_iota(I32, (tq, LANES), 1)

    dq = dq_ref[0]
    for h in range(N_HEADS_C):
        pair = q_ref[:, (h // 2) * LANES:(h // 2 + 1) * LANES] * (HEAD_DIM_C ** -0.5)
        if h % 2:
            pair = pltpu.roll(pair, HEAD_DIM_C, axis=1)
        ext = jnp.where((lane >= HEAD_DIM_C + FOX_SPLIT) & (lane < HEAD_DIM_C + 2 * FOX_SPLIT), 1.0, 0.0)
        for i, part in enumerate(_split_bf16(dq[:, h:h + 1])):
            ext = jnp.where(lane == HEAD_DIM_C + i, part, ext)
        qx_ref[h] = jnp.where(lane < HEAD_DIM_C, pair, ext).astype(BF16)
    m_ref[...] = jnp.full(m_ref.shape, -jnp.inf, F32)
    acc_ref[...] = jnp.zeros(acc_ref.shape, F32)

    q_pos = q_base + lax.broadcasted_iota(I32, (tq, kc), 0)
    k_loc = lax.broadcasted_iota(I32, (tq, kc), 1)
    zrows = jnp.zeros((LANES - HEAD_DIM_C - FOX_EXT_ROWS, kc), BF16)

    def chunk(c, masked):
        r0 = pl.multiple_of(c * kc, kc)
        for h in range(N_HEADS_C):
            kx = jnp.concatenate([kt_ref[0, c, h * HEAD_DIM_C:(h + 1) * HEAD_DIM_C, :], kext_ref[0, c, h], zrows],
                                 axis=0)
            s = _mm(qx_ref[h], kx)
            if masked:
                s = jnp.where(r0 + k_loc <= q_pos, s, -jnp.inf)
            m_old = m_ref[h]
            m_new = jnp.maximum(m_old, jnp.broadcast_to(jnp.max(s, axis=1, keepdims=True), (tq, LANES)))
            p = jnp.exp(s - jnp.concatenate([m_new] * (kc // LANES), axis=1))
            pv = _mm(p.astype(BF16), vx_ref[0, pl.ds(r0, kc), h * LANES:(h + 1) * LANES])
            acc_ref[h] = jnp.exp(m_old - m_new) * acc_ref[h] + pv
            m_ref[h] = m_new

    def full_body(c, carry):
        chunk(c, False)
        return carry

    def diag_body(c, carry):
        chunk(c, True)
        return carry

    lax.fori_loop(0, n_full, full_body, 0)
    lax.fori_loop(n_full, n_c, diag_body, 0)

    for j in range(N_HEADS_C // 2):
        even = acc_ref[2 * j] / acc_ref[2 * j][:, HEAD_DIM_C:HEAD_DIM_C + 1]
        odd = acc_ref[2 * j + 1] / acc_ref[2 * j + 1][:, HEAD_DIM_C:HEAD_DIM_C + 1]
        o_ref[:, j * LANES:(j + 1) * LANES] = jnp.where(lane < HEAD_DIM_C, even, pltpu.roll(odd, HEAD_DIM_C, axis=1))


def _fox(cq, dq, kt, kext, vx, *, tq, q_off):
    bn, tq_total, width = cq.shape
    _, n_chunks, _, kc = kt.shape
    lp = n_chunks * kc
    return pl.pallas_call(
        functools.partial(_fox_kernel, tq=tq, kc=kc, q_off=q_off),
        grid=(bn, tq_total // tq),
        in_specs=[pl.BlockSpec((None, tq, width), lambda b, q: (b, q, 0)),
                  pl.BlockSpec((1, tq, N_HEADS_C), lambda b, q: (b, q, 0)),
                  pl.BlockSpec((1, n_chunks, width, kc), lambda b, q: (b, 0, 0, 0)),
                  pl.BlockSpec((1, n_chunks, N_HEADS_C, FOX_EXT_ROWS, kc), lambda b, q: (b, 0, 0, 0, 0)),
                  pl.BlockSpec((1, lp, N_HEADS_C * LANES), lambda b, q: (b, 0, 0))],
        out_specs=pl.BlockSpec((None, tq, width), lambda b, q: (b, q, 0)),
        out_shape=jax.ShapeDtypeStruct((bn, tq_total, width), F32),
        scratch_shapes=[pltpu.VMEM((N_HEADS_C, tq, LANES), BF16),
                        pltpu.VMEM((N_HEADS_C, tq, LANES), F32),
                        pltpu.VMEM((N_HEADS_C, tq, LANES), F32)],
        compiler_params=_cparams(("arbitrary", "arbitrary")),
        name="fox",
    )(cq, dq, kt, kext, vx)


def _merge_kernel(x_ref, ya_ref, yb_ref, yc_ref, g_ref, wa_ref, wb_ref, wc_ref, wo_ref, lng_ref, lnb_ref,
                  o_ref, *, alpha):
    ba = _mm(ya_ref[...].astype(BF16), wa_ref[...])
    bc = _mm(yc_ref[...].astype(BF16), wc_ref[...])
    glu = _mm(jax.nn.gelu(yb_ref[...]).astype(BF16), wb_ref[...])
    bb = glu[:, :D_MODEL] * jax.nn.sigmoid(glu[:, D_MODEL:])
    g = jax.nn.sigmoid(g_ref[...])
    merged = g[:, :D_MODEL] * ba + g[:, D_MODEL:2 * D_MODEL] * bb + g[:, 2 * D_MODEL:] * bc
    out = _mm(merged.astype(BF16), wo_ref[...])
    o_ref[...] = _layer_norm(alpha * x_ref[...] + out, lng_ref[...], lnb_ref[...])


def _merge(x2d, ya, yb, yb_time_major, yc, gates, wa, wb, wc, wo, ln_g, ln_b, alpha, n_time):
    m = x2d.shape[0]
    tm = min(OUT_TM, m)
    row = lambda w: pl.BlockSpec((tm, w), lambda i: (i, 0))
    const = lambda *s: pl.BlockSpec(s, lambda i: (0,) * len(s))
    if yb_time_major:
        n_t = n_time // tm
        yb_spec = pl.BlockSpec((tm, SSM_WIDTH), lambda i: (i % n_t, i // n_t))
    else:
        yb_spec = row(SSM_WIDTH)
    return pl.pallas_call(
        functools.partial(_merge_kernel, alpha=alpha),
        grid=(m // tm,),
        in_specs=[row(D_MODEL), row(512), yb_spec, row(512), row(N_BRANCH * D_MODEL),
                  const(512, D_MODEL), const(SSM_WIDTH, 2 * D_MODEL), const(512, D_MODEL),
                  const(D_MODEL, D_MODEL), const(1, D_MODEL), const(1, D_MODEL)],
        out_specs=row(D_MODEL),
        out_shape=jax.ShapeDtypeStruct((m, D_MODEL), F32),
        compiler_params=_cparams(("arbitrary",)),
        name="merge",
    )(x2d, ya, yb, yc, gates, wa, wb, wc, wo, ln_g.reshape(1, -1), ln_b.reshape(1, -1))


def _ffn_kernel(x_ref, wg_ref, wu_ref, wd_ref, lng_ref, lnb_ref, o_ref, acc_ref, *, alpha):
    c = pl.program_id(1)

    @pl.when(c == 0)
    def _():
        acc_ref[...] = jnp.zeros_like(acc_ref)

    xb = x_ref[...].astype(BF16)
    a = jax.nn.silu(_mm(xb, wg_ref[...])) * _mm(xb, wu_ref[...])
    acc_ref[...] += _mm(a.astype(BF16), wd_ref[...])

    @pl.when(c == pl.num_programs(1) - 1)
    def _():
        o_ref[...] = _layer_norm(alpha * x_ref[...] + acc_ref[...], lng_ref[...], lnb_ref[...])


def _ffn(x2d, w_gu, w_down, ln_g, ln_b, alpha):
    m = x2d.shape[0]
    tm = min(FFN_TM, m)
    d_ff = w_down.shape[0]
    n_c = d_ff // FFN_FC
    return pl.pallas_call(
        functools.partial(_ffn_kernel, alpha=alpha),
        grid=(m // tm, n_c),
        in_specs=[pl.BlockSpec((tm, D_MODEL), lambda i, c: (i, 0)),
                  pl.BlockSpec((D_MODEL, FFN_FC), lambda i, c: (0, c)),
                  pl.BlockSpec((D_MODEL, FFN_FC), lambda i, c: (0, n_c + c)),
                  pl.BlockSpec((FFN_FC, D_MODEL), lambda i, c: (c, 0)),
                  pl.BlockSpec((1, D_MODEL), lambda i, c: (0, 0)),
                  pl.BlockSpec((1, D_MODEL), lambda i, c: (0, 0))],
        out_specs=pl.BlockSpec((tm, D_MODEL), lambda i, c: (i, 0)),
        out_shape=jax.ShapeDtypeStruct((m, D_MODEL), F32),
        scratch_shapes=[pltpu.VMEM((tm, D_MODEL), F32)],
        compiler_params=_cparams(("arbitrary", "arbitrary")),
        name="ffn",
    )(x2d, w_gu, w_gu, w_down, ln_g.reshape(1, -1), ln_b.reshape(1, -1))


def _moe_kernel(x_ref, wr_ref, br_ref, wgu_ref, wd_ref, lng_ref, lnb_ref, o_ref, acc_ref, gate_ref, *, alpha):
    e = pl.program_id(1)
    tm = x_ref.shape[0]
    xb = x_ref[...].astype(BF16)
    lane = lax.broadcasted_iota(I32, (tm, N_EXPERTS), 1).astype(F32)

    @pl.when(e == 0)
    def _():
        acc_ref[...] = jnp.zeros_like(acc_ref)
        logits = _mm(xb, wr_ref[...]) + br_ref[...]
        m1 = jnp.max(logits, axis=1, keepdims=True)
        i1 = jnp.min(jnp.where(logits == m1, lane, float(N_EXPERTS)), axis=1, keepdims=True)
        rest = jnp.where(lane == i1, -jnp.inf, logits)
        m2 = jnp.max(rest, axis=1, keepdims=True)
        i2 = jnp.min(jnp.where(rest == m2, lane, float(N_EXPERTS)), axis=1, keepdims=True)
        ex = jnp.exp(m2 - m1)
        gate_ref[...] = jnp.where(lane == i1, 1.0 / (1.0 + ex), 0.0) + jnp.where(lane == i2, ex / (1.0 + ex), 0.0)

    ge = jnp.sum(jnp.where(lane == e.astype(F32), gate_ref[...], 0.0), axis=1, keepdims=True)
    h = _mm(xb, wgu_ref[0])
    a = jax.nn.silu(h[:, :D_FF_EXPERT]) * h[:, D_FF_EXPERT:]
    acc_ref[...] += ge * _mm(a.astype(BF16), wd_ref[0])

    @pl.when(e == pl.num_programs(1) - 1)
    def _():
        o_ref[...] = _layer_norm(alpha * x_ref[...] + acc_ref[...], lng_ref[...], lnb_ref[...])


def _moe(x2d, w_router, b_router, w_exp_gu, w_exp_down, ln_g, ln_b, alpha):
    m = x2d.shape[0]
    tm = min(MOE_TM, m)
    return pl.pallas_call(
        functools.partial(_moe_kernel, alpha=alpha),
        grid=(m // tm, N_EXPERTS),
        in_specs=[pl.BlockSpec((tm, D_MODEL), lambda i, e: (i, 0)),
                  pl.BlockSpec((D_MODEL, N_EXPERTS), lambda i, e: (0, 0)),
                  pl.BlockSpec((1, N_EXPERTS), lambda i, e: (0, 0)),
                  pl.BlockSpec((1, D_MODEL, 2 * D_FF_EXPERT), lambda i, e: (e, 0, 0)),
                  pl.BlockSpec((1, D_FF_EXPERT, D_MODEL), lambda i, e: (e, 0, 0)),
                  pl.BlockSpec((1, D_MODEL), lambda i, e: (0, 0)),
                  pl.BlockSpec((1, D_MODEL), lambda i, e: (0, 0))],
        out_specs=pl.BlockSpec((tm, D_MODEL), lambda i, e: (i, 0)),
        out_shape=jax.ShapeDtypeStruct((m, D_MODEL), F32),
        scratch_shapes=[pltpu.VMEM((tm, D_MODEL), F32), pltpu.VMEM((tm, N_EXPERTS), F32)],
        compiler_params=_cparams(("arbitrary", "arbitrary")),
        name="moe",
    )(x2d, w_router, b_router.reshape(1, -1), w_exp_gu, w_exp_down, ln_g.reshape(1, -1), ln_b.reshape(1, -1))


def _pad_rows(a, rows):
    if a.shape[1] == rows:
        return a
    pad = jnp.zeros((a.shape[0], rows - a.shape[1]) + a.shape[2:], a.dtype)
    return jnp.concatenate([a, pad], axis=1)


def _round_up(n, mult):
    return -(-n // mult) * mult


def _token_mixer(x, past, lw):
    bn, n_time, _ = x.shape
    m = bn * n_time
    x2d = x.reshape(m, D_MODEL)
    prompt = past is None
    proj = _project(x2d, lw["w_in"], bn, n_time, time_major_u=prompt)

    kv = proj["kv"].reshape(bn, n_time, 2 * HEAD_DIM_A)
    a_k, a_v = kv[..., :HEAD_DIM_A], kv[..., HEAD_DIM_A:]
    misc = proj["misc"].reshape(bn, n_time, LANES)
    a_ki = misc[..., :IDX_DIM]
    a_w = misc[..., IDX_DIM:IDX_DIM + N_IDX_HEADS]
    c_f = misc[..., IDX_DIM + N_IDX_HEADS:IDX_DIM + N_IDX_HEADS + N_HEADS_C]
    c_k = proj["ck"].reshape(bn, n_time, D_MODEL // 2)
    c_v = proj["cv"].reshape(bn, n_time, D_MODEL // 2)
    logf = _log_forget(c_f.reshape(m, N_HEADS_C), lw["b_forget"]).reshape(bn, n_time, N_HEADS_C)

    if prompt:
        past_len = 0
        ak_all, av_all, aki_all, ck_all, cv_all, logf_all = a_k, a_v, a_ki, c_k, c_v, logf
        h0_re = jnp.zeros((bn, SSM_STATES), F32)
        h0_im = jnp.zeros((bn, SSM_STATES), F32)
        u_tm = proj["u"].reshape(n_time, bn, SSM_WIDTH)
    else:
        p_ak, p_av, p_aki, p_hr, p_hi, p_ck, p_cv, p_logf = past
        past_len = p_ak.shape[1]
        cat = lambda p, n: jnp.concatenate([p.reshape(bn, past_len, -1), n], axis=1)
        ak_all, av_all, aki_all = cat(p_ak, a_k), cat(p_av, a_v), cat(p_aki, a_ki)
        ck_all, cv_all, logf_all = cat(p_ck, c_k), cat(p_cv, c_v), cat(p_logf, logf)
        h0_re = p_hr.reshape(bn, SSM_STATES)
        h0_im = p_hi.reshape(bn, SSM_STATES)
        u_tm = jnp.swapaxes(proj["u"].reshape(bn, n_time, SSM_WIDTH), 0, 1)
    kv_len = past_len + n_time
    n_top = min(TOPK_MAX, kv_len // 4)

    y_b, h_re, h_im = _s5(u_tm, h0_re, h0_im, lw["ab_re"], lw["ab_im"], lw["bd"], lw["cr"], lw["ci"], lw["ssm_d"])
    if prompt:
        y_b = y_b.reshape(n_time, bn * SSM_WIDTH)
    else:
        y_b = jnp.swapaxes(y_b, 0, 1).reshape(m, SSM_WIDTH)

    tq_pad = _round_up(n_time, DSA_TQ)
    lp = _round_up(kv_len, DSA_KC)
    ext = lambda a, n_ones: _pad_rows(jnp.concatenate(
        [a, jnp.ones(a.shape[:-1] + (n_ones,), a.dtype),
         jnp.zeros(a.shape[:-1] + (LANES - a.shape[-1] - n_ones,), a.dtype)], axis=-1).astype(BF16), lp)
    qa = _pad_rows(proj["qa"].reshape(bn, n_time, -1), tq_pad)
    qit = jnp.swapaxes(_pad_rows(proj["qi"].reshape(bn, n_time, -1), tq_pad), 1, 2)
    wt = jnp.swapaxes(_pad_rows(a_w, tq_pad), 1, 2)
    y_a = _dsa(qa, qit, wt, ext(ak_all, DSA_SPLIT), ext(av_all, 1), _pad_rows(aki_all.astype(BF16), lp),
               lw["bias_near"], lw["bias_lhs"], q_off=past_len, kv_len=kv_len, n_top=n_top)[:, :n_time].reshape(m, -1)

    lpc = _round_up(kv_len, FOX_KC)
    dcum, kext = _forget_cumsum(_pad_rows(logf_all, lpc), FOX_KC)
    kt = jnp.swapaxes(_pad_rows(ck_all.astype(BF16), lpc).reshape(bn, lpc // FOX_KC, FOX_KC, -1), 2, 3)
    cv_h = _pad_rows(cv_all.astype(BF16), lpc).reshape(bn, lpc, N_HEADS_C, HEAD_DIM_C)
    vx = jnp.concatenate([cv_h, jnp.ones((bn, lpc, N_HEADS_C, 1), BF16),
                          jnp.zeros((bn, lpc, N_HEADS_C, LANES - HEAD_DIM_C - 1), BF16)], axis=-1)
    y_c = _fox(proj["cq"].reshape(bn, n_time, -1), dcum[:, past_len:kv_len], kt, kext,
               vx.reshape(bn, lpc, N_HEADS_C * LANES), tq=min(FOX_TQ_PROMPT, n_time), q_off=past_len).reshape(m, -1)

    x1 = _merge(x2d, y_a, y_b, prompt, y_c, proj["gates"], lw["w_a_out"], lw["w_b_glu"], lw["w_c_out"], lw["w_o"],
                lw["ln1_g"], lw["ln1_b"], lw["alpha"], n_time)
    new_state = (a_k, a_v, a_ki, h_re.reshape(bn, N_GROUPS, STATE_DIM), h_im.reshape(bn, N_GROUPS, STATE_DIM),
                 c_k.reshape(bn, n_time, N_HEADS_C, HEAD_DIM_C), c_v.reshape(bn, n_time, N_HEADS_C, HEAD_DIM_C), logf)
    return x1, new_state


def kernel(x_prompt, x_sample, cache_a_k, cache_a_v, cache_a_kidx, state_ssm_re, state_ssm_im, cache_c_k, cache_c_v, cache_c_logf, w_in, b_forget, ssm_lam_re, ssm_lam_im, ssm_log_step, ssm_b_re, ssm_b_im, ssm_c_re, ssm_c_im, ssm_d, w_a_out, w_b_glu, w_c_out, w_o, ln1_g, ln1_b, ln2_g, ln2_b, t5_table, w_ffn_gu, w_ffn_down, w_router, b_router, w_exp_gu, w_exp_down):
    depth = w_in.shape[0]
    alpha = float((2 * depth) ** 0.25)
    bias_near, bias_lhs = _dsa_bias_consts(t5_table)
    xp, xs = x_prompt, x_sample
    rows_p = [[] for _ in range(8)]
    rows_s = [[] for _ in range(8)]
    for layer in range(depth):
        ab_re, ab_im, bb_re, bb_im = _s5_discretize(ssm_lam_re[layer], ssm_lam_im[layer], ssm_log_step[layer],
                                                    ssm_b_re[layer], ssm_b_im[layer])
        lw = dict(
            w_in=_pack_w_in(w_in[layer]), b_forget=b_forget[layer],
            ab_re=ab_re.reshape(1, SSM_STATES), ab_im=ab_im.reshape(1, SSM_STATES),
            bd=jnp.concatenate([_block_diag(bb_re), _block_diag(bb_im)], axis=1).astype(BF16),
            cr=_block_diag(jnp.swapaxes(ssm_c_re[layer], 1, 2)).astype(BF16),
            ci=_block_diag(jnp.swapaxes(ssm_c_im[layer], 1, 2)).astype(BF16),
            ssm_d=ssm_d[layer].reshape(1, SSM_WIDTH),
            w_a_out=w_a_out[layer].astype(BF16), w_b_glu=w_b_glu[layer].astype(BF16),
            w_c_out=w_c_out[layer].astype(BF16), w_o=w_o[layer].astype(BF16),
            ln1_g=ln1_g[layer], ln1_b=ln1_b[layer], bias_near=bias_near, bias_lhs=bias_lhs, alpha=alpha)
        past = (cache_a_k[layer], cache_a_v[layer], cache_a_kidx[layer], state_ssm_re[layer],
                state_ssm_im[layer], cache_c_k[layer], cache_c_v[layer], cache_c_logf[layer])
        xp1, st_p = _token_mixer(xp, None, lw)
        xs1, st_s = _token_mixer(xs, past, lw)
        i = layer // 2
        if layer % 2 == 0:
            wgu, wdn = w_ffn_gu[i].astype(BF16), w_ffn_down[i].astype(BF16)
            xp2 = _ffn(xp1, wgu, wdn, ln2_g[layer], ln2_b[layer], alpha)
            xs2 = _ffn(xs1, wgu, wdn, ln2_g[layer], ln2_b[layer], alpha)
        else:
            wr, wgu, wdn = w_router[i].astype(BF16), w_exp_gu[i].astype(BF16), w_exp_down[i].astype(BF16)
            xp2 = _moe(xp1, wr, b_router[i], wgu, wdn, ln2_g[layer], ln2_b[layer], alpha)
            xs2 = _moe(xs1, wr, b_router[i], wgu, wdn, ln2_g[layer], ln2_b[layer], alpha)
        xp = xp2.reshape(x_prompt.shape)
        xs = xs2.reshape(x_sample.shape)
        for j in range(8):
            rows_p[j].append(st_p[j])
            rows_s[j].append(st_s[j])
    (a_k_p, a_v_p, a_kidx_p, ssm_re_p, ssm_im_p, c_k_p, c_v_p, c_logf_p) = [jnp.stack(r) for r in rows_p]
    (a_k_s, a_v_s, a_kidx_s, ssm_re_s, ssm_im_s, c_k_s, c_v_s, c_logf_s) = [jnp.stack(r) for r in rows_s]
    return (xp, xs, a_k_p, a_k_s, a_v_p, a_v_s, a_kidx_p, a_kidx_s, ssm_re_p, ssm_re_s,
            ssm_im_p, ssm_im_s, c_k_p, c_k_s, c_v_p, c_v_s, c_logf_p, c_logf_s)
```

```python
import functools
import math

import jax
import jax.numpy as jnp
import numpy as np
from jax import lax
from jax.experimental import pallas as pl
from jax.experimental.pallas import tpu as pltpu

F32 = jnp.float32
BF16 = jnp.bfloat16
I32 = jnp.int32

D_MODEL = 1024
CHUNK = 64
N_HEADS_A = 8
HEAD_DIM_A = 64
N_IDX_HEADS = 8
IDX_DIM = 32
TOPK_MAX = 256
NUM_BUCKETS = 32
MAX_DISTANCE = 128
SSM_WIDTH = 512
GROUP_SIZE = 16
N_GROUPS = SSM_WIDTH // GROUP_SIZE
STATE_DIM = 64
SSM_STATES = N_GROUPS * STATE_DIM
N_HEADS_C = 8
HEAD_DIM_C = 64
N_BRANCH = 3
D_FF = 2816
N_EXPERTS = 8
D_FF_EXPERT = 1408
LN_EPS = 1e-5
PROJ_SIZES = (512, 64, 64, 256, 32, 8, 512, 512, 512, 512, 8, 3072)

LANES = 128
SUBLANES = 8
VMEM_LIMIT_BYTES = 56 * 1024 * 1024

PROJ_TM = 256
S5_ROWS = 512
S5_CW = 256
DSA_TQ = 128
DSA_KC = 256
FOX_TQ_PROMPT = 256
FOX_KC = 256
OUT_TM = 256
FFN_TM = 1024
FFN_FC = 256
MOE_TM = 512

INT_MIN = -(2 ** 31)
SKEY_NEG_INF = -2139095041


def _cparams(sem):
    return pltpu.CompilerParams(dimension_semantics=sem, vmem_limit_bytes=VMEM_LIMIT_BYTES)


def _mm(a, b):
    return jnp.dot(a, b, preferred_element_type=F32)


def _mm_nt(a, b):
    return lax.dot_general(a, b, (((1,), (1,)), ((), ())), preferred_element_type=F32)


def _layer_norm(z, g, b):
    mu = jnp.mean(z, axis=-1, keepdims=True)
    zc = z - mu
    var = jnp.mean(zc * zc, axis=-1, keepdims=True)
    return zc * lax.rsqrt(var + LN_EPS) * g + b


def _pack_w_in(w):
    offs = np.cumsum((0,) + PROJ_SIZES)
    a_q, a_k, a_v, a_qi, a_ki, a_w, b_u, c_q, c_k, c_v, c_f, gates = [
        w[:, offs[i]:offs[i + 1]] for i in range(len(PROJ_SIZES))]
    pad = jnp.zeros((w.shape[0], LANES - IDX_DIM - N_IDX_HEADS - N_HEADS_C), w.dtype)
    return jnp.concatenate([a_q, a_k, a_v, a_qi, a_ki, a_w, c_f, pad, b_u, c_q, c_k, c_v, gates],
                           axis=1).astype(BF16)


PROJ_OUT = (("qa", 0, 512), ("kv", 512, 128), ("qi", 640, 256), ("misc", 896, 128), ("u", 1024, 512),
            ("cq", 1536, 512), ("ck", 2048, 512), ("cv", 2560, 512), ("gates", 3072, 3072))
PROJ_COLS_PACKED = 6144


def _proj_kernel(x_ref, w_ref, *out_refs):
    xb = x_ref[...].astype(BF16)
    for (_, lo, width), o_ref in zip(PROJ_OUT, out_refs):
        for c in range(0, width, 512):
            cw = min(512, width - c)
            o_ref[:, c:c + cw] = _mm(xb, w_ref[:, lo + c:lo + c + cw])


def _project(x2d, w_packed):
    m = x2d.shape[0]
    tm = min(PROJ_TM, m)
    outs = pl.pallas_call(
        _proj_kernel,
        grid=(m // tm,),
        in_specs=[pl.BlockSpec((tm, D_MODEL), lambda i: (i, 0)),
                  pl.BlockSpec((D_MODEL, PROJ_COLS_PACKED), lambda i: (0, 0))],
        out_specs=[pl.BlockSpec((tm, width), lambda i: (i, 0)) for _, _, width in PROJ_OUT],
        out_shape=[jax.ShapeDtypeStruct((m, width), F32) for _, _, width in PROJ_OUT],
        compiler_params=_cparams(("arbitrary",)),
        name="proj",
    )(x2d, w_packed)
    return dict(zip([p[0] for p in PROJ_OUT], outs))


MISC_W_LANE = IDX_DIM
MISC_F_LANE = IDX_DIM + N_IDX_HEADS
DSA_K_ONES = 3


def _attn_layouts(kv, misc, ck, cv, kx_ref, vxa_ref, ki16_ref, kt_ref, vxc_ref):
    rows = kv.shape[0]
    lane = lax.broadcasted_iota(I32, (rows, LANES), 1)
    ones_k = jnp.where((lane >= HEAD_DIM_A) & (lane < HEAD_DIM_A + DSA_K_ONES), 1.0, 0.0)
    ones_v = jnp.where(lane == HEAD_DIM_A, 1.0, 0.0)
    kx_ref[...] = jnp.where(lane < HEAD_DIM_A, kv, ones_k).astype(BF16)
    vxa_ref[...] = jnp.where(lane < HEAD_DIM_A, pltpu.roll(kv, HEAD_DIM_A, axis=1), ones_v).astype(BF16)
    ki16_ref[...] = misc[:, :IDX_DIM].astype(BF16)
    kt_ref[...] = ck.T.astype(BF16)
    for h in range(N_HEADS_C):
        pair = cv[:, (h // 2) * LANES:(h // 2 + 1) * LANES]
        if h % 2:
            pair = pltpu.roll(pair, HEAD_DIM_C, axis=1)
        vxc_ref[:, h * LANES:(h + 1) * LANES] = jnp.where(lane < HEAD_DIM_C, pair, ones_v).astype(BF16)


def _proj_prompt_kernel(x_ref, w_ref, bf_ref, qa_ref, qit_ref, wt_ref, ak_ref, av_ref, aki_ref, logf_ref, u_ref,
                        cq_ref, ck_ref, cv_ref, g_ref, kx_ref, vxa_ref, ki16_ref, kt_ref, vxc_ref):
    xb = x_ref[...].astype(BF16)
    cols = {name: (lo, width) for name, lo, width in PROJ_OUT}

    def mm(name, c0=0, cw=None):
        lo, width = cols[name]
        cw = width if cw is None else cw
        return _mm(xb, w_ref[:, lo + c0:lo + c0 + cw])

    qa_ref[...] = mm("qa")
    u_ref[...] = mm("u")
    for c in range(0, cols["gates"][1], 512):
        g_ref[:, c:c + 512] = mm("gates", c, 512)
    small = mm("kv", 0, cols["kv"][1] + cols["qi"][1] + cols["misc"][1])
    kv = small[:, :LANES]
    qi = small[:, LANES:LANES + cols["qi"][1]]
    misc = small[:, LANES + cols["qi"][1]:]
    cqk = mm("cq", 0, cols["cq"][1] + cols["ck"][1])
    cq_ref[...] = cqk[:, :cols["cq"][1]]
    ck = cqk[:, cols["cq"][1]:]
    cv = mm("cv")
    ck_ref[...] = ck
    cv_ref[...] = cv
    qit_ref[...] = qi.T
    wt_ref[...] = misc.T[MISC_W_LANE:MISC_W_LANE + N_IDX_HEADS, :]
    ak_ref[...] = kv[:, :HEAD_DIM_A]
    av_ref[...] = pltpu.roll(kv, HEAD_DIM_A, axis=1)[:, :HEAD_DIM_A]
    aki_ref[...] = misc[:, :IDX_DIM]
    lf = jax.nn.log_sigmoid(misc + bf_ref[...])
    logf_ref[...] = pltpu.roll(lf, LANES - MISC_F_LANE, axis=1)[:, :N_HEADS_C]
    _attn_layouts(kv, misc, ck, cv, kx_ref, vxa_ref, ki16_ref, kt_ref.at[0], vxc_ref)


def _project_prompt(x2d, w_packed, b_forget, n_batch, n_time):
    m = x2d.shape[0]
    tm = FOX_KC
    n_t = n_time // tm
    bf = jnp.zeros((1, LANES), F32).at[0, MISC_F_LANE:MISC_F_LANE + N_HEADS_C].set(b_forget)
    row = lambda w, dt=F32: (jax.ShapeDtypeStruct((m, w), dt), pl.BlockSpec((tm, w), lambda i: (i, 0)))
    col = lambda r: (jax.ShapeDtypeStruct((r, m), F32), pl.BlockSpec((r, tm), lambda i: (0, i)))
    outs = dict(
        qa=row(512), qit=col(N_IDX_HEADS * IDX_DIM), wt=col(N_IDX_HEADS),
        a_k=row(HEAD_DIM_A), a_v=row(HEAD_DIM_A), a_ki=row(IDX_DIM), logf=row(N_HEADS_C),
        u=(jax.ShapeDtypeStruct((n_time, n_batch * SSM_WIDTH), F32),
           pl.BlockSpec((tm, SSM_WIDTH), lambda i: (i % n_t, i // n_t))),
        cq=row(512), ck=row(512), cv=row(512), gates=row(N_BRANCH * D_MODEL),
        kx=row(LANES, BF16), vxa=row(LANES, BF16), ki16=row(IDX_DIM, BF16),
        kt=(jax.ShapeDtypeStruct((m // tm, 512, tm), BF16), pl.BlockSpec((1, 512, tm), lambda i: (i, 0, 0))),
        vxc=row(N_HEADS_C * LANES, BF16))
    res = pl.pallas_call(
        _proj_prompt_kernel,
        grid=(m // tm,),
        in_specs=[pl.BlockSpec((tm, D_MODEL), lambda i: (i, 0)),
                  pl.BlockSpec((D_MODEL, PROJ_COLS_PACKED), lambda i: (0, 0)),
                  pl.BlockSpec((1, LANES), lambda i: (0, 0))],
        out_specs=[v[1] for v in outs.values()],
        out_shape=[v[0] for v in outs.values()],
        compiler_params=_cparams(("arbitrary",)),
        name="proj_prompt",
    )(x2d, w_packed, bf)
    return dict(zip(outs.keys(), res))


def _cache_layout_kernel(pk_ref, pv_ref, pki_ref, pck_ref, pcv_ref, nkv_ref, nmisc_ref, nck_ref, ncv_ref,
                         kx_ref, vxa_ref, ki16_ref, kt_ref, vxc_ref, *, n_past):
    is_new = pl.program_id(1) >= n_past
    rows = nkv_ref.shape[1]
    lane = lax.broadcasted_iota(I32, (rows, LANES), 1)
    zeros_k = jnp.zeros((rows, LANES - HEAD_DIM_A), F32)
    past_kv = jnp.where(lane < HEAD_DIM_A, jnp.concatenate([pk_ref[0], zeros_k], axis=1),
                        pltpu.roll(jnp.concatenate([pv_ref[0], zeros_k], axis=1), HEAD_DIM_A, axis=1))
    past_misc = jnp.concatenate([pki_ref[0], jnp.zeros((rows, LANES - IDX_DIM), F32)], axis=1)
    kv = jnp.where(is_new, nkv_ref[0], past_kv)
    misc = jnp.where(is_new, nmisc_ref[0], past_misc)
    ck = jnp.where(is_new, nck_ref[0], pck_ref[0])
    cv = jnp.where(is_new, ncv_ref[0], pcv_ref[0])
    _attn_layouts(kv, misc, ck, cv, kx_ref.at[0], vxa_ref.at[0], ki16_ref.at[0], kt_ref.at[0, 0], vxc_ref.at[0])


def _cache_layouts(p_k, p_v, p_ki, p_ck, p_cv, n_kv, n_misc, n_ck, n_cv):
    bn, n_rows, _ = p_k.shape
    kc = FOX_KC
    n_past = n_rows // kc
    n_chunks = n_past + 1
    lp = n_chunks * kc
    past = lambda w: pl.BlockSpec((1, kc, w), lambda b, c: (b, jnp.minimum(c, n_past - 1), 0))
    new = lambda w: pl.BlockSpec((1, kc, w), lambda b, c: (b, 0, 0))
    out = lambda w: pl.BlockSpec((1, kc, w), lambda b, c: (b, c, 0))
    return pl.pallas_call(
        functools.partial(_cache_layout_kernel, n_past=n_past),
        grid=(bn, n_chunks),
        in_specs=[past(HEAD_DIM_A), past(HEAD_DIM_A), past(IDX_DIM), past(512), past(512),
                  new(LANES), new(LANES), new(512), new(512)],
        out_specs=[out(LANES), out(LANES), out(IDX_DIM),
                   pl.BlockSpec((1, 1, 512, kc), lambda b, c: (b, c, 0, 0)), out(N_HEADS_C * LANES)],
        out_shape=[jax.ShapeDtypeStruct((bn, lp, LANES), BF16), jax.ShapeDtypeStruct((bn, lp, LANES), BF16),
                   jax.ShapeDtypeStruct((bn, lp, IDX_DIM), BF16),
                   jax.ShapeDtypeStruct((bn, n_chunks, 512, kc), BF16),
                   jax.ShapeDtypeStruct((bn, lp, N_HEADS_C * LANES), BF16)],
        compiler_params=_cparams(("arbitrary", "arbitrary")),
        name="cache_layouts",
    )(p_k, p_v, p_ki, p_ck, p_cv, n_kv, n_misc, n_ck, n_cv)


def _s5_disc_kernel(lr_ref, li_ref, ls_ref, br_ref, bi_ref, ar_ref, ai_ref, bbr_ref, bbi_ref):
    lr, li = lr_ref[...], li_ref[...]
    dt = jnp.exp(ls_ref[...])
    mag = jnp.exp(lr * dt)
    ab_re = mag * jnp.cos(li * dt)
    ab_im = mag * jnp.sin(li * dt)
    den = lr * lr + li * li
    fr = ((ab_re - 1.0) * lr + ab_im * li) / den
    fi = (ab_im * lr - (ab_re - 1.0) * li) / den
    ar_ref[...] = ab_re
    ai_ref[...] = ab_im
    br, bi = br_ref[...], bi_ref[...]
    bbr_ref[...] = fr[:, None, :] * br - fi[:, None, :] * bi
    bbi_ref[...] = fr[:, None, :] * bi + fi[:, None, :] * br


def _s5_discretize(lam_re, lam_im, log_step, b_re, b_im):
    g, p, gs = b_re.shape
    shp = lambda *s: jax.ShapeDtypeStruct(s, F32)
    return pl.pallas_call(
        _s5_disc_kernel,
        out_shape=[shp(g, p), shp(g, p), shp(g, gs, p), shp(g, gs, p)],
        name="s5_disc",
    )(lam_re, lam_im, log_step.reshape(g, 1), jnp.swapaxes(b_re, 1, 2), jnp.swapaxes(b_im, 1, 2))


def _block_diag(blocks):
    g, r, c = blocks.shape
    eye = jnp.eye(g, dtype=bool)
    return jnp.where(eye[:, None, :, None], blocks[:, :, None, :], 0.0).reshape(g * r, g * c)


def _s5_kernel(u_ref, h0r_ref, h0i_ref, ar_ref, ai_ref, bd_ref, cr_ref, ci_ref, d_ref,
               y_ref, hr_ref, hi_ref, bur_ref, bui_ref, *, tc, bn):
    rows = tc * bn

    @pl.when(pl.program_id(0) == 0)
    def _():
        hr_ref[...] = h0r_ref[...]
        hi_ref[...] = h0i_ref[...]

    u = u_ref[...].reshape(rows, SSM_WIDTH)
    ub = u.astype(BF16)
    bur_ref[...] = _mm(ub, bd_ref[:, :SSM_STATES])
    bui_ref[...] = _mm(ub, bd_ref[:, SSM_STATES:])

    for cc in range(SSM_STATES // S5_CW):
        sl = slice(cc * S5_CW, (cc + 1) * S5_CW)
        ar = jnp.broadcast_to(ar_ref[:, sl], (bn, S5_CW))
        ai = jnp.broadcast_to(ai_ref[:, sl], (bn, S5_CW))

        def step(t, carry, sl=sl, ar=ar, ai=ai):
            hr, hi = carry
            r0 = pl.multiple_of(t * bn, bn)
            nr = ar * hr - ai * hi + bur_ref[pl.ds(r0, bn), sl]
            ni = ar * hi + ai * hr + bui_ref[pl.ds(r0, bn), sl]
            bur_ref[pl.ds(r0, bn), sl] = nr
            bui_ref[pl.ds(r0, bn), sl] = ni
            return nr, ni

        hr, hi = lax.fori_loop(0, tc, step, (hr_ref[:, sl], hi_ref[:, sl]))
        hr_ref[:, sl] = hr
        hi_ref[:, sl] = hi

    y = (_mm(bur_ref[...].astype(BF16), cr_ref[...]) - _mm(bui_ref[...].astype(BF16), ci_ref[...])
         + d_ref[...] * u)
    y_ref[...] = y.reshape(tc, bn, SSM_WIDTH)


def _s5(u_tm, h0_re, h0_im, ab_re, ab_im, bd, cr, ci, d_skip):
    n_time, bn, _ = u_tm.shape
    tc = min(S5_ROWS // bn, n_time)
    const = lambda *s: pl.BlockSpec(s, lambda i: (0,) * len(s))
    y, hr, hi = pl.pallas_call(
        functools.partial(_s5_kernel, tc=tc, bn=bn),
        grid=(n_time // tc,),
        in_specs=[pl.BlockSpec((tc, bn, SSM_WIDTH), lambda i: (i, 0, 0)),
                  const(bn, SSM_STATES), const(bn, SSM_STATES),
                  const(1, SSM_STATES), const(1, SSM_STATES),
                  const(SSM_WIDTH, 2 * SSM_STATES),
                  const(SSM_STATES, SSM_WIDTH), const(SSM_STATES, SSM_WIDTH),
                  const(1, SSM_WIDTH)],
        out_specs=[pl.BlockSpec((tc, bn, SSM_WIDTH), lambda i: (i, 0, 0)),
                   const(bn, SSM_STATES), const(bn, SSM_STATES)],
        out_shape=[jax.ShapeDtypeStruct((n_time, bn, SSM_WIDTH), F32),
                   jax.ShapeDtypeStruct((bn, SSM_STATES), F32),
                   jax.ShapeDtypeStruct((bn, SSM_STATES), F32)],
        scratch_shapes=[pltpu.VMEM((tc * bn, SSM_STATES), F32), pltpu.VMEM((tc * bn, SSM_STATES), F32)],
        compiler_params=_cparams(("arbitrary",)),
        name="s5",
    )(u_tm, h0_re, h0_im, ab_re, ab_im, bd, cr, ci, d_skip)
    return y, hr, hi


def _t5_bucket(rel):
    half = NUM_BUCKETS // 2
    max_exact = half // 2
    n = jnp.abs(rel)
    large = max_exact + (jnp.log(jnp.maximum(n, 1).astype(F32) / max_exact)
                         / math.log(MAX_DISTANCE / max_exact) * (half - max_exact)).astype(I32)
    large = jnp.minimum(large, half - 1)
    return jnp.where(rel > 0, half, 0) + jnp.where(n < max_exact, n, large)


DSA_NEAR_TILES = 3
DSA_SPLIT = 3
MASK_OFF = -(2.0 ** 100)


def _split3(x):
    parts, rem = [], x
    for _ in range(DSA_SPLIT):
        p = rem.astype(BF16).astype(F32)
        parts.append(p)
        rem = rem - p
    return parts


def _dsa_bias_consts(t5_table):
    q = jnp.arange(DSA_TQ, dtype=I32)[:, None]
    k = jnp.arange(DSA_KC, dtype=I32)[None, :]
    far = t5_table[_t5_bucket(jnp.full((), -(1 << 20), I32))].astype(F32)
    near = []
    for e in range(DSA_NEAR_TILES):
        tile = jnp.moveaxis(t5_table[_t5_bucket(k - q - LANES * e)], -1, 0)
        near.append(tile.astype(F32) - far[:, None, None])
    near.append(jnp.zeros_like(near[0]))
    near = jnp.stack(near).reshape(DSA_NEAR_TILES + 1, N_HEADS_A * DSA_TQ, DSA_KC)
    lane = jnp.arange(LANES)[None, None, :]
    left = jnp.zeros((N_HEADS_A, DSA_TQ, LANES), F32)
    for i, part in enumerate(_split3(far)):
        left = jnp.where(lane == HEAD_DIM_A + i, part[:, None, None], left)
    eye = jnp.broadcast_to(jnp.eye(DSA_TQ, dtype=F32)[None], (N_HEADS_A, DSA_TQ, DSA_TQ))
    lhs_static = jnp.concatenate([left, eye], axis=-1).reshape(N_HEADS_A * DSA_TQ, 2 * LANES).astype(BF16)
    return near, lhs_static


def _sort_key(s):
    bits = lax.bitcast_convert_type(s, I32)
    return bits ^ (lax.shift_right_arithmetic(bits, 31) & 0x7FFFFFFF)


def _dsa_kernel(qa_ref, qit_ref, wt_ref, kx_ref, vx_ref, ki_ref, near_ref, lhs_ref, o_ref,
                skey_ref, mask_ref, thr_ref, tie_ref, *, q_off, kv_len, n_top, idx_bits, chunk_counts):
    tq, kc = DSA_TQ, DSA_KC
    qb = pl.program_id(1)
    q_base = q_off + qb * tq
    adm_end = jnp.minimum(((q_base + tq - 1) // CHUNK + 1) * CHUNK, kv_len)
    n_c = (adm_end + kc - 1) // kc

    q_pos_t = q_base + lax.broadcasted_iota(I32, (kc, tq), 1)
    k_loc_t = lax.broadcasted_iota(I32, (kc, tq), 0)

    qit = qit_ref[...].astype(BF16)
    w = wt_ref[...]
    rhs_pairs = [jnp.concatenate([qit[(2 * j) * IDX_DIM:(2 * j + 1) * IDX_DIM, :],
                                  qit[(2 * j + 1) * IDX_DIM:(2 * j + 2) * IDX_DIM, :]], axis=1)
                 for j in range(N_IDX_HEADS // 2)]

    def score_chunk(c, _):
        r0 = pl.multiple_of(c * kc, kc)
        kic = ki_ref[0, pl.ds(r0, kc), :]
        acc = jnp.zeros((kc, tq), F32)
        for j in range(N_IDX_HEADS // 2):
            d = jnp.maximum(_mm(kic, rhs_pairs[j]), 0.0)
            acc = acc + w[2 * j:2 * j + 1, :] * d[:, :tq] + w[2 * j + 1:2 * j + 2, :] * d[:, tq:]
        acc = jnp.where(acc == 0.0, 0.0, acc)
        k_pos = r0 + k_loc_t
        adm = ((k_pos // CHUNK) <= (q_pos_t // CHUNK)) & (k_pos < kv_len)
        skey_ref[pl.ds(r0, kc), :] = _sort_key(jnp.where(adm, acc, -jnp.inf))
        return 0

    lax.fori_loop(0, n_c, score_chunk, 0)

    def fold(ind):
        return jnp.sum(ind.reshape(kc // SUBLANES, SUBLANES, tq), axis=0)

    for nc in chunk_counts:
        @pl.when(n_c == nc)
        def _(nc=nc):
            def value_bit(i, t):
                cand = t ^ lax.shift_left(jnp.int32(1), 31 - i)
                part = jnp.zeros((SUBLANES, tq), F32)
                for c in range(nc):
                    part = part + fold(jnp.where(skey_ref[c * kc:(c + 1) * kc, :] >= cand, 1.0, 0.0))
                cnt = jnp.sum(part, axis=0, keepdims=True)
                return jnp.where(cnt >= n_top, cand, t)
            thr_ref[...] = lax.fori_loop(0, 32, value_bit, jnp.full((1, tq), INT_MIN, I32))

    thr = thr_ref[...]

    def count(pred_fn):
        def body(c, part):
            r0 = pl.multiple_of(c * kc, kc)
            return part + fold(jnp.where(pred_fn(skey_ref[pl.ds(r0, kc), :], r0 + k_loc_t), 1.0, 0.0))
        return jnp.sum(lax.fori_loop(0, n_c, body, jnp.zeros((SUBLANES, tq), F32)), axis=0, keepdims=True)

    cnt_gt = count(lambda sk, _: sk > thr)
    cnt_ge = count(lambda sk, _: sk >= thr)
    need = n_top - cnt_gt

    tie_ref[...] = jnp.full((1, tq), 1 << 30, I32)
    has_tie = (cnt_ge > n_top) & (thr > SKEY_NEG_INF)

    @pl.when(jnp.max(jnp.where(has_tie, 1.0, 0.0)) > 0.5)
    def _():
        def index_bit(i, j):
            cand = j | lax.shift_left(jnp.int32(1), idx_bits - 1 - i)
            cnt = count(lambda sk, kp: (sk == thr) & (kp < cand))
            return jnp.where(cnt <= need - 1.0, cand, j)
        tie_ref[...] = lax.fori_loop(0, idx_bits, index_bit, jnp.zeros((1, tq), I32))

    tie_idx = tie_ref[...]

    def mask_chunk(c, _):
        r0 = pl.multiple_of(c * kc, kc)
        sk = skey_ref[pl.ds(r0, kc), :]
        sel = ((sk > thr) | ((sk == thr) & (r0 + k_loc_t <= tie_idx))) & (sk > SKEY_NEG_INF)
        mask_ref[pl.ds(r0, kc), :] = jnp.where(sel, 0.0, MASK_OFF).astype(BF16)
        return 0

    lax.fori_loop(0, n_c, mask_chunk, 0)

    lane = lax.broadcasted_iota(I32, (tq, LANES), 1)
    q_rows = []
    for h in range(N_HEADS_A):
        pair = qa_ref[:, (h // 2) * LANES:(h // 2 + 1) * LANES] * (HEAD_DIM_A ** -0.5)
        if h % 2:
            pair = pltpu.roll(pair, HEAD_DIM_A, axis=1)
        q_rows.append(pair)
    q_left = jnp.concatenate(q_rows, axis=0).astype(BF16)
    lane_all = lax.broadcasted_iota(I32, (N_HEADS_A * tq, LANES), 1)
    lhs = jnp.concatenate([jnp.where(lane_all < HEAD_DIM_A, q_left, lhs_ref[:, :LANES]), lhs_ref[:, LANES:]],
                          axis=1)

    def logits(c):
        r0 = pl.multiple_of(c * kc, kc)
        rhs = jnp.concatenate([kx_ref[0, pl.ds(r0, kc), :], mask_ref[pl.ds(r0, kc), :]], axis=1)
        return _mm_nt(lhs, rhs) + near_ref[jnp.clip(q_base // LANES - c * (kc // LANES), 0, DSA_NEAR_TILES)]

    def attend(c, carry):
        s, m, acc = carry
        s_next = logits(jnp.minimum(c + 1, n_c - 1))
        r0 = pl.multiple_of(c * kc, kc)
        m_new = jnp.maximum(m, jnp.max(s, axis=1, keepdims=True))
        p = jnp.exp(s - m_new)
        acc = jnp.exp(m - m_new) * acc + _mm(p.astype(BF16), vx_ref[0, pl.ds(r0, kc), :])
        return s_next, m_new, acc

    rows = N_HEADS_A * tq
    _, m, acc = lax.fori_loop(
        0, n_c, attend,
        (logits(0), jnp.full((rows, 1), -jnp.inf, F32), jnp.zeros((rows, LANES), F32)))
    out = acc / acc[:, HEAD_DIM_A:HEAD_DIM_A + 1]
    for j in range(N_HEADS_A // 2):
        even = out[(2 * j) * tq:(2 * j + 1) * tq]
        odd = pltpu.roll(out[(2 * j + 1) * tq:(2 * j + 2) * tq], HEAD_DIM_A, axis=1)
        o_ref[:, j * LANES:(j + 1) * LANES] = jnp.where(lane < HEAD_DIM_A, even, odd)


def _dsa(qa, qit, wt, kx, vx, ki, near, lhs_static, *, q_off, kv_len, n_top):
    bn, tq_total, _ = qa.shape
    lp = kx.shape[1]
    nq = tq_total // DSA_TQ
    idx_bits = max(1, int(lp - 1).bit_length())
    counts = sorted({-(-min(((q_off + (qb + 1) * DSA_TQ - 1) // CHUNK + 1) * CHUNK, kv_len) // DSA_KC)
                     for qb in range(nq)})

    def qside(a):
        if a.ndim == 3:
            return pl.BlockSpec((None, a.shape[1], DSA_TQ), lambda b, q: (b, 0, q))
        return pl.BlockSpec((a.shape[0], DSA_TQ), lambda b, q: (0, b * nq + q))

    return pl.pallas_call(
        functools.partial(_dsa_kernel, q_off=q_off, kv_len=kv_len, n_top=float(n_top), idx_bits=idx_bits,
                          chunk_counts=tuple(counts)),
        grid=(bn, nq),
        in_specs=[pl.BlockSpec((None, DSA_TQ, N_HEADS_A * HEAD_DIM_A), lambda b, q: (b, q, 0)),
                  qside(qit), qside(wt),
                  pl.BlockSpec((1, lp, LANES), lambda b, q: (b, 0, 0)),
                  pl.BlockSpec((1, lp, LANES), lambda b, q: (b, 0, 0)),
                  pl.BlockSpec((1, lp, IDX_DIM), lambda b, q: (b, 0, 0)),
                  pl.BlockSpec((DSA_NEAR_TILES + 1, N_HEADS_A * DSA_TQ, DSA_KC), lambda b, q: (0, 0, 0)),
                  pl.BlockSpec((N_HEADS_A * DSA_TQ, 2 * LANES), lambda b, q: (0, 0))],
        out_specs=pl.BlockSpec((None, DSA_TQ, N_HEADS_A * HEAD_DIM_A), lambda b, q: (b, q, 0)),
        out_shape=jax.ShapeDtypeStruct((bn, tq_total, N_HEADS_A * HEAD_DIM_A), F32),
        scratch_shapes=[pltpu.VMEM((lp, DSA_TQ), I32), pltpu.VMEM((lp, DSA_TQ), BF16),
                        pltpu.VMEM((1, DSA_TQ), I32), pltpu.VMEM((1, DSA_TQ), I32)],
        compiler_params=_cparams(("arbitrary", "arbitrary")),
        name="dsa",
    )(qa, qit, wt, kx, vx, ki, near, lhs_static)


def _logf_kernel(cf_ref, b_ref, o_ref):
    o_ref[...] = jax.nn.log_sigmoid(cf_ref[...] + b_ref[...])


def _log_forget(cf2d, b_forget):
    m = cf2d.shape[0]
    tm = min(2048, m)
    return pl.pallas_call(
        _logf_kernel,
        grid=(m // tm,),
        in_specs=[pl.BlockSpec((tm, N_HEADS_C), lambda i: (i, 0)), pl.BlockSpec((1, N_HEADS_C), lambda i: (0, 0))],
        out_specs=pl.BlockSpec((tm, N_HEADS_C), lambda i: (i, 0)),
        out_shape=jax.ShapeDtypeStruct((m, N_HEADS_C), F32),
        compiler_params=_cparams(("arbitrary",)),
        name="logf",
    )(cf2d, b_forget.reshape(1, N_HEADS_C))


FOX_EXT_ROWS = 16
FOX_SPLIT = 3


def _split_bf16(x):
    parts = []
    rem = x
    for _ in range(FOX_SPLIT):
        p = rem.astype(BF16).astype(F32)
        parts.append(p)
        rem = rem - p
    return parts


def _cumsum_kernel(lf_ref, dc_ref, kext_ref, carry_ref):
    tm = lf_ref.shape[1]

    @pl.when(pl.program_id(1) == 0)
    def _():
        carry_ref[...] = jnp.zeros_like(carry_ref)

    lf = lf_ref[0]
    tri = (lax.broadcasted_iota(I32, (tm, tm), 1) <= lax.broadcasted_iota(I32, (tm, tm), 0)).astype(F32)
    dc = jnp.dot(tri, lf, preferred_element_type=F32, precision=lax.Precision.HIGHEST) + carry_ref[...]
    dc_ref[0] = dc
    carry_ref[...] = dc[tm - 1:tm, :]
    eye = (lax.broadcasted_iota(I32, (N_HEADS_C, N_HEADS_C), 0)
           == lax.broadcasted_iota(I32, (N_HEADS_C, N_HEADS_C), 1)).astype(F32)
    dct = lax.dot_general(eye, dc, (((1,), (1,)), ((), ())), preferred_element_type=F32,
                          precision=lax.Precision.HIGHEST)
    neg = _split_bf16(-dct)
    row = lax.broadcasted_iota(I32, (FOX_EXT_ROWS, tm), 0)
    for h in range(N_HEADS_C):
        tile = jnp.where(row < FOX_SPLIT, 1.0, 0.0)
        for i in range(FOX_SPLIT):
            tile = jnp.where(row == FOX_SPLIT + i, neg[i][h:h + 1, :], tile)
        kext_ref[0, 0, h] = tile.astype(BF16)


def _forget_cumsum(logf_all, kc):
    bn, lp, _ = logf_all.shape
    return pl.pallas_call(
        _cumsum_kernel,
        grid=(bn, lp // kc),
        in_specs=[pl.BlockSpec((1, kc, N_HEADS_C), lambda b, t: (b, t, 0))],
        out_specs=[pl.BlockSpec((1, kc, N_HEADS_C), lambda b, t: (b, t, 0)),
                   pl.BlockSpec((1, 1, N_HEADS_C, FOX_EXT_ROWS, kc), lambda b, t: (b, t, 0, 0, 0))],
        out_shape=[jax.ShapeDtypeStruct((bn, lp, N_HEADS_C), F32),
                   jax.ShapeDtypeStruct((bn, lp // kc, N_HEADS_C, FOX_EXT_ROWS, kc), BF16)],
        scratch_shapes=[pltpu.VMEM((1, N_HEADS_C), F32)],
        compiler_params=_cparams(("arbitrary", "arbitrary")),
        name="forget_cumsum",
    )(logf_all)


def _fox_kernel(q_ref, dq_ref, kt_ref, kext_ref, vx_ref, o_ref, qx_ref, m_ref, acc_ref, *, tq, kc, q_off):
    qb = pl.program_id(1)
    q_base = q_off + qb * tq
    n_c = (q_base + tq + kc - 1) // kc
    n_full = (q_base + 1) // kc
    lane = lax.broadcasted_iota(I32, (tq, LANES), 1)

    dq = dq_ref[0]
    for h in range(N_HEADS_C):
        pair = q_ref[:, (h // 2) * LANES:(h // 2 + 1) * LANES] * (HEAD_DIM_C ** -0.5)
        if h % 2:
            pair = pltpu.roll(pair, HEAD_DIM_C, axis=1)
        ext = jnp.where((lane >= HEAD_DIM_C + FOX_SPLIT) & (lane < HEAD_DIM_C + 2 * FOX_SPLIT), 1.0, 0.0)
        for i, part in enumerate(_split_bf16(dq[:, h:h + 1])):
            ext = jnp.where(lane == HEAD_DIM_C + i, part, ext)
        qx_ref[h] = jnp.where(lane < HEAD_DIM_C, pair, ext).astype(BF16)
    m_ref[...] = jnp.full(m_ref.shape, -jnp.inf, F32)
    acc_ref[...] = jnp.zeros(acc_ref.shape, F32)

    q_pos = q_base + lax.broadcasted_iota(I32, (tq, kc), 0)
    k_loc = lax.broadcasted_iota(I32, (tq, kc), 1)
    zrows = jnp.zeros((LANES - HEAD_DIM_C - FOX_EXT_ROWS, kc), BF16)

    def chunk(c, masked):
        r0 = pl.multiple_of(c * kc, kc)
        for h in range(N_HEADS_C):
            kx = jnp.concatenate([kt_ref[0, c, h * HEAD_DIM_C:(h + 1) * HEAD_DIM_C, :], kext_ref[0, c, h], zrows],
                                 axis=0)
            s = _mm(qx_ref[h], kx)
            if masked:
                s = jnp.where(r0 + k_loc <= q_pos, s, -jnp.inf)
            m_old = m_ref[h]
            m_new = jnp.maximum(m_old, jnp.broadcast_to(jnp.max(s, axis=1, keepdims=True), (tq, LANES)))
            p = jnp.exp(s - jnp.concatenate([m_new] * (kc // LANES), axis=1))
            pv = _mm(p.astype(BF16), vx_ref[0, pl.ds(r0, kc), h * LANES:(h + 1) * LANES])
            acc_ref[h] = jnp.exp(m_old - m_new) * acc_ref[h] + pv
            m_ref[h] = m_new

    def full_body(c, carry):
        chunk(c, False)
        return carry

    def diag_body(c, carry):
        chunk(c, True)
        return carry

    lax.fori_loop(0, n_full, full_body, 0)
    lax.fori_loop(n_full, n_c, diag_body, 0)

    for j in range(N_HEADS_C // 2):
        even = acc_ref[2 * j] / acc_ref[2 * j][:, HEAD_DIM_C:HEAD_DIM_C + 1]
        odd = acc_ref[2 * j + 1] / acc_ref[2 * j + 1][:, HEAD_DIM_C:HEAD_DIM_C + 1]
        o_ref[:, j * LANES:(j + 1) * LANES] = jnp.where(lane < HEAD_DIM_C, even, pltpu.roll(odd, HEAD_DIM_C, axis=1))


def _fox(cq, dq, kt, kext, vx, *, tq, q_off):
    bn, tq_total, width = cq.shape
    _, n_chunks, _, kc = kt.shape
    lp = n_chunks * kc
    return pl.pallas_call(
        functools.partial(_fox_kernel, tq=tq, kc=kc, q_off=q_off),
        grid=(bn, tq_total // tq),
        in_specs=[pl.BlockSpec((None, tq, width), lambda b, q: (b, q, 0)),
                  pl.BlockSpec((1, tq, N_HEADS_C), lambda b, q: (b, q, 0)),
                  pl.BlockSpec((1, n_chunks, width, kc), lambda b, q: (b, 0, 0, 0)),
                  pl.BlockSpec((1, n_chunks, N_HEADS_C, FOX_EXT_ROWS, kc), lambda b, q: (b, 0, 0, 0, 0)),
                  pl.BlockSpec((1, lp, N_HEADS_C * LANES), lambda b, q: (b, 0, 0))],
        out_specs=pl.BlockSpec((None, tq, width), lambda b, q: (b, q, 0)),
        out_shape=jax.ShapeDtypeStruct((bn, tq_total, width), F32),
        scratch_shapes=[pltpu.VMEM((N_HEADS_C, tq, LANES), BF16),
                        pltpu.VMEM((N_HEADS_C, tq, LANES), F32),
                        pltpu.VMEM((N_HEADS_C, tq, LANES), F32)],
        compiler_params=_cparams(("arbitrary", "arbitrary")),
        name="fox",
    )(cq, dq, kt, kext, vx)


def _merge_kernel(x_ref, ya_ref, yb_ref, yc_ref, g_ref, wa_ref, wb_ref, wc_ref, wo_ref, lng_ref, lnb_ref,
                  o_ref, *, alpha):
    ba = _mm(ya_ref[...].astype(BF16), wa_ref[...])
    bc = _mm(yc_ref[...].astype(BF16), wc_ref[...])
    glu = _mm(jax.nn.gelu(yb_ref[...]).astype(BF16), wb_ref[...])
    bb = glu[:, :D_MODEL] * jax.nn.sigmoid(glu[:, D_MODEL:])
    g = jax.nn.sigmoid(g_ref[...])
    merged = g[:, :D_MODEL] * ba + g[:, D_MODEL:2 * D_MODEL] * bb + g[:, 2 * D_MODEL:] * bc
    out = _mm(merged.astype(BF16), wo_ref[...])
    o_ref[...] = _layer_norm(alpha * x_ref[...] + out, lng_ref[...], lnb_ref[...])


def _merge(x2d, ya, yb, yb_time_major, yc, gates, wa, wb, wc, wo, ln_g, ln_b, alpha, n_time):
    m = x2d.shape[0]
    tm = min(OUT_TM, m)
    row = lambda w: pl.BlockSpec((tm, w), lambda i: (i, 0))
    const = lambda *s: pl.BlockSpec(s, lambda i: (0,) * len(s))
    if yb_time_major:
        n_t = n_time // tm
        yb_spec = pl.BlockSpec((tm, SSM_WIDTH), lambda i: (i % n_t, i // n_t))
    else:
        yb_spec = row(SSM_WIDTH)
    return pl.pallas_call(
        functools.partial(_merge_kernel, alpha=alpha),
        grid=(m // tm,),
        in_specs=[row(D_MODEL), row(512), yb_spec, row(512), row(N_BRANCH * D_MODEL),
                  const(512, D_MODEL), const(SSM_WIDTH, 2 * D_MODEL), const(512, D_MODEL),
                  const(D_MODEL, D_MODEL), const(1, D_MODEL), const(1, D_MODEL)],
        out_specs=row(D_MODEL),
        out_shape=jax.ShapeDtypeStruct((m, D_MODEL), F32),
        compiler_params=_cparams(("arbitrary",)),
        name="merge",
    )(x2d, ya, yb, yc, gates, wa, wb, wc, wo, ln_g.reshape(1, -1), ln_b.reshape(1, -1))


def _ffn_kernel(x_ref, wg_ref, wu_ref, wd_ref, lng_ref, lnb_ref, o_ref, acc_ref, *, alpha):
    c = pl.program_id(1)

    @pl.when(c == 0)
    def _():
        acc_ref[...] = jnp.zeros_like(acc_ref)

    xb = x_ref[...].astype(BF16)
    a = jax.nn.silu(_mm(xb, wg_ref[...])) * _mm(xb, wu_ref[...])
    acc_ref[...] += _mm(a.astype(BF16), wd_ref[...])

    @pl.when(c == pl.num_programs(1) - 1)
    def _():
        o_ref[...] = _layer_norm(alpha * x_ref[...] + acc_ref[...], lng_ref[...], lnb_ref[...])


def _ffn(x2d, w_gu, w_down, ln_g, ln_b, alpha):
    m = x2d.shape[0]
    tm = min(FFN_TM, m)
    d_ff = w_down.shape[0]
    n_c = d_ff // FFN_FC
    return pl.pallas_call(
        functools.partial(_ffn_kernel, alpha=alpha),
        grid=(m // tm, n_c),
        in_specs=[pl.BlockSpec((tm, D_MODEL), lambda i, c: (i, 0)),
                  pl.BlockSpec((D_MODEL, FFN_FC), lambda i, c: (0, c)),
                  pl.BlockSpec((D_MODEL, FFN_FC), lambda i, c: (0, n_c + c)),
                  pl.BlockSpec((FFN_FC, D_MODEL), lambda i, c: (c, 0)),
                  pl.BlockSpec((1, D_MODEL), lambda i, c: (0, 0)),
                  pl.BlockSpec((1, D_MODEL), lambda i, c: (0, 0))],
        out_specs=pl.BlockSpec((tm, D_MODEL), lambda i, c: (i, 0)),
        out_shape=jax.ShapeDtypeStruct((m, D_MODEL), F32),
        scratch_shapes=[pltpu.VMEM((tm, D_MODEL), F32)],
        compiler_params=_cparams(("arbitrary", "arbitrary")),
        name="ffn",
    )(x2d, w_gu, w_gu, w_down, ln_g.reshape(1, -1), ln_b.reshape(1, -1))


def _moe_kernel(x_ref, wr_ref, br_ref, wgu_ref, wd_ref, lng_ref, lnb_ref, o_ref, acc_ref, gate_ref, *, alpha):
    e = pl.program_id(1)
    tm = x_ref.shape[0]
    xb = x_ref[...].astype(BF16)
    lane = lax.broadcasted_iota(I32, (tm, N_EXPERTS), 1).astype(F32)

    @pl.when(e == 0)
    def _():
        acc_ref[...] = jnp.zeros_like(acc_ref)
        logits = _mm(xb, wr_ref[...]) + br_ref[...]
        m1 = jnp.max(logits, axis=1, keepdims=True)
        i1 = jnp.min(jnp.where(logits == m1, lane, float(N_EXPERTS)), axis=1, keepdims=True)
        rest = jnp.where(lane == i1, -jnp.inf, logits)
        m2 = jnp.max(rest, axis=1, keepdims=True)
        i2 = jnp.min(jnp.where(rest == m2, lane, float(N_EXPERTS)), axis=1, keepdims=True)
        ex = jnp.exp(m2 - m1)
        gate_ref[...] = jnp.where(lane == i1, 1.0 / (1.0 + ex), 0.0) + jnp.where(lane == i2, ex / (1.0 + ex), 0.0)

    ge = jnp.sum(jnp.where(lane == e.astype(F32), gate_ref[...], 0.0), axis=1, keepdims=True)
    h = _mm(xb, wgu_ref[0])
    a = jax.nn.silu(h[:, :D_FF_EXPERT]) * h[:, D_FF_EXPERT:]
    acc_ref[...] += ge * _mm(a.astype(BF16), wd_ref[0])

    @pl.when(e == pl.num_programs(1) - 1)
    def _():
        o_ref[...] = _layer_norm(alpha * x_ref[...] + acc_ref[...], lng_ref[...], lnb_ref[...])


def _moe(x2d, w_router, b_router, w_exp_gu, w_exp_down, ln_g, ln_b, alpha):
    m = x2d.shape[0]
    tm = min(MOE_TM, m)
    return pl.pallas_call(
        functools.partial(_moe_kernel, alpha=alpha),
        grid=(m // tm, N_EXPERTS),
        in_specs=[pl.BlockSpec((tm, D_MODEL), lambda i, e: (i, 0)),
                  pl.BlockSpec((D_MODEL, N_EXPERTS), lambda i, e: (0, 0)),
                  pl.BlockSpec((1, N_EXPERTS), lambda i, e: (0, 0)),
                  pl.BlockSpec((1, D_MODEL, 2 * D_FF_EXPERT), lambda i, e: (e, 0, 0)),
                  pl.BlockSpec((1, D_FF_EXPERT, D_MODEL), lambda i, e: (e, 0, 0)),
                  pl.BlockSpec((1, D_MODEL), lambda i, e: (0, 0)),
                  pl.BlockSpec((1, D_MODEL), lambda i, e: (0, 0))],
        out_specs=pl.BlockSpec((tm, D_MODEL), lambda i, e: (i, 0)),
        out_shape=jax.ShapeDtypeStruct((m, D_MODEL), F32),
        scratch_shapes=[pltpu.VMEM((tm, D_MODEL), F32), pltpu.VMEM((tm, N_EXPERTS), F32)],
        compiler_params=_cparams(("arbitrary", "arbitrary")),
        name="moe",
    )(x2d, w_router, b_router.reshape(1, -1), w_exp_gu, w_exp_down, ln_g.reshape(1, -1), ln_b.reshape(1, -1))


def _pad_rows(a, rows):
    if a.shape[1] == rows:
        return a
    pad = jnp.zeros((a.shape[0], rows - a.shape[1]) + a.shape[2:], a.dtype)
    return jnp.concatenate([a, pad], axis=1)


def _round_up(n, mult):
    return -(-n // mult) * mult


def _token_mixer(x, past, lw):
    bn, n_time, _ = x.shape
    m = bn * n_time
    x2d = x.reshape(m, D_MODEL)
    prompt = past is None
    per_batch = lambda a: a.reshape((bn, -1) + a.shape[1:])

    if prompt:
        past_len = 0
        proj = _project_prompt(x2d, lw["w_in"], lw["b_forget"], bn, n_time)
        a_k, a_v, a_ki, logf = [per_batch(proj[n]) for n in ("a_k", "a_v", "a_ki", "logf")]
        c_k, c_v = per_batch(proj["ck"]), per_batch(proj["cv"])
        logf_all = logf
        h0_re = jnp.zeros((bn, SSM_STATES), F32)
        h0_im = jnp.zeros((bn, SSM_STATES), F32)
        u_tm = proj["u"].reshape(n_time, bn, SSM_WIDTH)
        qa, qit, wt = per_batch(proj["qa"]), proj["qit"], proj["wt"]
        kx, vxa, ki16, vxc = [per_batch(proj[n]) for n in ("kx", "vxa", "ki16", "vxc")]
        kt = proj["kt"].reshape(bn, n_time // FOX_KC, 512, FOX_KC)
    else:
        p_ak, p_av, p_aki, p_hr, p_hi, p_ck, p_cv, p_logf = past
        past_len = p_ak.shape[1]
        proj = _project(x2d, lw["w_in"])
        kv, misc = per_batch(proj["kv"]), per_batch(proj["misc"])
        a_k, a_v = kv[..., :HEAD_DIM_A], kv[..., HEAD_DIM_A:]
        a_ki = misc[..., :IDX_DIM]
        a_w = misc[..., MISC_W_LANE:MISC_W_LANE + N_IDX_HEADS]
        c_f = misc[..., MISC_F_LANE:MISC_F_LANE + N_HEADS_C]
        c_k, c_v = per_batch(proj["ck"]), per_batch(proj["cv"])
        logf = _log_forget(c_f.reshape(m, N_HEADS_C), lw["b_forget"]).reshape(bn, n_time, N_HEADS_C)
        logf_all = jnp.concatenate([p_logf, logf], axis=1)
        h0_re = p_hr.reshape(bn, SSM_STATES)
        h0_im = p_hi.reshape(bn, SSM_STATES)
        u_tm = jnp.swapaxes(per_batch(proj["u"]), 0, 1)
        tq_pad = _round_up(n_time, DSA_TQ)
        qa = _pad_rows(per_batch(proj["qa"]), tq_pad)
        qit = jnp.swapaxes(_pad_rows(per_batch(proj["qi"]), tq_pad), 1, 2)
        wt = jnp.swapaxes(_pad_rows(a_w, tq_pad), 1, 2)
        kx, vxa, ki16, kt, vxc = _cache_layouts(
            p_ak, p_av, p_aki, p_ck.reshape(bn, past_len, -1), p_cv.reshape(bn, past_len, -1),
            _pad_rows(kv, FOX_KC), _pad_rows(misc, FOX_KC), _pad_rows(c_k, FOX_KC), _pad_rows(c_v, FOX_KC))
    kv_len = past_len + n_time
    n_top = min(TOPK_MAX, kv_len // 4)

    y_b, h_re, h_im = _s5(u_tm, h0_re, h0_im, lw["ab_re"], lw["ab_im"], lw["bd"], lw["cr"], lw["ci"], lw["ssm_d"])
    if prompt:
        y_b = y_b.reshape(n_time, bn * SSM_WIDTH)
    else:
        y_b = jnp.swapaxes(y_b, 0, 1).reshape(m, SSM_WIDTH)

    y_a = _dsa(qa, qit, wt, kx, vxa, ki16, lw["bias_near"], lw["bias_lhs"],
               q_off=past_len, kv_len=kv_len, n_top=n_top)[:, :n_time].reshape(m, -1)

    lpc = _round_up(kv_len, FOX_KC)
    dcum, kext = _forget_cumsum(_pad_rows(logf_all, lpc), FOX_KC)
    y_c = _fox(per_batch(proj["cq"]), dcum[:, past_len:kv_len], kt, kext, vxc,
               tq=min(FOX_TQ_PROMPT, n_time), q_off=past_len).reshape(m, -1)

    x1 = _merge(x2d, y_a, y_b, prompt, y_c, proj["gates"], lw["w_a_out"], lw["w_b_glu"], lw["w_c_out"], lw["w_o"],
                lw["ln1_g"], lw["ln1_b"], lw["alpha"], n_time)
    new_state = (a_k, a_v, a_ki, h_re.reshape(bn, N_GROUPS, STATE_DIM), h_im.reshape(bn, N_GROUPS, STATE_DIM),
                 c_k.reshape(bn, n_time, N_HEADS_C, HEAD_DIM_C), c_v.reshape(bn, n_time, N_HEADS_C, HEAD_DIM_C), logf)
    return x1, new_state


def kernel(x_prompt, x_sample, cache_a_k, cache_a_v, cache_a_kidx, state_ssm_re, state_ssm_im, cache_c_k, cache_c_v, cache_c_logf, w_in, b_forget, ssm_lam_re, ssm_lam_im, ssm_log_step, ssm_b_re, ssm_b_im, ssm_c_re, ssm_c_im, ssm_d, w_a_out, w_b_glu, w_c_out, w_o, ln1_g, ln1_b, ln2_g, ln2_b, t5_table, w_ffn_gu, w_ffn_down, w_router, b_router, w_exp_gu, w_exp_down):
    depth = w_in.shape[0]
    alpha = float((2 * depth) ** 0.25)
    bias_near, bias_lhs = _dsa_bias_consts(t5_table)
    xp, xs = x_prompt, x_sample
    rows_p = [[] for _ in range(8)]
    rows_s = [[] for _ in range(8)]
    for layer in range(depth):
        ab_re, ab_im, bb_re, bb_im = _s5_discretize(ssm_lam_re[layer], ssm_lam_im[layer], ssm_log_step[layer],
                                                    ssm_b_re[layer], ssm_b_im[layer])
        lw = dict(
            w_in=_pack_w_in(w_in[layer]), b_forget=b_forget[layer],
            ab_re=ab_re.reshape(1, SSM_STATES), ab_im=ab_im.reshape(1, SSM_STATES),
            bd=jnp.concatenate([_block_diag(bb_re), _block_diag(bb_im)], axis=1).astype(BF16),
            cr=_block_diag(jnp.swapaxes(ssm_c_re[layer], 1, 2)).astype(BF16),
            ci=_block_diag(jnp.swapaxes(ssm_c_im[layer], 1, 2)).astype(BF16),
            ssm_d=ssm_d[layer].reshape(1, SSM_WIDTH),
            w_a_out=w_a_out[layer].astype(BF16), w_b_glu=w_b_glu[layer].astype(BF16),
            w_c_out=w_c_out[layer].astype(BF16), w_o=w_o[layer].astype(BF16),
            ln1_g=ln1_g[layer], ln1_b=ln1_b[layer], bias_near=bias_near, bias_lhs=bias_lhs, alpha=alpha)
        past = (cache_a_k[layer], cache_a_v[layer], cache_a_kidx[layer], state_ssm_re[layer],
                state_ssm_im[layer], cache_c_k[layer], cache_c_v[layer], cache_c_logf[layer])
        xp1, st_p = _token_mixer(xp, None, lw)
        xs1, st_s = _token_mixer(xs, past, lw)
        i = layer // 2
        if layer % 2 == 0:
            wgu, wdn = w_ffn_gu[i].astype(BF16), w_ffn_down[i].astype(BF16)
            xp2 = _ffn(xp1, wgu, wdn, ln2_g[layer], ln2_b[layer], alpha)
            xs2 = _ffn(xs1, wgu, wdn, ln2_g[layer], ln2_b[layer], alpha)
        else:
            wr, wgu, wdn = w_router[i].astype(BF16), w_exp_gu[i].astype(BF16), w_exp_down[i].astype(BF16)
            xp2 = _moe(xp1, wr, b_router[i], wgu, wdn, ln2_g[layer], ln2_b[layer], alpha)
            xs2 = _moe(xs1, wr, b_router[i], wgu, wdn, ln2_g[layer], ln2_b[layer], alpha)
        xp = xp2.reshape(x_prompt.shape)
        xs = xs2.reshape(x_sample.shape)
        for j in range(8):
            rows_p[j].append(st_p[j])
            rows_s[j].append(st_s[j])
    (a_k_p, a_v_p, a_kidx_p, ssm_re_p, ssm_im_p, c_k_p, c_v_p, c_logf_p) = [jnp.stack(r) for r in rows_p]
    (a_k_s, a_v_s, a_kidx_s, ssm_re_s, ssm_im_s, c_k_s, c_v_s, c_logf_s) = [jnp.stack(r) for r in rows_s]
    return (xp, xs, a_k_p, a_k_s, a_v_p, a_v_s, a_kidx_p, a_kidx_s, ssm_re_p, ssm_re_s,
            ssm_im_p, ssm_im_s, c_k_p, c_k_s, c_v_p, c_v_s, c_logf_p, c_logf_s)
```

```python
import functools
import math

import jax
import jax.numpy as jnp
import numpy as np
from jax import lax
from jax.experimental import pallas as pl
from jax.experimental.pallas import tpu as pltpu

F32 = jnp.float32
BF16 = jnp.bfloat16
I32 = jnp.int32

D_MODEL = 1024
CHUNK = 64
N_HEADS_A = 8
HEAD_DIM_A = 64
N_IDX_HEADS = 8
IDX_DIM = 32
TOPK_MAX = 256
NUM_BUCKETS = 32
MAX_DISTANCE = 128
SSM_WIDTH = 512
GROUP_SIZE = 16
N_GROUPS = SSM_WIDTH // GROUP_SIZE
STATE_DIM = 64
SSM_STATES = N_GROUPS * STATE_DIM
N_HEADS_C = 8
HEAD_DIM_C = 64
N_BRANCH = 3
D_FF = 2816
N_EXPERTS = 8
D_FF_EXPERT = 1408
LN_EPS = 1e-5
PROJ_SIZES = (512, 64, 64, 256, 32, 8, 512, 512, 512, 512, 8, 3072)

LANES = 128
SUBLANES = 8
VMEM_LIMIT_BYTES = 56 * 1024 * 1024

PROJ_TM = 256
S5_ROWS = 512
S5_CW = 256
S5_SLAB_GROUPS = 8
DSA_TQ = 128
DSA_KC = 256
FOX_TQ_PROMPT = 256
FOX_KC = 256
OUT_TM = 256
FFN_TM = 1024
FFN_FC = 256
MOE_TM = 512

INT_MIN = -(2 ** 31)
SKEY_NEG_INF = -2139095041


def _cparams(sem):
    return pltpu.CompilerParams(dimension_semantics=sem, vmem_limit_bytes=VMEM_LIMIT_BYTES)


def _mm(a, b):
    return jnp.dot(a, b, preferred_element_type=F32)


def _mm_nt(a, b):
    return lax.dot_general(a, b, (((1,), (1,)), ((), ())), preferred_element_type=F32)


def _layer_norm(z, g, b):
    mu = jnp.mean(z, axis=-1, keepdims=True)
    zc = z - mu
    var = jnp.mean(zc * zc, axis=-1, keepdims=True)
    return zc * lax.rsqrt(var + LN_EPS) * g + b


def _pack_w_in(w):
    offs = np.cumsum((0,) + PROJ_SIZES)
    a_q, a_k, a_v, a_qi, a_ki, a_w, b_u, c_q, c_k, c_v, c_f, gates = [
        w[:, offs[i]:offs[i + 1]] for i in range(len(PROJ_SIZES))]
    pad = jnp.zeros((w.shape[0], LANES - IDX_DIM - N_IDX_HEADS - N_HEADS_C), w.dtype)
    return jnp.concatenate([a_q, a_k, a_v, a_qi, a_ki, a_w, c_f, pad, b_u, c_q, c_k, c_v, gates],
                           axis=1).astype(BF16)


PROJ_OUT = (("qa", 0, 512), ("kv", 512, 128), ("qi", 640, 256), ("misc", 896, 128), ("u", 1024, 512),
            ("cq", 1536, 512), ("ck", 2048, 512), ("cv", 2560, 512), ("gates", 3072, 3072))
PROJ_COLS_PACKED = 6144


def _proj_kernel(x_ref, w_ref, *out_refs):
    xb = x_ref[...].astype(BF16)
    for (_, lo, width), o_ref in zip(PROJ_OUT, out_refs):
        for c in range(0, width, 512):
            cw = min(512, width - c)
            o_ref[:, c:c + cw] = _mm(xb, w_ref[:, lo + c:lo + c + cw])


def _project(x2d, w_packed):
    m = x2d.shape[0]
    tm = min(PROJ_TM, m)
    outs = pl.pallas_call(
        _proj_kernel,
        grid=(m // tm,),
        in_specs=[pl.BlockSpec((tm, D_MODEL), lambda i: (i, 0)),
                  pl.BlockSpec((D_MODEL, PROJ_COLS_PACKED), lambda i: (0, 0))],
        out_specs=[pl.BlockSpec((tm, width), lambda i: (i, 0)) for _, _, width in PROJ_OUT],
        out_shape=[jax.ShapeDtypeStruct((m, width), F32) for _, _, width in PROJ_OUT],
        compiler_params=_cparams(("arbitrary",)),
        name="proj",
    )(x2d, w_packed)
    return dict(zip([p[0] for p in PROJ_OUT], outs))


MISC_W_LANE = IDX_DIM
MISC_F_LANE = IDX_DIM + N_IDX_HEADS
DSA_K_ONES = 3


def _attn_layouts(kv, misc, ck, cv, kx_ref, vxa_ref, ki16_ref, kt_ref, vxc_ref):
    rows = kv.shape[0]
    lane = lax.broadcasted_iota(I32, (rows, LANES), 1)
    ones_k = jnp.where((lane >= HEAD_DIM_A) & (lane < HEAD_DIM_A + DSA_K_ONES), 1.0, 0.0)
    ones_v = jnp.where(lane == HEAD_DIM_A, 1.0, 0.0)
    kx_ref[...] = jnp.where(lane < HEAD_DIM_A, kv, ones_k).astype(BF16)
    vxa_ref[...] = jnp.where(lane < HEAD_DIM_A, pltpu.roll(kv, HEAD_DIM_A, axis=1), ones_v).astype(BF16)
    ki16_ref[...] = misc[:, :IDX_DIM].astype(BF16)
    kt_ref[...] = ck.T.astype(BF16)
    for h in range(N_HEADS_C):
        pair = cv[:, (h // 2) * LANES:(h // 2 + 1) * LANES]
        if h % 2:
            pair = pltpu.roll(pair, HEAD_DIM_C, axis=1)
        vxc_ref[:, h * LANES:(h + 1) * LANES] = jnp.where(lane < HEAD_DIM_C, pair, ones_v).astype(BF16)


def _proj_prompt_kernel(x_ref, w_ref, bf_ref, *refs, n_prev):
    if n_prev:
        pck_ref, pcv_ref = refs[:2]
        refs = refs[2:]
    (qa_ref, qit_ref, wt_ref, ak_ref, av_ref, aki_ref, logf_ref, u_ref, cq_ref, ck_ref, cv_ref, g_ref,
     kx_ref, vxa_ref, ki16_ref, kt_ref, vxc_ref) = refs
    xb = x_ref[...].astype(BF16)
    cols = {name: (lo, width) for name, lo, width in PROJ_OUT}

    def mm(name, c0=0, cw=None):
        lo, width = cols[name]
        cw = width if cw is None else cw
        return _mm(xb, w_ref[:, lo + c0:lo + c0 + cw])

    qa_ref[...] = mm("qa")
    u_ref[...] = mm("u")
    for c in range(0, cols["gates"][1], 512):
        g_ref[:, c:c + 512] = mm("gates", c, 512)
    small = mm("kv", 0, cols["kv"][1] + cols["qi"][1] + cols["misc"][1])
    kv = small[:, :LANES]
    qi = small[:, LANES:LANES + cols["qi"][1]]
    misc = small[:, LANES + cols["qi"][1]:]
    cqk = mm("cq", 0, cols["cq"][1] + cols["ck"][1])
    cq_ref[...] = cqk[:, :cols["cq"][1]]
    ck = cqk[:, cols["cq"][1]:]
    cv = mm("cv")
    if n_prev:
        ck_ref[:n_prev] = pck_ref[...]
        cv_ref[:n_prev] = pcv_ref[...]
    ck_ref[n_prev] = ck
    cv_ref[n_prev] = cv
    qit_ref[...] = qi.T
    wt_ref[...] = misc.T[MISC_W_LANE:MISC_W_LANE + N_IDX_HEADS, :]
    ak_ref[...] = kv[:, :HEAD_DIM_A]
    av_ref[...] = pltpu.roll(kv, HEAD_DIM_A, axis=1)[:, :HEAD_DIM_A]
    aki_ref[...] = misc[:, :IDX_DIM]
    lf = jax.nn.log_sigmoid(misc + bf_ref[...])
    logf_ref[...] = pltpu.roll(lf, LANES - MISC_F_LANE, axis=1)[:, :N_HEADS_C]
    _attn_layouts(kv, misc, ck, cv, kx_ref, vxa_ref, ki16_ref, kt_ref.at[0], vxc_ref)


def _project_prompt(x2d, w_packed, b_forget, n_batch, n_time, prev_ck, prev_cv):
    m = x2d.shape[0]
    n_prev = 0 if prev_ck is None else prev_ck.shape[0]
    tm = FOX_KC
    n_t = n_time // tm
    bf = jnp.zeros((1, LANES), F32).at[0, MISC_F_LANE:MISC_F_LANE + N_HEADS_C].set(b_forget)
    row = lambda w, dt=F32: (jax.ShapeDtypeStruct((m, w), dt), pl.BlockSpec((tm, w), lambda i: (i, 0)))
    col = lambda r: (jax.ShapeDtypeStruct((r, m), F32), pl.BlockSpec((r, tm), lambda i: (0, i)))
    stacked = (jax.ShapeDtypeStruct((n_prev + 1, m, 512), F32), pl.BlockSpec((n_prev + 1, tm, 512), lambda i: (0, i, 0)))
    prev_specs = [pl.BlockSpec((n_prev, tm, 512), lambda i: (0, i, 0))] * 2 if n_prev else []
    prev_args = [prev_ck, prev_cv] if n_prev else []
    outs = dict(
        qa=row(512), qit=col(N_IDX_HEADS * IDX_DIM), wt=col(N_IDX_HEADS),
        a_k=row(HEAD_DIM_A), a_v=row(HEAD_DIM_A), a_ki=row(IDX_DIM), logf=row(N_HEADS_C),
        u=(jax.ShapeDtypeStruct((n_time, n_batch * SSM_WIDTH), F32),
           pl.BlockSpec((tm, SSM_WIDTH), lambda i: (i % n_t, i // n_t))),
        cq=row(512), ck=stacked, cv=stacked, gates=row(N_BRANCH * D_MODEL),
        kx=row(LANES, BF16), vxa=row(LANES, BF16), ki16=row(IDX_DIM, BF16),
        kt=(jax.ShapeDtypeStruct((m // tm, 512, tm), BF16), pl.BlockSpec((1, 512, tm), lambda i: (i, 0, 0))),
        vxc=row(N_HEADS_C * LANES, BF16))
    res = pl.pallas_call(
        functools.partial(_proj_prompt_kernel, n_prev=n_prev),
        grid=(m // tm,),
        in_specs=[pl.BlockSpec((tm, D_MODEL), lambda i: (i, 0)),
                  pl.BlockSpec((D_MODEL, PROJ_COLS_PACKED), lambda i: (0, 0)),
                  pl.BlockSpec((1, LANES), lambda i: (0, 0))] + prev_specs,
        out_specs=[v[1] for v in outs.values()],
        out_shape=[v[0] for v in outs.values()],
        compiler_params=_cparams(("arbitrary",)),
        name="proj_prompt",
    )(x2d, w_packed, bf, *prev_args)
    return dict(zip(outs.keys(), res))


def _cache_layout_kernel(pk_ref, pv_ref, pki_ref, pck_ref, pcv_ref, nkv_ref, nmisc_ref, nck_ref, ncv_ref,
                         kx_ref, vxa_ref, ki16_ref, kt_ref, vxc_ref, *, n_past):
    is_new = pl.program_id(1) >= n_past
    rows = nkv_ref.shape[1]
    lane = lax.broadcasted_iota(I32, (rows, LANES), 1)
    zeros_k = jnp.zeros((rows, LANES - HEAD_DIM_A), F32)
    past_kv = jnp.where(lane < HEAD_DIM_A, jnp.concatenate([pk_ref[0], zeros_k], axis=1),
                        pltpu.roll(jnp.concatenate([pv_ref[0], zeros_k], axis=1), HEAD_DIM_A, axis=1))
    past_misc = jnp.concatenate([pki_ref[0], jnp.zeros((rows, LANES - IDX_DIM), F32)], axis=1)
    kv = jnp.where(is_new, nkv_ref[0], past_kv)
    misc = jnp.where(is_new, nmisc_ref[0], past_misc)
    ck = jnp.where(is_new, nck_ref[0], pck_ref[0])
    cv = jnp.where(is_new, ncv_ref[0], pcv_ref[0])
    _attn_layouts(kv, misc, ck, cv, kx_ref.at[0], vxa_ref.at[0], ki16_ref.at[0], kt_ref.at[0, 0], vxc_ref.at[0])


def _cache_layouts(p_k, p_v, p_ki, p_ck, p_cv, n_kv, n_misc, n_ck, n_cv):
    bn, n_rows, _ = p_k.shape
    kc = FOX_KC
    n_past = n_rows // kc
    n_chunks = n_past + 1
    lp = n_chunks * kc
    past = lambda w: pl.BlockSpec((1, kc, w), lambda b, c: (b, jnp.minimum(c, n_past - 1), 0))
    new = lambda w: pl.BlockSpec((1, kc, w), lambda b, c: (b, 0, 0))
    out = lambda w: pl.BlockSpec((1, kc, w), lambda b, c: (b, c, 0))
    return pl.pallas_call(
        functools.partial(_cache_layout_kernel, n_past=n_past),
        grid=(bn, n_chunks),
        in_specs=[past(HEAD_DIM_A), past(HEAD_DIM_A), past(IDX_DIM), past(512), past(512),
                  new(LANES), new(LANES), new(512), new(512)],
        out_specs=[out(LANES), out(LANES), out(IDX_DIM),
                   pl.BlockSpec((1, 1, 512, kc), lambda b, c: (b, c, 0, 0)), out(N_HEADS_C * LANES)],
        out_shape=[jax.ShapeDtypeStruct((bn, lp, LANES), BF16), jax.ShapeDtypeStruct((bn, lp, LANES), BF16),
                   jax.ShapeDtypeStruct((bn, lp, IDX_DIM), BF16),
                   jax.ShapeDtypeStruct((bn, n_chunks, 512, kc), BF16),
                   jax.ShapeDtypeStruct((bn, lp, N_HEADS_C * LANES), BF16)],
        compiler_params=_cparams(("arbitrary", "arbitrary")),
        name="cache_layouts",
    )(p_k, p_v, p_ki, p_ck, p_cv, n_kv, n_misc, n_ck, n_cv)


def _s5_disc_kernel(lr_ref, li_ref, ls_ref, br_ref, bi_ref, ar_ref, ai_ref, bbr_ref, bbi_ref):
    lr, li = lr_ref[...], li_ref[...]
    dt = jnp.exp(ls_ref[...])
    mag = jnp.exp(lr * dt)
    ab_re = mag * jnp.cos(li * dt)
    ab_im = mag * jnp.sin(li * dt)
    den = lr * lr + li * li
    fr = ((ab_re - 1.0) * lr + ab_im * li) / den
    fi = (ab_im * lr - (ab_re - 1.0) * li) / den
    ar_ref[...] = ab_re
    ai_ref[...] = ab_im
    br, bi = br_ref[...], bi_ref[...]
    bbr_ref[...] = fr[:, None, :] * br - fi[:, None, :] * bi
    bbi_ref[...] = fr[:, None, :] * bi + fi[:, None, :] * br


def _s5_discretize(lam_re, lam_im, log_step, b_re, b_im):
    g, p, gs = b_re.shape
    shp = lambda *s: jax.ShapeDtypeStruct(s, F32)
    return pl.pallas_call(
        _s5_disc_kernel,
        out_shape=[shp(g, p), shp(g, p), shp(g, gs, p), shp(g, gs, p)],
        name="s5_disc",
    )(lam_re, lam_im, log_step.reshape(g, 1), jnp.swapaxes(b_re, 1, 2), jnp.swapaxes(b_im, 1, 2))


def _block_diag(blocks):
    g, r, c = blocks.shape
    eye = jnp.eye(g, dtype=bool)
    return jnp.where(eye[:, None, :, None], blocks[:, :, None, :], 0.0).reshape(g * r, g * c)


def _s5_kernel(u_ref, h0r_ref, h0i_ref, ar_ref, ai_ref, bd_ref, cr_ref, ci_ref, d_ref,
               y_ref, hr_ref, hi_ref, bur_ref, bui_ref, *, tc, bn):
    rows = tc * bn

    @pl.when(pl.program_id(0) == 0)
    def _():
        hr_ref[...] = h0r_ref[...]
        hi_ref[...] = h0i_ref[...]

    u = u_ref[...].reshape(rows, SSM_WIDTH)
    ub = u.astype(BF16)
    n_slab = N_GROUPS // S5_SLAB_GROUPS
    cw = S5_SLAB_GROUPS * GROUP_SIZE
    sw = S5_SLAB_GROUPS * STATE_DIM
    for k in range(n_slab):
        uk = ub[:, k * cw:(k + 1) * cw]
        bur_ref[:, k * sw:(k + 1) * sw] = _mm(uk, bd_ref[k * cw:(k + 1) * cw, k * sw:(k + 1) * sw])
        bui_ref[:, k * sw:(k + 1) * sw] = _mm(
            uk, bd_ref[k * cw:(k + 1) * cw, SSM_STATES + k * sw:SSM_STATES + (k + 1) * sw])

    for cc in range(SSM_STATES // S5_CW):
        sl = slice(cc * S5_CW, (cc + 1) * S5_CW)
        ar = jnp.broadcast_to(ar_ref[:, sl], (bn, S5_CW))
        ai = jnp.broadcast_to(ai_ref[:, sl], (bn, S5_CW))

        def step(t, carry, sl=sl, ar=ar, ai=ai):
            hr, hi = carry
            r0 = pl.multiple_of(t * bn, bn)
            nr = ar * hr - ai * hi + bur_ref[pl.ds(r0, bn), sl]
            ni = ar * hi + ai * hr + bui_ref[pl.ds(r0, bn), sl]
            bur_ref[pl.ds(r0, bn), sl] = nr
            bui_ref[pl.ds(r0, bn), sl] = ni
            return nr, ni

        hr, hi = lax.fori_loop(0, tc, step, (hr_ref[:, sl], hi_ref[:, sl]))
        hr_ref[:, sl] = hr
        hi_ref[:, sl] = hi

    for k in range(n_slab):
        hr = bur_ref[:, k * sw:(k + 1) * sw].astype(BF16)
        hi = bui_ref[:, k * sw:(k + 1) * sw].astype(BF16)
        yk = (_mm(hr, cr_ref[k * sw:(k + 1) * sw, k * cw:(k + 1) * cw])
              - _mm(hi, ci_ref[k * sw:(k + 1) * sw, k * cw:(k + 1) * cw])
              + d_ref[:, k * cw:(k + 1) * cw] * u[:, k * cw:(k + 1) * cw])
        y_ref[:, :, k * cw:(k + 1) * cw] = yk.reshape(tc, bn, cw)


def _s5(u_tm, h0_re, h0_im, ab_re, ab_im, bd, cr, ci, d_skip):
    n_time, bn, _ = u_tm.shape
    tc = min(S5_ROWS // bn, n_time)
    const = lambda *s: pl.BlockSpec(s, lambda i: (0,) * len(s))
    y, hr, hi = pl.pallas_call(
        functools.partial(_s5_kernel, tc=tc, bn=bn),
        grid=(n_time // tc,),
        in_specs=[pl.BlockSpec((tc, bn, SSM_WIDTH), lambda i: (i, 0, 0)),
                  const(bn, SSM_STATES), const(bn, SSM_STATES),
                  const(1, SSM_STATES), const(1, SSM_STATES),
                  const(SSM_WIDTH, 2 * SSM_STATES),
                  const(SSM_STATES, SSM_WIDTH), const(SSM_STATES, SSM_WIDTH),
                  const(1, SSM_WIDTH)],
        out_specs=[pl.BlockSpec((tc, bn, SSM_WIDTH), lambda i: (i, 0, 0)),
                   const(bn, SSM_STATES), const(bn, SSM_STATES)],
        out_shape=[jax.ShapeDtypeStruct((n_time, bn, SSM_WIDTH), F32),
                   jax.ShapeDtypeStruct((bn, SSM_STATES), F32),
                   jax.ShapeDtypeStruct((bn, SSM_STATES), F32)],
        scratch_shapes=[pltpu.VMEM((tc * bn, SSM_STATES), F32), pltpu.VMEM((tc * bn, SSM_STATES), F32)],
        compiler_params=_cparams(("arbitrary",)),
        name="s5",
    )(u_tm, h0_re, h0_im, ab_re, ab_im, bd, cr, ci, d_skip)
    return y, hr, hi


def _t5_bucket(rel):
    half = NUM_BUCKETS // 2
    max_exact = half // 2
    n = jnp.abs(rel)
    large = max_exact + (jnp.log(jnp.maximum(n, 1).astype(F32) / max_exact)
                         / math.log(MAX_DISTANCE / max_exact) * (half - max_exact)).astype(I32)
    large = jnp.minimum(large, half - 1)
    return jnp.where(rel > 0, half, 0) + jnp.where(n < max_exact, n, large)


DSA_NEAR_TILES = 3
DSA_SPLIT = 3
MASK_OFF = -(2.0 ** 100)


def _split3(x):
    parts, rem = [], x
    for _ in range(DSA_SPLIT):
        p = rem.astype(BF16).astype(F32)
        parts.append(p)
        rem = rem - p
    return parts


def _dsa_bias_consts(t5_table):
    q = jnp.arange(DSA_TQ, dtype=I32)[:, None]
    k = jnp.arange(DSA_KC, dtype=I32)[None, :]
    def lookup(bucket):
        out = jnp.zeros((N_HEADS_A,) + bucket.shape, F32)
        for b in range(NUM_BUCKETS):
            out = out + jnp.where(bucket[None] == b, t5_table[b].astype(F32).reshape((-1,) + (1,) * bucket.ndim), 0.0)
        return out

    far = lookup(_t5_bucket(jnp.full((), -(1 << 20), I32)))
    near = []
    for e in range(DSA_NEAR_TILES):
        near.append(lookup(_t5_bucket(k - q - LANES * e)) - far[:, None, None])
    near.append(jnp.zeros_like(near[0]))
    near = jnp.stack(near).reshape(DSA_NEAR_TILES + 1, N_HEADS_A * DSA_TQ, DSA_KC)
    lane = jnp.arange(LANES)[None, None, :]
    left = jnp.zeros((N_HEADS_A, DSA_TQ, LANES), F32)
    for i, part in enumerate(_split3(far)):
        left = jnp.where(lane == HEAD_DIM_A + i, part[:, None, None], left)
    eye = jnp.broadcast_to(jnp.eye(DSA_TQ, dtype=F32)[None], (N_HEADS_A, DSA_TQ, DSA_TQ))
    lhs_static = jnp.concatenate([left, eye], axis=-1).reshape(N_HEADS_A * DSA_TQ, 2 * LANES).astype(BF16)
    return near, lhs_static


def _sort_key(s):
    bits = lax.bitcast_convert_type(s, I32)
    return bits ^ (lax.shift_right_arithmetic(bits, 31) & 0x7FFFFFFF)


def _dsa_kernel(qa_ref, qit_ref, wt_ref, kx_ref, vx_ref, ki_ref, near_ref, lhs_ref, o_ref,
                skey_ref, mask_ref, thr_ref, tie_ref, *, q_off, kv_len, n_top, idx_bits, chunk_counts):
    tq, kc = DSA_TQ, DSA_KC
    qb = pl.program_id(1)
    q_base = q_off + qb * tq
    adm_end = jnp.minimum(((q_base + tq - 1) // CHUNK + 1) * CHUNK, kv_len)
    n_c = (adm_end + kc - 1) // kc

    q_pos_t = q_base + lax.broadcasted_iota(I32, (kc, tq), 1)
    k_loc_t = lax.broadcasted_iota(I32, (kc, tq), 0)

    qit = qit_ref[...].astype(BF16)
    w = wt_ref[...]
    rhs_pairs = [jnp.concatenate([qit[(2 * j) * IDX_DIM:(2 * j + 1) * IDX_DIM, :],
                                  qit[(2 * j + 1) * IDX_DIM:(2 * j + 2) * IDX_DIM, :]], axis=1)
                 for j in range(N_IDX_HEADS // 2)]

    def score_chunk(c, _):
        r0 = pl.multiple_of(c * kc, kc)
        kic = ki_ref[0, pl.ds(r0, kc), :]
        acc = jnp.zeros((kc, tq), F32)
        for j in range(N_IDX_HEADS // 2):
            d = jnp.maximum(_mm(kic, rhs_pairs[j]), 0.0)
            acc = acc + w[2 * j:2 * j + 1, :] * d[:, :tq] + w[2 * j + 1:2 * j + 2, :] * d[:, tq:]
        acc = jnp.where(acc == 0.0, 0.0, acc)
        k_pos = r0 + k_loc_t
        adm = ((k_pos // CHUNK) <= (q_pos_t // CHUNK)) & (k_pos < kv_len)
        skey_ref[pl.ds(r0, kc), :] = _sort_key(jnp.where(adm, acc, -jnp.inf))
        return 0

    lax.fori_loop(0, n_c, score_chunk, 0)

    def fold(ind):
        return jnp.sum(ind.reshape(kc // SUBLANES, SUBLANES, tq), axis=0)

    for nc in chunk_counts:
        @pl.when(n_c == nc)
        def _(nc=nc):
            def value_bit(i, t):
                cand = t ^ lax.shift_left(jnp.int32(1), 31 - i)
                part = jnp.zeros((SUBLANES, tq), F32)
                for c in range(nc):
                    part = part + fold(jnp.where(skey_ref[c * kc:(c + 1) * kc, :] >= cand, 1.0, 0.0))
                cnt = jnp.sum(part, axis=0, keepdims=True)
                return jnp.where(cnt >= n_top, cand, t)
            thr_ref[...] = lax.fori_loop(0, 32, value_bit, jnp.full((1, tq), INT_MIN, I32))

    thr = thr_ref[...]

    def count(pred_fn):
        def body(c, part):
            r0 = pl.multiple_of(c * kc, kc)
            return part + fold(jnp.where(pred_fn(skey_ref[pl.ds(r0, kc), :], r0 + k_loc_t), 1.0, 0.0))
        return jnp.sum(lax.fori_loop(0, n_c, body, jnp.zeros((SUBLANES, tq), F32)), axis=0, keepdims=True)

    cnt_gt = count(lambda sk, _: sk > thr)
    cnt_ge = count(lambda sk, _: sk >= thr)
    need = n_top - cnt_gt

    tie_ref[...] = jnp.full((1, tq), 1 << 30, I32)
    has_tie = (cnt_ge > n_top) & (thr > SKEY_NEG_INF)

    @pl.when(jnp.max(jnp.where(has_tie, 1.0, 0.0)) > 0.5)
    def _():
        def index_bit(i, j):
            cand = j | lax.shift_left(jnp.int32(1), idx_bits - 1 - i)
            cnt = count(lambda sk, kp: (sk == thr) & (kp < cand))
            return jnp.where(cnt <= need - 1.0, cand, j)
        tie_ref[...] = lax.fori_loop(0, idx_bits, index_bit, jnp.zeros((1, tq), I32))

    tie_idx = tie_ref[...]

    def mask_chunk(c, _):
        r0 = pl.multiple_of(c * kc, kc)
        sk = skey_ref[pl.ds(r0, kc), :]
        sel = ((sk > thr) | ((sk == thr) & (r0 + k_loc_t <= tie_idx))) & (sk > SKEY_NEG_INF)
        mask_ref[pl.ds(r0, kc), :] = jnp.where(sel, 0.0, MASK_OFF).astype(BF16)
        return 0

    lax.fori_loop(0, n_c, mask_chunk, 0)

    lane = lax.broadcasted_iota(I32, (tq, LANES), 1)
    q_rows = []
    for h in range(N_HEADS_A):
        pair = qa_ref[:, (h // 2) * LANES:(h // 2 + 1) * LANES] * (HEAD_DIM_A ** -0.5)
        if h % 2:
            pair = pltpu.roll(pair, HEAD_DIM_A, axis=1)
        q_rows.append(pair)
    q_left = jnp.concatenate(q_rows, axis=0).astype(BF16)
    lane_all = lax.broadcasted_iota(I32, (N_HEADS_A * tq, LANES), 1)
    lhs = jnp.concatenate([jnp.where(lane_all < HEAD_DIM_A, q_left, lhs_ref[:, :LANES]), lhs_ref[:, LANES:]],
                          axis=1)

    def logits(c):
        r0 = pl.multiple_of(c * kc, kc)
        rhs = jnp.concatenate([kx_ref[0, pl.ds(r0, kc), :], mask_ref[pl.ds(r0, kc), :]], axis=1)
        return _mm_nt(lhs, rhs) + near_ref[jnp.clip(q_base // LANES - c * (kc // LANES), 0, DSA_NEAR_TILES)]

    def attend(c, carry):
        s, m, acc = carry
        s_next = logits(jnp.minimum(c + 1, n_c - 1))
        r0 = pl.multiple_of(c * kc, kc)
        m_new = jnp.maximum(m, jnp.max(s, axis=1, keepdims=True))
        p = jnp.exp(s - m_new)
        acc = jnp.exp(m - m_new) * acc + _mm(p.astype(BF16), vx_ref[0, pl.ds(r0, kc), :])
        return s_next, m_new, acc

    rows = N_HEADS_A * tq
    _, m, acc = lax.fori_loop(
        0, n_c, attend,
        (logits(0), jnp.full((rows, 1), -jnp.inf, F32), jnp.zeros((rows, LANES), F32)))
    out = acc / acc[:, HEAD_DIM_A:HEAD_DIM_A + 1]
    for j in range(N_HEADS_A // 2):
        even = out[(2 * j) * tq:(2 * j + 1) * tq]
        odd = pltpu.roll(out[(2 * j + 1) * tq:(2 * j + 2) * tq], HEAD_DIM_A, axis=1)
        o_ref[:, j * LANES:(j + 1) * LANES] = jnp.where(lane < HEAD_DIM_A, even, odd)


def _dsa(qa, qit, wt, kx, vx, ki, near, lhs_static, *, q_off, kv_len, n_top):
    bn, tq_total, _ = qa.shape
    lp = kx.shape[1]
    nq = tq_total // DSA_TQ
    idx_bits = max(1, int(lp - 1).bit_length())
    counts = sorted({-(-min(((q_off + (qb + 1) * DSA_TQ - 1) // CHUNK + 1) * CHUNK, kv_len) // DSA_KC)
                     for qb in range(nq)})

    def qside(a):
        if a.ndim == 3:
            return pl.BlockSpec((None, a.shape[1], DSA_TQ), lambda b, q: (b, 0, q))
        return pl.BlockSpec((a.shape[0], DSA_TQ), lambda b, q: (0, b * nq + q))

    return pl.pallas_call(
        functools.partial(_dsa_kernel, q_off=q_off, kv_len=kv_len, n_top=float(n_top), idx_bits=idx_bits,
                          chunk_counts=tuple(counts)),
        grid=(bn, nq),
        in_specs=[pl.BlockSpec((None, DSA_TQ, N_HEADS_A * HEAD_DIM_A), lambda b, q: (b, q, 0)),
                  qside(qit), qside(wt),
                  pl.BlockSpec((1, lp, LANES), lambda b, q: (b, 0, 0)),
                  pl.BlockSpec((1, lp, LANES), lambda b, q: (b, 0, 0)),
                  pl.BlockSpec((1, lp, IDX_DIM), lambda b, q: (b, 0, 0)),
                  pl.BlockSpec((DSA_NEAR_TILES + 1, N_HEADS_A * DSA_TQ, DSA_KC), lambda b, q: (0, 0, 0)),
                  pl.BlockSpec((N_HEADS_A * DSA_TQ, 2 * LANES), lambda b, q: (0, 0))],
        out_specs=pl.BlockSpec((None, DSA_TQ, N_HEADS_A * HEAD_DIM_A), lambda b, q: (b, q, 0)),
        out_shape=jax.ShapeDtypeStruct((bn, tq_total, N_HEADS_A * HEAD_DIM_A), F32),
        scratch_shapes=[pltpu.VMEM((lp, DSA_TQ), I32), pltpu.VMEM((lp, DSA_TQ), BF16),
                        pltpu.VMEM((1, DSA_TQ), I32), pltpu.VMEM((1, DSA_TQ), I32)],
        compiler_params=_cparams(("arbitrary", "arbitrary")),
        name="dsa",
    )(qa, qit, wt, kx, vx, ki, near, lhs_static)


def _logf_kernel(cf_ref, b_ref, o_ref):
    o_ref[...] = jax.nn.log_sigmoid(cf_ref[...] + b_ref[...])


def _log_forget(cf2d, b_forget):
    m = cf2d.shape[0]
    tm = min(2048, m)
    return pl.pallas_call(
        _logf_kernel,
        grid=(m // tm,),
        in_specs=[pl.BlockSpec((tm, N_HEADS_C), lambda i: (i, 0)), pl.BlockSpec((1, N_HEADS_C), lambda i: (0, 0))],
        out_specs=pl.BlockSpec((tm, N_HEADS_C), lambda i: (i, 0)),
        out_shape=jax.ShapeDtypeStruct((m, N_HEADS_C), F32),
        compiler_params=_cparams(("arbitrary",)),
        name="logf",
    )(cf2d, b_forget.reshape(1, N_HEADS_C))


FOX_EXT_ROWS = 16
FOX_SPLIT = 3


def _split_bf16(x):
    parts = []
    rem = x
    for _ in range(FOX_SPLIT):
        p = rem.astype(BF16).astype(F32)
        parts.append(p)
        rem = rem - p
    return parts


def _cumsum_kernel(lf_ref, dc_ref, kext_ref, carry_ref):
    tm = lf_ref.shape[1]

    @pl.when(pl.program_id(1) == 0)
    def _():
        carry_ref[...] = jnp.zeros_like(carry_ref)

    lf = lf_ref[0]
    tri = (lax.broadcasted_iota(I32, (tm, tm), 1) <= lax.broadcasted_iota(I32, (tm, tm), 0)).astype(F32)
    dc = jnp.dot(tri, lf, preferred_element_type=F32, precision=lax.Precision.HIGHEST) + carry_ref[...]
    dc_ref[0] = dc
    carry_ref[...] = dc[tm - 1:tm, :]
    eye = (lax.broadcasted_iota(I32, (N_HEADS_C, N_HEADS_C), 0)
           == lax.broadcasted_iota(I32, (N_HEADS_C, N_HEADS_C), 1)).astype(F32)
    dct = lax.dot_general(eye, dc, (((1,), (1,)), ((), ())), preferred_element_type=F32,
                          precision=lax.Precision.HIGHEST)
    neg = _split_bf16(-dct)
    row = lax.broadcasted_iota(I32, (FOX_EXT_ROWS, tm), 0)
    for h in range(N_HEADS_C):
        tile = jnp.where(row < FOX_SPLIT, 1.0, 0.0)
        for i in range(FOX_SPLIT):
            tile = jnp.where(row == FOX_SPLIT + i, neg[i][h:h + 1, :], tile)
        kext_ref[0, 0, h] = tile.astype(BF16)


def _forget_cumsum(logf_all, kc):
    bn, lp, _ = logf_all.shape
    return pl.pallas_call(
        _cumsum_kernel,
        grid=(bn, lp // kc),
        in_specs=[pl.BlockSpec((1, kc, N_HEADS_C), lambda b, t: (b, t, 0))],
        out_specs=[pl.BlockSpec((1, kc, N_HEADS_C), lambda b, t: (b, t, 0)),
                   pl.BlockSpec((1, 1, N_HEADS_C, FOX_EXT_ROWS, kc), lambda b, t: (b, t, 0, 0, 0))],
        out_shape=[jax.ShapeDtypeStruct((bn, lp, N_HEADS_C), F32),
                   jax.ShapeDtypeStruct((bn, lp // kc, N_HEADS_C, FOX_EXT_ROWS, kc), BF16)],
        scratch_shapes=[pltpu.VMEM((1, N_HEADS_C), F32)],
        compiler_params=_cparams(("arbitrary", "arbitrary")),
        name="forget_cumsum",
    )(logf_all)


def _fox_kernel(q_ref, dq_ref, kt_ref, kext_ref, vx_ref, o_ref, qx_ref, m_ref, acc_ref, *, tq, kc, q_off):
    qb = pl.program_id(1)
    q_base = q_off + qb * tq
    n_c = (q_base + tq + kc - 1) // kc
    n_full = (q_base + 1) // kc
    lane = lax.broadcasted_iota(I32, (tq, LANES), 1)

    dq = dq_ref[0]
    for h in range(N_HEADS_C):
        pair = q_ref[:, (h // 2) * LANES:(h // 2 + 1) * LANES] * (HEAD_DIM_C ** -0.5)
        if h % 2:
            pair = pltpu.roll(pair, HEAD_DIM_C, axis=1)
        ext = jnp.where((lane >= HEAD_DIM_C + FOX_SPLIT) & (lane < HEAD_DIM_C + 2 * FOX_SPLIT), 1.0, 0.0)
        for i, part in enumerate(_split_bf16(dq[:, h:h + 1])):
            ext = jnp.where(lane == HEAD_DIM_C + i, part, ext)
        qx_ref[h] = jnp.where(lane < HEAD_DIM_C, pair, ext).astype(BF16)
    m_ref[...] = jnp.full(m_ref.shape, -jnp.inf, F32)
    acc_ref[...] = jnp.zeros(acc_ref.shape, F32)

    q_pos = q_base + lax.broadcasted_iota(I32, (tq, kc), 0)
    k_loc = lax.broadcasted_iota(I32, (tq, kc), 1)
    zrows = jnp.zeros((LANES - HEAD_DIM_C - FOX_EXT_ROWS, kc), BF16)

    def chunk(c, masked):
        r0 = pl.multiple_of(c * kc, kc)
        for h in range(N_HEADS_C):
            kx = jnp.concatenate([kt_ref[0, c, h * HEAD_DIM_C:(h + 1) * HEAD_DIM_C, :], kext_ref[0, c, h], zrows],
                                 axis=0)
            s = _mm(qx_ref[h], kx)
            if masked:
                s = jnp.where(r0 + k_loc <= q_pos, s, -jnp.inf)
            m_old = m_ref[h]
            m_new = jnp.maximum(m_old, jnp.broadcast_to(jnp.max(s, axis=1, keepdims=True), (tq, LANES)))
            p = jnp.exp(s - jnp.concatenate([m_new] * (kc // LANES), axis=1))
            pv = _mm(p.astype(BF16), vx_ref[0, pl.ds(r0, kc), h * LANES:(h + 1) * LANES])
            acc_ref[h] = jnp.exp(m_old - m_new) * acc_ref[h] + pv
            m_ref[h] = m_new

    def full_body(c, carry):
        chunk(c, False)
        return carry

    def diag_body(c, carry):
        chunk(c, True)
        return carry

    lax.fori_loop(0, n_full, full_body, 0)
    lax.fori_loop(n_full, n_c, diag_body, 0)

    for j in range(N_HEADS_C // 2):
        even = acc_ref[2 * j] / acc_ref[2 * j][:, HEAD_DIM_C:HEAD_DIM_C + 1]
        odd = acc_ref[2 * j + 1] / acc_ref[2 * j + 1][:, HEAD_DIM_C:HEAD_DIM_C + 1]
        o_ref[:, j * LANES:(j + 1) * LANES] = jnp.where(lane < HEAD_DIM_C, even, pltpu.roll(odd, HEAD_DIM_C, axis=1))


def _fox(cq, dq, kt, kext, vx, *, tq, q_off):
    bn, tq_total, width = cq.shape
    _, n_chunks, _, kc = kt.shape
    lp = n_chunks * kc
    return pl.pallas_call(
        functools.partial(_fox_kernel, tq=tq, kc=kc, q_off=q_off),
        grid=(bn, tq_total // tq),
        in_specs=[pl.BlockSpec((None, tq, width), lambda b, q: (b, q, 0)),
                  pl.BlockSpec((1, tq, N_HEADS_C), lambda b, q: (b, q, 0)),
                  pl.BlockSpec((1, n_chunks, width, kc), lambda b, q: (b, 0, 0, 0)),
                  pl.BlockSpec((1, n_chunks, N_HEADS_C, FOX_EXT_ROWS, kc), lambda b, q: (b, 0, 0, 0, 0)),
                  pl.BlockSpec((1, lp, N_HEADS_C * LANES), lambda b, q: (b, 0, 0))],
        out_specs=pl.BlockSpec((None, tq, width), lambda b, q: (b, q, 0)),
        out_shape=jax.ShapeDtypeStruct((bn, tq_total, width), F32),
        scratch_shapes=[pltpu.VMEM((N_HEADS_C, tq, LANES), BF16),
                        pltpu.VMEM((N_HEADS_C, tq, LANES), F32),
                        pltpu.VMEM((N_HEADS_C, tq, LANES), F32)],
        compiler_params=_cparams(("arbitrary", "arbitrary")),
        name="fox",
    )(cq, dq, kt, kext, vx)


def _merge_kernel(x_ref, ya_ref, yb_ref, yc_ref, g_ref, wa_ref, wb_ref, wc_ref, wo_ref, lng_ref, lnb_ref,
                  o_ref, *, alpha):
    ba = _mm(ya_ref[...].astype(BF16), wa_ref[...])
    bc = _mm(yc_ref[...].astype(BF16), wc_ref[...])
    glu = _mm(jax.nn.gelu(yb_ref[...]).astype(BF16), wb_ref[...])
    bb = glu[:, :D_MODEL] * jax.nn.sigmoid(glu[:, D_MODEL:])
    g = jax.nn.sigmoid(g_ref[...])
    merged = g[:, :D_MODEL] * ba + g[:, D_MODEL:2 * D_MODEL] * bb + g[:, 2 * D_MODEL:] * bc
    out = _mm(merged.astype(BF16), wo_ref[...])
    o_ref[...] = _layer_norm(alpha * x_ref[...] + out, lng_ref[...], lnb_ref[...])


def _merge(x2d, ya, yb, yb_time_major, yc, gates, wa, wb, wc, wo, ln_g, ln_b, alpha, n_time):
    m = x2d.shape[0]
    tm = min(OUT_TM, m)
    row = lambda w: pl.BlockSpec((tm, w), lambda i: (i, 0))
    const = lambda *s: pl.BlockSpec(s, lambda i: (0,) * len(s))
    if yb_time_major:
        n_t = n_time // tm
        yb_spec = pl.BlockSpec((tm, SSM_WIDTH), lambda i: (i % n_t, i // n_t))
    else:
        yb_spec = row(SSM_WIDTH)
    return pl.pallas_call(
        functools.partial(_merge_kernel, alpha=alpha),
        grid=(m // tm,),
        in_specs=[row(D_MODEL), row(512), yb_spec, row(512), row(N_BRANCH * D_MODEL),
                  const(512, D_MODEL), const(SSM_WIDTH, 2 * D_MODEL), const(512, D_MODEL),
                  const(D_MODEL, D_MODEL), const(1, D_MODEL), const(1, D_MODEL)],
        out_specs=row(D_MODEL),
        out_shape=jax.ShapeDtypeStruct((m, D_MODEL), F32),
        compiler_params=_cparams(("arbitrary",)),
        name="merge",
    )(x2d, ya, yb, yc, gates, wa, wb, wc, wo, ln_g.reshape(1, -1), ln_b.reshape(1, -1))


def _ffn_kernel(x_ref, wg_ref, wu_ref, wd_ref, lng_ref, lnb_ref, o_ref, acc_ref, *, alpha):
    c = pl.program_id(1)

    @pl.when(c == 0)
    def _():
        acc_ref[...] = jnp.zeros_like(acc_ref)

    xb = x_ref[...].astype(BF16)
    a = jax.nn.silu(_mm(xb, wg_ref[...])) * _mm(xb, wu_ref[...])
    acc_ref[...] += _mm(a.astype(BF16), wd_ref[...])

    @pl.when(c == pl.num_programs(1) - 1)
    def _():
        o_ref[...] = _layer_norm(alpha * x_ref[...] + acc_ref[...], lng_ref[...], lnb_ref[...])


def _ffn(x2d, w_gu, w_down, ln_g, ln_b, alpha):
    m = x2d.shape[0]
    tm = min(FFN_TM, m)
    d_ff = w_down.shape[0]
    n_c = d_ff // FFN_FC
    return pl.pallas_call(
        functools.partial(_ffn_kernel, alpha=alpha),
        grid=(m // tm, n_c),
        in_specs=[pl.BlockSpec((tm, D_MODEL), lambda i, c: (i, 0)),
                  pl.BlockSpec((D_MODEL, FFN_FC), lambda i, c: (0, c)),
                  pl.BlockSpec((D_MODEL, FFN_FC), lambda i, c: (0, n_c + c)),
                  pl.BlockSpec((FFN_FC, D_MODEL), lambda i, c: (c, 0)),
                  pl.BlockSpec((1, D_MODEL), lambda i, c: (0, 0)),
                  pl.BlockSpec((1, D_MODEL), lambda i, c: (0, 0))],
        out_specs=pl.BlockSpec((tm, D_MODEL), lambda i, c: (i, 0)),
        out_shape=jax.ShapeDtypeStruct((m, D_MODEL), F32),
        scratch_shapes=[pltpu.VMEM((tm, D_MODEL), F32)],
        compiler_params=_cparams(("arbitrary", "arbitrary")),
        name="ffn",
    )(x2d, w_gu, w_gu, w_down, ln_g.reshape(1, -1), ln_b.reshape(1, -1))


def _moe_kernel(x_ref, wr_ref, br_ref, wgu_ref, wd_ref, lng_ref, lnb_ref, o_ref, acc_ref, gate_ref, *, alpha):
    e = pl.program_id(1)
    tm = x_ref.shape[0]
    xb = x_ref[...].astype(BF16)
    lane = lax.broadcasted_iota(I32, (tm, N_EXPERTS), 1).astype(F32)

    @pl.when(e == 0)
    def _():
        acc_ref[...] = jnp.zeros_like(acc_ref)
        logits = _mm(xb, wr_ref[...]) + br_ref[...]
        m1 = jnp.max(logits, axis=1, keepdims=True)
        i1 = jnp.min(jnp.where(logits == m1, lane, float(N_EXPERTS)), axis=1, keepdims=True)
        rest = jnp.where(lane == i1, -jnp.inf, logits)
        m2 = jnp.max(rest, axis=1, keepdims=True)
        i2 = jnp.min(jnp.where(rest == m2, lane, float(N_EXPERTS)), axis=1, keepdims=True)
        ex = jnp.exp(m2 - m1)
        gate_ref[...] = jnp.where(lane == i1, 1.0 / (1.0 + ex), 0.0) + jnp.where(lane == i2, ex / (1.0 + ex), 0.0)

    ge = jnp.sum(jnp.where(lane == e.astype(F32), gate_ref[...], 0.0), axis=1, keepdims=True)
    h = _mm(xb, wgu_ref[0])
    a = jax.nn.silu(h[:, :D_FF_EXPERT]) * h[:, D_FF_EXPERT:]
    acc_ref[...] += ge * _mm(a.astype(BF16), wd_ref[0])

    @pl.when(e == pl.num_programs(1) - 1)
    def _():
        o_ref[...] = _layer_norm(alpha * x_ref[...] + acc_ref[...], lng_ref[...], lnb_ref[...])


def _moe(x2d, w_router, b_router, w_exp_gu, w_exp_down, ln_g, ln_b, alpha):
    m = x2d.shape[0]
    tm = min(MOE_TM, m)
    return pl.pallas_call(
        functools.partial(_moe_kernel, alpha=alpha),
        grid=(m // tm, N_EXPERTS),
        in_specs=[pl.BlockSpec((tm, D_MODEL), lambda i, e: (i, 0)),
                  pl.BlockSpec((D_MODEL, N_EXPERTS), lambda i, e: (0, 0)),
                  pl.BlockSpec((1, N_EXPERTS), lambda i, e: (0, 0)),
                  pl.BlockSpec((1, D_MODEL, 2 * D_FF_EXPERT), lambda i, e: (e, 0, 0)),
                  pl.BlockSpec((1, D_FF_EXPERT, D_MODEL), lambda i, e: (e, 0, 0)),
                  pl.BlockSpec((1, D_MODEL), lambda i, e: (0, 0)),
                  pl.BlockSpec((1, D_MODEL), lambda i, e: (0, 0))],
        out_specs=pl.BlockSpec((tm, D_MODEL), lambda i, e: (i, 0)),
        out_shape=jax.ShapeDtypeStruct((m, D_MODEL), F32),
        scratch_shapes=[pltpu.VMEM((tm, D_MODEL), F32), pltpu.VMEM((tm, N_EXPERTS), F32)],
        compiler_params=_cparams(("arbitrary", "arbitrary")),
        name="moe",
    )(x2d, w_router, b_router.reshape(1, -1), w_exp_gu, w_exp_down, ln_g.reshape(1, -1), ln_b.reshape(1, -1))


def _pad_rows(a, rows):
    if a.shape[1] == rows:
        return a
    pad = jnp.zeros((a.shape[0], rows - a.shape[1]) + a.shape[2:], a.dtype)
    return jnp.concatenate([a, pad], axis=1)


def _round_up(n, mult):
    return -(-n // mult) * mult


def _token_mixer(x, past, lw):
    bn, n_time, _ = x.shape
    m = bn * n_time
    x2d = x.reshape(m, D_MODEL)
    prompt = past is None
    per_batch = lambda a: a.reshape((bn, -1) + a.shape[1:])

    if prompt:
        past_len = 0
        proj = _project_prompt(x2d, lw["w_in"], lw["b_forget"], bn, n_time, lw["prev_ck"], lw["prev_cv"])
        a_k, a_v, a_ki, logf = [per_batch(proj[n]) for n in ("a_k", "a_v", "a_ki", "logf")]
        c_k, c_v = proj["ck"], proj["cv"]
        logf_all = logf
        h0_re = jnp.zeros((bn, SSM_STATES), F32)
        h0_im = jnp.zeros((bn, SSM_STATES), F32)
        u_tm = proj["u"].reshape(n_time, bn, SSM_WIDTH)
        qa, qit, wt = per_batch(proj["qa"]), proj["qit"], proj["wt"]
        kx, vxa, ki16, vxc = [per_batch(proj[n]) for n in ("kx", "vxa", "ki16", "vxc")]
        kt = proj["kt"].reshape(bn, n_time // FOX_KC, 512, FOX_KC)
    else:
        p_ak, p_av, p_aki, p_hr, p_hi, p_ck, p_cv, p_logf = past
        past_len = p_ak.shape[1]
        proj = _project(x2d, lw["w_in"])
        kv, misc = per_batch(proj["kv"]), per_batch(proj["misc"])
        a_k, a_v = kv[..., :HEAD_DIM_A], kv[..., HEAD_DIM_A:]
        a_ki = misc[..., :IDX_DIM]
        a_w = misc[..., MISC_W_LANE:MISC_W_LANE + N_IDX_HEADS]
        c_f = misc[..., MISC_F_LANE:MISC_F_LANE + N_HEADS_C]
        c_k, c_v = per_batch(proj["ck"]), per_batch(proj["cv"])
        logf = _log_forget(c_f.reshape(m, N_HEADS_C), lw["b_forget"]).reshape(bn, n_time, N_HEADS_C)
        logf_all = jnp.concatenate([p_logf, logf], axis=1)
        h0_re = p_hr.reshape(bn, SSM_STATES)
        h0_im = p_hi.reshape(bn, SSM_STATES)
        u_tm = jnp.swapaxes(per_batch(proj["u"]), 0, 1)
        tq_pad = _round_up(n_time, DSA_TQ)
        qa = _pad_rows(per_batch(proj["qa"]), tq_pad)
        qit = jnp.swapaxes(_pad_rows(per_batch(proj["qi"]), tq_pad), 1, 2)
        wt = jnp.swapaxes(_pad_rows(a_w, tq_pad), 1, 2)
        kx, vxa, ki16, kt, vxc = _cache_layouts(
            p_ak, p_av, p_aki, p_ck.reshape(bn, past_len, -1), p_cv.reshape(bn, past_len, -1),
            _pad_rows(kv, FOX_KC), _pad_rows(misc, FOX_KC), _pad_rows(c_k, FOX_KC), _pad_rows(c_v, FOX_KC))
    kv_len = past_len + n_time
    n_top = min(TOPK_MAX, kv_len // 4)

    y_b, h_re, h_im = _s5(u_tm, h0_re, h0_im, lw["ab_re"], lw["ab_im"], lw["bd"], lw["cr"], lw["ci"], lw["ssm_d"])
    if prompt:
        y_b = y_b.reshape(n_time, bn * SSM_WIDTH)
    else:
        y_b = jnp.swapaxes(y_b, 0, 1).reshape(m, SSM_WIDTH)

    y_a = _dsa(qa, qit, wt, kx, vxa, ki16, lw["bias_near"], lw["bias_lhs"],
               q_off=past_len, kv_len=kv_len, n_top=n_top)[:, :n_time].reshape(m, -1)

    lpc = _round_up(kv_len, FOX_KC)
    dcum, kext = _forget_cumsum(_pad_rows(logf_all, lpc), FOX_KC)
    y_c = _fox(per_batch(proj["cq"]), dcum[:, past_len:kv_len], kt, kext, vxc,
               tq=min(FOX_TQ_PROMPT, n_time), q_off=past_len).reshape(m, -1)

    x1 = _merge(x2d, y_a, y_b, prompt, y_c, proj["gates"], lw["w_a_out"], lw["w_b_glu"], lw["w_c_out"], lw["w_o"],
                lw["ln1_g"], lw["ln1_b"], lw["alpha"], n_time)
    if not prompt:
        c_k = c_k.reshape(bn, n_time, N_HEADS_C, HEAD_DIM_C)
        c_v = c_v.reshape(bn, n_time, N_HEADS_C, HEAD_DIM_C)
    new_state = (a_k, a_v, a_ki, h_re.reshape(bn, N_GROUPS, STATE_DIM), h_im.reshape(bn, N_GROUPS, STATE_DIM),
                 c_k, c_v, logf)
    return x1, new_state


def kernel(x_prompt, x_sample, cache_a_k, cache_a_v, cache_a_kidx, state_ssm_re, state_ssm_im, cache_c_k, cache_c_v, cache_c_logf, w_in, b_forget, ssm_lam_re, ssm_lam_im, ssm_log_step, ssm_b_re, ssm_b_im, ssm_c_re, ssm_c_im, ssm_d, w_a_out, w_b_glu, w_c_out, w_o, ln1_g, ln1_b, ln2_g, ln2_b, t5_table, w_ffn_gu, w_ffn_down, w_router, b_router, w_exp_gu, w_exp_down):
    depth = w_in.shape[0]
    alpha = float((2 * depth) ** 0.25)
    bias_near, bias_lhs = _dsa_bias_consts(t5_table)
    xp, xs = x_prompt, x_sample
    rows_p = [[] for _ in range(8)]
    rows_s = [[] for _ in range(8)]
    prev_ck = prev_cv = None
    for layer in range(depth):
        ab_re, ab_im, bb_re, bb_im = _s5_discretize(ssm_lam_re[layer], ssm_lam_im[layer], ssm_log_step[layer],
                                                    ssm_b_re[layer], ssm_b_im[layer])
        lw = dict(
            w_in=_pack_w_in(w_in[layer]), b_forget=b_forget[layer],
            ab_re=ab_re.reshape(1, SSM_STATES), ab_im=ab_im.reshape(1, SSM_STATES),
            bd=jnp.concatenate([_block_diag(bb_re), _block_diag(bb_im)], axis=1).astype(BF16),
            cr=_block_diag(jnp.swapaxes(ssm_c_re[layer], 1, 2)).astype(BF16),
            ci=_block_diag(jnp.swapaxes(ssm_c_im[layer], 1, 2)).astype(BF16),
            ssm_d=ssm_d[layer].reshape(1, SSM_WIDTH),
            w_a_out=w_a_out[layer].astype(BF16), w_b_glu=w_b_glu[layer].astype(BF16),
            w_c_out=w_c_out[layer].astype(BF16), w_o=w_o[layer].astype(BF16),
            ln1_g=ln1_g[layer], ln1_b=ln1_b[layer], bias_near=bias_near, bias_lhs=bias_lhs, alpha=alpha)
        past = (cache_a_k[layer], cache_a_v[layer], cache_a_kidx[layer], state_ssm_re[layer],
                state_ssm_im[layer], cache_c_k[layer], cache_c_v[layer], cache_c_logf[layer])
        xp1, st_p = _token_mixer(xp, None, dict(lw, prev_ck=prev_ck, prev_cv=prev_cv))
        prev_ck, prev_cv = st_p[5], st_p[6]
        xs1, st_s = _token_mixer(xs, past, lw)
        i = layer // 2
        if layer % 2 == 0:
            wgu, wdn = w_ffn_gu[i].astype(BF16), w_ffn_down[i].astype(BF16)
            xp2 = _ffn(xp1, wgu, wdn, ln2_g[layer], ln2_b[layer], alpha)
            xs2 = _ffn(xs1, wgu, wdn, ln2_g[layer], ln2_b[layer], alpha)
        else:
            wr, wgu, wdn = w_router[i].astype(BF16), w_exp_gu[i].astype(BF16), w_exp_down[i].astype(BF16)
            xp2 = _moe(xp1, wr, b_router[i], wgu, wdn, ln2_g[layer], ln2_b[layer], alpha)
            xs2 = _moe(xs1, wr, b_router[i], wgu, wdn, ln2_g[layer], ln2_b[layer], alpha)
        xp = xp2.reshape(x_prompt.shape)
        xs = xs2.reshape(x_sample.shape)
        for j in range(8):
            rows_p[j].append(st_p[j])
            rows_s[j].append(st_s[j])
    bshape = (depth,) + x_prompt.shape[:2] + (N_HEADS_C, HEAD_DIM_C)
    rows_p[5], rows_p[6] = None, None
    (a_k_p, a_v_p, a_kidx_p, ssm_re_p, ssm_im_p, _, _, c_logf_p) = [None if r is None else jnp.stack(r) for r in rows_p]
    c_k_p, c_v_p = prev_ck.reshape(bshape), prev_cv.reshape(bshape)
    (a_k_s, a_v_s, a_kidx_s, ssm_re_s, ssm_im_s, c_k_s, c_v_s, c_logf_s) = [jnp.stack(r) for r in rows_s]
    return (xp, xs, a_k_p, a_k_s, a_v_p, a_v_s, a_kidx_p, a_kidx_s, ssm_re_p, ssm_re_s,
            ssm_im_p, ssm_im_s, c_k_p, c_k_s, c_v_p, c_v_s, c_logf_p, c_logf_s)
```

```python
import functools
import math

import jax
import jax.numpy as jnp
import numpy as np
from jax import lax
from jax.experimental import pallas as pl
from jax.experimental.pallas import tpu as pltpu

F32 = jnp.float32
BF16 = jnp.bfloat16
I32 = jnp.int32

D_MODEL = 1024
CHUNK = 64
N_HEADS_A = 8
HEAD_DIM_A = 64
N_IDX_HEADS = 8
IDX_DIM = 32
TOPK_MAX = 256
NUM_BUCKETS = 32
MAX_DISTANCE = 128
SSM_WIDTH = 512
GROUP_SIZE = 16
N_GROUPS = SSM_WIDTH // GROUP_SIZE
STATE_DIM = 64
SSM_STATES = N_GROUPS * STATE_DIM
N_HEADS_C = 8
HEAD_DIM_C = 64
N_BRANCH = 3
D_FF = 2816
N_EXPERTS = 8
D_FF_EXPERT = 1408
LN_EPS = 1e-5
PROJ_SIZES = (512, 64, 64, 256, 32, 8, 512, 512, 512, 512, 8, 3072)

LANES = 128
SUBLANES = 8
VMEM_LIMIT_BYTES = 56 * 1024 * 1024

PROJ_TM = 256
S5_ROWS = 512
S5_CW = 256
S5_SLAB_GROUPS = 8
DSA_TQ = 128
DSA_KC = 256
FOX_TQ_PROMPT = 256
FOX_KC = 256
OUT_TM = 256
FFN_TM = 1024
FFN_FC = 256
MOE_TM = 1024

INT_MIN = -(2 ** 31)
SKEY_NEG_INF = -2139095041


def _cparams(sem):
    return pltpu.CompilerParams(dimension_semantics=sem, vmem_limit_bytes=VMEM_LIMIT_BYTES)


def _mm(a, b):
    return jnp.dot(a, b, preferred_element_type=F32)


def _mm_nt(a, b):
    return lax.dot_general(a, b, (((1,), (1,)), ((), ())), preferred_element_type=F32)


def _layer_norm(z, g, b):
    mu = jnp.mean(z, axis=-1, keepdims=True)
    zc = z - mu
    var = jnp.mean(zc * zc, axis=-1, keepdims=True)
    return zc * lax.rsqrt(var + LN_EPS) * g + b


def _pack_w_in(w):
    offs = np.cumsum((0,) + PROJ_SIZES)
    a_q, a_k, a_v, a_qi, a_ki, a_w, b_u, c_q, c_k, c_v, c_f, gates = [
        w[:, offs[i]:offs[i + 1]] for i in range(len(PROJ_SIZES))]
    pad = jnp.zeros((w.shape[0], LANES - IDX_DIM - N_IDX_HEADS - N_HEADS_C), w.dtype)
    return jnp.concatenate([a_q, a_k, a_v, a_qi, a_ki, a_w, c_f, pad, b_u, c_q, c_k, c_v, gates],
                           axis=1).astype(BF16)


PROJ_OUT = (("qa", 0, 512), ("kv", 512, 128), ("qi", 640, 256), ("misc", 896, 128), ("u", 1024, 512),
            ("cq", 1536, 512), ("ck", 2048, 512), ("cv", 2560, 512), ("gates", 3072, 3072))
PROJ_COLS_PACKED = 6144


def _proj_kernel(x_ref, w_ref, *out_refs):
    xb = x_ref[...].astype(BF16)
    for (_, lo, width), o_ref in zip(PROJ_OUT, out_refs):
        for c in range(0, width, 512):
            cw = min(512, width - c)
            o_ref[:, c:c + cw] = _mm(xb, w_ref[:, lo + c:lo + c + cw])


def _project(x2d, w_packed):
    m = x2d.shape[0]
    tm = min(PROJ_TM, m)
    outs = pl.pallas_call(
        _proj_kernel,
        grid=(m // tm,),
        in_specs=[pl.BlockSpec((tm, D_MODEL), lambda i: (i, 0)),
                  pl.BlockSpec((D_MODEL, PROJ_COLS_PACKED), lambda i: (0, 0))],
        out_specs=[pl.BlockSpec((tm, width), lambda i: (i, 0)) for _, _, width in PROJ_OUT],
        out_shape=[jax.ShapeDtypeStruct((m, width), F32) for _, _, width in PROJ_OUT],
        compiler_params=_cparams(("arbitrary",)),
        name="proj",
    )(x2d, w_packed)
    return dict(zip([p[0] for p in PROJ_OUT], outs))


MISC_W_LANE = IDX_DIM
MISC_F_LANE = IDX_DIM + N_IDX_HEADS
DSA_K_ONES = 3


def _attn_layouts(kv, misc, ck, cv, kx_ref, vxa_ref, ki16_ref, kt_ref, vxc_ref):
    rows = kv.shape[0]
    lane = lax.broadcasted_iota(I32, (rows, LANES), 1)
    ones_k = jnp.where((lane >= HEAD_DIM_A) & (lane < HEAD_DIM_A + DSA_K_ONES), 1.0, 0.0)
    ones_v = jnp.where(lane == HEAD_DIM_A, 1.0, 0.0)
    kx_ref[...] = jnp.where(lane < HEAD_DIM_A, kv, ones_k).astype(BF16)
    vxa_ref[...] = jnp.where(lane < HEAD_DIM_A, pltpu.roll(kv, HEAD_DIM_A, axis=1), ones_v).astype(BF16)
    ki16_ref[...] = misc[:, :IDX_DIM].astype(BF16)
    kt_ref[...] = ck.T.astype(BF16)
    for h in range(N_HEADS_C):
        pair = cv[:, (h // 2) * LANES:(h // 2 + 1) * LANES]
        if h % 2:
            pair = pltpu.roll(pair, HEAD_DIM_C, axis=1)
        vxc_ref[:, h * LANES:(h + 1) * LANES] = jnp.where(lane < HEAD_DIM_C, pair, ones_v).astype(BF16)


def _proj_prompt_kernel(x_ref, w_ref, bf_ref, *refs, n_prev):
    if n_prev:
        pck_ref, pcv_ref = refs[:2]
        refs = refs[2:]
    (qa_ref, qit_ref, wt_ref, ak_ref, av_ref, aki_ref, logf_ref, u_ref, cq_ref, ck_ref, cv_ref, g_ref,
     kx_ref, vxa_ref, ki16_ref, kt_ref, vxc_ref) = refs
    xb = x_ref[...].astype(BF16)
    cols = {name: (lo, width) for name, lo, width in PROJ_OUT}

    def mm(name, c0=0, cw=None):
        lo, width = cols[name]
        cw = width if cw is None else cw
        return _mm(xb, w_ref[:, lo + c0:lo + c0 + cw])

    qa_ref[...] = mm("qa")
    u_ref[...] = mm("u")
    for c in range(0, cols["gates"][1], 512):
        g_ref[:, c:c + 512] = mm("gates", c, 512)
    small = mm("kv", 0, cols["kv"][1] + cols["qi"][1] + cols["misc"][1])
    kv = small[:, :LANES]
    qi = small[:, LANES:LANES + cols["qi"][1]]
    misc = small[:, LANES + cols["qi"][1]:]
    cqk = mm("cq", 0, cols["cq"][1] + cols["ck"][1])
    cq_ref[...] = cqk[:, :cols["cq"][1]]
    ck = cqk[:, cols["cq"][1]:]
    cv = mm("cv")
    if n_prev:
        ck_ref[:n_prev] = pck_ref[...]
        cv_ref[:n_prev] = pcv_ref[...]
    for h in range(N_HEADS_C):
        pk = ck[:, (h // 2) * LANES:(h // 2 + 1) * LANES]
        pv = cv[:, (h // 2) * LANES:(h // 2 + 1) * LANES]
        if h % 2:
            pk = pltpu.roll(pk, HEAD_DIM_C, axis=1)
            pv = pltpu.roll(pv, HEAD_DIM_C, axis=1)
        ck_ref[n_prev, :, h, :] = pk[:, :HEAD_DIM_C]
        cv_ref[n_prev, :, h, :] = pv[:, :HEAD_DIM_C]
    qit_ref[...] = qi.T
    wt_ref[...] = misc.T[MISC_W_LANE:MISC_W_LANE + N_IDX_HEADS, :]
    ak_ref[...] = kv[:, :HEAD_DIM_A]
    av_ref[...] = pltpu.roll(kv, HEAD_DIM_A, axis=1)[:, :HEAD_DIM_A]
    aki_ref[...] = misc[:, :IDX_DIM]
    lf = jax.nn.log_sigmoid(misc + bf_ref[...])
    logf_ref[...] = pltpu.roll(lf, LANES - MISC_F_LANE, axis=1)[:, :N_HEADS_C]
    _attn_layouts(kv, misc, ck, cv, kx_ref, vxa_ref, ki16_ref, kt_ref.at[0], vxc_ref)


def _project_prompt(x2d, w_packed, b_forget, n_batch, n_time, prev_ck, prev_cv):
    m = x2d.shape[0]
    n_prev = 0 if prev_ck is None else prev_ck.shape[0]
    tm = FOX_KC
    n_t = n_time // tm
    bf = jnp.zeros((1, LANES), F32).at[0, MISC_F_LANE:MISC_F_LANE + N_HEADS_C].set(b_forget)
    row = lambda w, dt=F32: (jax.ShapeDtypeStruct((m, w), dt), pl.BlockSpec((tm, w), lambda i: (i, 0)))
    col = lambda r: (jax.ShapeDtypeStruct((r, m), F32), pl.BlockSpec((r, tm), lambda i: (0, i)))
    head_shape = (N_HEADS_C, HEAD_DIM_C)
    stacked = (jax.ShapeDtypeStruct((n_prev + 1, m) + head_shape, F32),
               pl.BlockSpec((n_prev + 1, tm) + head_shape, lambda i: (0, i, 0, 0)))
    prev_specs = [pl.BlockSpec((n_prev, tm) + head_shape, lambda i: (0, i, 0, 0))] * 2 if n_prev else []
    prev_args = [prev_ck, prev_cv] if n_prev else []
    outs = dict(
        qa=row(512), qit=col(N_IDX_HEADS * IDX_DIM), wt=col(N_IDX_HEADS),
        a_k=row(HEAD_DIM_A), a_v=row(HEAD_DIM_A), a_ki=row(IDX_DIM), logf=row(N_HEADS_C),
        u=(jax.ShapeDtypeStruct((n_time, n_batch * SSM_WIDTH), F32),
           pl.BlockSpec((tm, SSM_WIDTH), lambda i: (i % n_t, i // n_t))),
        cq=row(512), ck=stacked, cv=stacked, gates=row(N_BRANCH * D_MODEL),
        kx=row(LANES, BF16), vxa=row(LANES, BF16), ki16=row(IDX_DIM, BF16),
        kt=(jax.ShapeDtypeStruct((m // tm, 512, tm), BF16), pl.BlockSpec((1, 512, tm), lambda i: (i, 0, 0))),
        vxc=row(N_HEADS_C * LANES, BF16))
    res = pl.pallas_call(
        functools.partial(_proj_prompt_kernel, n_prev=n_prev),
        grid=(m // tm,),
        in_specs=[pl.BlockSpec((tm, D_MODEL), lambda i: (i, 0)),
                  pl.BlockSpec((D_MODEL, PROJ_COLS_PACKED), lambda i: (0, 0)),
                  pl.BlockSpec((1, LANES), lambda i: (0, 0))] + prev_specs,
        out_specs=[v[1] for v in outs.values()],
        out_shape=[v[0] for v in outs.values()],
        compiler_params=_cparams(("arbitrary",)),
        name="proj_prompt",
    )(x2d, w_packed, bf, *prev_args)
    return dict(zip(outs.keys(), res))


def _cache_layout_kernel(pk_ref, pv_ref, pki_ref, pck_ref, pcv_ref, nkv_ref, nmisc_ref, nck_ref, ncv_ref,
                         kx_ref, vxa_ref, ki16_ref, kt_ref, vxc_ref, *, n_past):
    is_new = pl.program_id(1) >= n_past
    rows = nkv_ref.shape[1]
    lane = lax.broadcasted_iota(I32, (rows, LANES), 1)
    zeros_k = jnp.zeros((rows, LANES - HEAD_DIM_A), F32)
    past_kv = jnp.where(lane < HEAD_DIM_A, jnp.concatenate([pk_ref[0], zeros_k], axis=1),
                        pltpu.roll(jnp.concatenate([pv_ref[0], zeros_k], axis=1), HEAD_DIM_A, axis=1))
    past_misc = jnp.concatenate([pki_ref[0], jnp.zeros((rows, LANES - IDX_DIM), F32)], axis=1)
    kv = jnp.where(is_new, nkv_ref[0], past_kv)
    misc = jnp.where(is_new, nmisc_ref[0], past_misc)
    ck = jnp.where(is_new, nck_ref[0], pck_ref[0])
    cv = jnp.where(is_new, ncv_ref[0], pcv_ref[0])
    _attn_layouts(kv, misc, ck, cv, kx_ref.at[0], vxa_ref.at[0], ki16_ref.at[0], kt_ref.at[0, 0], vxc_ref.at[0])


def _cache_layouts(p_k, p_v, p_ki, p_ck, p_cv, n_kv, n_misc, n_ck, n_cv):
    bn, n_rows, _ = p_k.shape
    kc = FOX_KC
    n_past = n_rows // kc
    n_chunks = n_past + 1
    lp = n_chunks * kc
    past = lambda w: pl.BlockSpec((1, kc, w), lambda b, c: (b, jnp.minimum(c, n_past - 1), 0))
    new = lambda w: pl.BlockSpec((1, kc, w), lambda b, c: (b, 0, 0))
    out = lambda w: pl.BlockSpec((1, kc, w), lambda b, c: (b, c, 0))
    return pl.pallas_call(
        functools.partial(_cache_layout_kernel, n_past=n_past),
        grid=(bn, n_chunks),
        in_specs=[past(HEAD_DIM_A), past(HEAD_DIM_A), past(IDX_DIM), past(512), past(512),
                  new(LANES), new(LANES), new(512), new(512)],
        out_specs=[out(LANES), out(LANES), out(IDX_DIM),
                   pl.BlockSpec((1, 1, 512, kc), lambda b, c: (b, c, 0, 0)), out(N_HEADS_C * LANES)],
        out_shape=[jax.ShapeDtypeStruct((bn, lp, LANES), BF16), jax.ShapeDtypeStruct((bn, lp, LANES), BF16),
                   jax.ShapeDtypeStruct((bn, lp, IDX_DIM), BF16),
                   jax.ShapeDtypeStruct((bn, n_chunks, 512, kc), BF16),
                   jax.ShapeDtypeStruct((bn, lp, N_HEADS_C * LANES), BF16)],
        compiler_params=_cparams(("arbitrary", "arbitrary")),
        name="cache_layouts",
    )(p_k, p_v, p_ki, p_ck, p_cv, n_kv, n_misc, n_ck, n_cv)


def _s5_disc_kernel(lr_ref, li_ref, ls_ref, br_ref, bi_ref, ar_ref, ai_ref, bbr_ref, bbi_ref):
    lr, li = lr_ref[...], li_ref[...]
    dt = jnp.exp(ls_ref[...])
    mag = jnp.exp(lr * dt)
    ab_re = mag * jnp.cos(li * dt)
    ab_im = mag * jnp.sin(li * dt)
    den = lr * lr + li * li
    fr = ((ab_re - 1.0) * lr + ab_im * li) / den
    fi = (ab_im * lr - (ab_re - 1.0) * li) / den
    ar_ref[...] = ab_re
    ai_ref[...] = ab_im
    br, bi = br_ref[...], bi_ref[...]
    bbr_ref[...] = fr[:, None, :] * br - fi[:, None, :] * bi
    bbi_ref[...] = fr[:, None, :] * bi + fi[:, None, :] * br


def _s5_discretize(lam_re, lam_im, log_step, b_re, b_im):
    g, p, gs = b_re.shape
    shp = lambda *s: jax.ShapeDtypeStruct(s, F32)
    return pl.pallas_call(
        _s5_disc_kernel,
        out_shape=[shp(g, p), shp(g, p), shp(g, gs, p), shp(g, gs, p)],
        name="s5_disc",
    )(lam_re, lam_im, log_step.reshape(g, 1), jnp.swapaxes(b_re, 1, 2), jnp.swapaxes(b_im, 1, 2))


def _block_diag(blocks):
    g, r, c = blocks.shape
    eye = jnp.eye(g, dtype=bool)
    return jnp.where(eye[:, None, :, None], blocks[:, :, None, :], 0.0).reshape(g * r, g * c)


def _s5_kernel(u_ref, h0r_ref, h0i_ref, ar_ref, ai_ref, bd_ref, cr_ref, ci_ref, d_ref,
               y_ref, hr_ref, hi_ref, bur_ref, bui_ref, *, tc, bn):
    rows = tc * bn

    @pl.when(pl.program_id(0) == 0)
    def _():
        hr_ref[...] = h0r_ref[...]
        hi_ref[...] = h0i_ref[...]

    u = u_ref[...].reshape(rows, SSM_WIDTH)
    ub = u.astype(BF16)
    n_slab = N_GROUPS // S5_SLAB_GROUPS
    cw = S5_SLAB_GROUPS * GROUP_SIZE
    sw = S5_SLAB_GROUPS * STATE_DIM
    for k in range(n_slab):
        uk = ub[:, k * cw:(k + 1) * cw]
        bur_ref[:, k * sw:(k + 1) * sw] = _mm(uk, bd_ref[k * cw:(k + 1) * cw, k * sw:(k + 1) * sw])
        bui_ref[:, k * sw:(k + 1) * sw] = _mm(
            uk, bd_ref[k * cw:(k + 1) * cw, SSM_STATES + k * sw:SSM_STATES + (k + 1) * sw])

    for cc in range(SSM_STATES // S5_CW):
        sl = slice(cc * S5_CW, (cc + 1) * S5_CW)
        ar = jnp.broadcast_to(ar_ref[:, sl], (bn, S5_CW))
        ai = jnp.broadcast_to(ai_ref[:, sl], (bn, S5_CW))

        def step(t, carry, sl=sl, ar=ar, ai=ai):
            hr, hi = carry
            r0 = pl.multiple_of(t * bn, bn)
            nr = ar * hr - ai * hi + bur_ref[pl.ds(r0, bn), sl]
            ni = ar * hi + ai * hr + bui_ref[pl.ds(r0, bn), sl]
            bur_ref[pl.ds(r0, bn), sl] = nr
            bui_ref[pl.ds(r0, bn), sl] = ni
            return nr, ni

        hr, hi = lax.fori_loop(0, tc, step, (hr_ref[:, sl], hi_ref[:, sl]))
        hr_ref[:, sl] = hr
        hi_ref[:, sl] = hi

    for k in range(n_slab):
        hr = bur_ref[:, k * sw:(k + 1) * sw].astype(BF16)
        hi = bui_ref[:, k * sw:(k + 1) * sw].astype(BF16)
        yk = (_mm(hr, cr_ref[k * sw:(k + 1) * sw, k * cw:(k + 1) * cw])
              - _mm(hi, ci_ref[k * sw:(k + 1) * sw, k * cw:(k + 1) * cw])
              + d_ref[:, k * cw:(k + 1) * cw] * u[:, k * cw:(k + 1) * cw])
        y_ref[:, :, k * cw:(k + 1) * cw] = yk.reshape(tc, bn, cw)


def _s5(u_tm, h0_re, h0_im, ab_re, ab_im, bd, cr, ci, d_skip):
    n_time, bn, _ = u_tm.shape
    tc = min(S5_ROWS // bn, n_time)
    const = lambda *s: pl.BlockSpec(s, lambda i: (0,) * len(s))
    y, hr, hi = pl.pallas_call(
        functools.partial(_s5_kernel, tc=tc, bn=bn),
        grid=(n_time // tc,),
        in_specs=[pl.BlockSpec((tc, bn, SSM_WIDTH), lambda i: (i, 0, 0)),
                  const(bn, SSM_STATES), const(bn, SSM_STATES),
                  const(1, SSM_STATES), const(1, SSM_STATES),
                  const(SSM_WIDTH, 2 * SSM_STATES),
                  const(SSM_STATES, SSM_WIDTH), const(SSM_STATES, SSM_WIDTH),
                  const(1, SSM_WIDTH)],
        out_specs=[pl.BlockSpec((tc, bn, SSM_WIDTH), lambda i: (i, 0, 0)),
                   const(bn, SSM_STATES), const(bn, SSM_STATES)],
        out_shape=[jax.ShapeDtypeStruct((n_time, bn, SSM_WIDTH), F32),
                   jax.ShapeDtypeStruct((bn, SSM_STATES), F32),
                   jax.ShapeDtypeStruct((bn, SSM_STATES), F32)],
        scratch_shapes=[pltpu.VMEM((tc * bn, SSM_STATES), F32), pltpu.VMEM((tc * bn, SSM_STATES), F32)],
        compiler_params=_cparams(("arbitrary",)),
        name="s5",
    )(u_tm, h0_re, h0_im, ab_re, ab_im, bd, cr, ci, d_skip)
    return y, hr, hi


def _t5_bucket(rel):
    half = NUM_BUCKETS // 2
    max_exact = half // 2
    n = jnp.abs(rel)
    large = max_exact + (jnp.log(jnp.maximum(n, 1).astype(F32) / max_exact)
                         / math.log(MAX_DISTANCE / max_exact) * (half - max_exact)).astype(I32)
    large = jnp.minimum(large, half - 1)
    return jnp.where(rel > 0, half, 0) + jnp.where(n < max_exact, n, large)


DSA_NEAR_TILES = 3
DSA_SPLIT = 3
MASK_OFF = -(2.0 ** 100)


def _split3(x):
    parts, rem = [], x
    for _ in range(DSA_SPLIT):
        p = rem.astype(BF16).astype(F32)
        parts.append(p)
        rem = rem - p
    return parts


def _dsa_bias_consts(t5_table):
    q = jnp.arange(DSA_TQ, dtype=I32)[:, None]
    k = jnp.arange(DSA_KC, dtype=I32)[None, :]
    def lookup(bucket):
        out = jnp.zeros((N_HEADS_A,) + bucket.shape, F32)
        for b in range(NUM_BUCKETS):
            out = out + jnp.where(bucket[None] == b, t5_table[b].astype(F32).reshape((-1,) + (1,) * bucket.ndim), 0.0)
        return out

    far = lookup(_t5_bucket(jnp.full((), -(1 << 20), I32)))
    near = []
    for e in range(DSA_NEAR_TILES):
        near.append(lookup(_t5_bucket(k - q - LANES * e)) - far[:, None, None])
    near.append(jnp.zeros_like(near[0]))
    near = jnp.stack(near).reshape(DSA_NEAR_TILES + 1, N_HEADS_A * DSA_TQ, DSA_KC)
    lane = jnp.arange(LANES)[None, None, :]
    left = jnp.zeros((N_HEADS_A, DSA_TQ, LANES), F32)
    for i, part in enumerate(_split3(far)):
        left = jnp.where(lane == HEAD_DIM_A + i, part[:, None, None], left)
    eye = jnp.broadcast_to(jnp.eye(DSA_TQ, dtype=F32)[None], (N_HEADS_A, DSA_TQ, DSA_TQ))
    lhs_static = jnp.concatenate([left, eye], axis=-1).reshape(N_HEADS_A * DSA_TQ, 2 * LANES).astype(BF16)
    return near, lhs_static


def _sort_key(s):
    bits = lax.bitcast_convert_type(s, I32)
    return bits ^ (lax.shift_right_arithmetic(bits, 31) & 0x7FFFFFFF)


def _dsa_kernel(qa_ref, qit_ref, wt_ref, kx_ref, vx_ref, ki_ref, near_ref, lhs_ref, o_ref,
                skey_ref, mask_ref, thr_ref, tie_ref, *, q_off, kv_len, n_top, idx_bits, chunk_counts):
    tq, kc = DSA_TQ, DSA_KC
    qb = pl.program_id(1)
    q_base = q_off + qb * tq
    adm_end = jnp.minimum(((q_base + tq - 1) // CHUNK + 1) * CHUNK, kv_len)
    n_c = (adm_end + kc - 1) // kc

    q_pos_t = q_base + lax.broadcasted_iota(I32, (kc, tq), 1)
    k_loc_t = lax.broadcasted_iota(I32, (kc, tq), 0)

    qit = qit_ref[...].astype(BF16)
    w = wt_ref[...]
    rhs_pairs = [jnp.concatenate([qit[(2 * j) * IDX_DIM:(2 * j + 1) * IDX_DIM, :],
                                  qit[(2 * j + 1) * IDX_DIM:(2 * j + 2) * IDX_DIM, :]], axis=1)
                 for j in range(N_IDX_HEADS // 2)]

    def score_chunk(c, _):
        r0 = pl.multiple_of(c * kc, kc)
        kic = ki_ref[0, pl.ds(r0, kc), :]
        acc = jnp.zeros((kc, tq), F32)
        for j in range(N_IDX_HEADS // 2):
            d = jnp.maximum(_mm(kic, rhs_pairs[j]), 0.0)
            acc = acc + w[2 * j:2 * j + 1, :] * d[:, :tq] + w[2 * j + 1:2 * j + 2, :] * d[:, tq:]
        acc = jnp.where(acc == 0.0, 0.0, acc)
        k_pos = r0 + k_loc_t
        adm = ((k_pos // CHUNK) <= (q_pos_t // CHUNK)) & (k_pos < kv_len)
        skey_ref[pl.ds(r0, kc), :] = _sort_key(jnp.where(adm, acc, -jnp.inf))
        return 0

    lax.fori_loop(0, n_c, score_chunk, 0)

    def fold(ind):
        return jnp.sum(ind.reshape(kc // SUBLANES, SUBLANES, tq), axis=0)

    for nc in chunk_counts:
        @pl.when(n_c == nc)
        def _(nc=nc):
            def value_bit(i, t):
                cand = t ^ lax.shift_left(jnp.int32(1), 31 - i)
                part = jnp.zeros((SUBLANES, tq), F32)
                for c in range(nc):
                    part = part + fold(jnp.where(skey_ref[c * kc:(c + 1) * kc, :] >= cand, 1.0, 0.0))
                cnt = jnp.sum(part, axis=0, keepdims=True)
                return jnp.where(cnt >= n_top, cand, t)
            thr_ref[...] = lax.fori_loop(0, 32, value_bit, jnp.full((1, tq), INT_MIN, I32))

    thr = thr_ref[...]

    def count(pred_fn):
        def body(c, part):
            r0 = pl.multiple_of(c * kc, kc)
            return part + fold(jnp.where(pred_fn(skey_ref[pl.ds(r0, kc), :], r0 + k_loc_t), 1.0, 0.0))
        return jnp.sum(lax.fori_loop(0, n_c, body, jnp.zeros((SUBLANES, tq), F32)), axis=0, keepdims=True)

    cnt_gt = count(lambda sk, _: sk > thr)
    cnt_ge = count(lambda sk, _: sk >= thr)
    need = n_top - cnt_gt

    tie_ref[...] = jnp.full((1, tq), 1 << 30, I32)
    has_tie = (cnt_ge > n_top) & (thr > SKEY_NEG_INF)

    @pl.when(jnp.max(jnp.where(has_tie, 1.0, 0.0)) > 0.5)
    def _():
        def index_bit(i, j):
            cand = j | lax.shift_left(jnp.int32(1), idx_bits - 1 - i)
            cnt = count(lambda sk, kp: (sk == thr) & (kp < cand))
            return jnp.where(cnt <= need - 1.0, cand, j)
        tie_ref[...] = lax.fori_loop(0, idx_bits, index_bit, jnp.zeros((1, tq), I32))

    tie_idx = tie_ref[...]

    def mask_chunk(c, _):
        r0 = pl.multiple_of(c * kc, kc)
        sk = skey_ref[pl.ds(r0, kc), :]
        sel = ((sk > thr) | ((sk == thr) & (r0 + k_loc_t <= tie_idx))) & (sk > SKEY_NEG_INF)
        mask_ref[pl.ds(r0, kc), :] = jnp.where(sel, 0.0, MASK_OFF).astype(BF16)
        return 0

    lax.fori_loop(0, n_c, mask_chunk, 0)

    lane = lax.broadcasted_iota(I32, (tq, LANES), 1)
    q_rows = []
    for h in range(N_HEADS_A):
        pair = qa_ref[:, (h // 2) * LANES:(h // 2 + 1) * LANES] * (HEAD_DIM_A ** -0.5)
        if h % 2:
            pair = pltpu.roll(pair, HEAD_DIM_A, axis=1)
        q_rows.append(pair)
    q_left = jnp.concatenate(q_rows, axis=0).astype(BF16)
    lane_all = lax.broadcasted_iota(I32, (N_HEADS_A * tq, LANES), 1)
    lhs = jnp.concatenate([jnp.where(lane_all < HEAD_DIM_A, q_left, lhs_ref[:, :LANES]), lhs_ref[:, LANES:]],
                          axis=1)

    def logits(c):
        r0 = pl.multiple_of(c * kc, kc)
        rhs = jnp.concatenate([kx_ref[0, pl.ds(r0, kc), :], mask_ref[pl.ds(r0, kc), :]], axis=1)
        return _mm_nt(lhs, rhs) + near_ref[jnp.clip(q_base // LANES - c * (kc // LANES), 0, DSA_NEAR_TILES)]

    def attend(c, carry):
        s, m, acc = carry
        s_next = logits(jnp.minimum(c + 1, n_c - 1))
        r0 = pl.multiple_of(c * kc, kc)
        m_new = jnp.maximum(m, jnp.max(s, axis=1, keepdims=True))
        p = jnp.exp(s - m_new)
        acc = jnp.exp(m - m_new) * acc + _mm(p.astype(BF16), vx_ref[0, pl.ds(r0, kc), :])
        return s_next, m_new, acc

    rows = N_HEADS_A * tq
    _, m, acc = lax.fori_loop(
        0, n_c, attend,
        (logits(0), jnp.full((rows, 1), -jnp.inf, F32), jnp.zeros((rows, LANES), F32)))
    out = acc / acc[:, HEAD_DIM_A:HEAD_DIM_A + 1]
    for j in range(N_HEADS_A // 2):
        even = out[(2 * j) * tq:(2 * j + 1) * tq]
        odd = pltpu.roll(out[(2 * j + 1) * tq:(2 * j + 2) * tq], HEAD_DIM_A, axis=1)
        o_ref[:, j * LANES:(j + 1) * LANES] = jnp.where(lane < HEAD_DIM_A, even, odd)


def _dsa(qa, qit, wt, kx, vx, ki, near, lhs_static, *, q_off, kv_len, n_top):
    bn, tq_total, _ = qa.shape
    lp = kx.shape[1]
    nq = tq_total // DSA_TQ
    idx_bits = max(1, int(lp - 1).bit_length())
    counts = sorted({-(-min(((q_off + (qb + 1) * DSA_TQ - 1) // CHUNK + 1) * CHUNK, kv_len) // DSA_KC)
                     for qb in range(nq)})

    def qside(a):
        if a.ndim == 3:
            return pl.BlockSpec((None, a.shape[1], DSA_TQ), lambda b, q: (b, 0, q))
        return pl.BlockSpec((a.shape[0], DSA_TQ), lambda b, q: (0, b * nq + q))

    return pl.pallas_call(
        functools.partial(_dsa_kernel, q_off=q_off, kv_len=kv_len, n_top=float(n_top), idx_bits=idx_bits,
                          chunk_counts=tuple(counts)),
        grid=(bn, nq),
        in_specs=[pl.BlockSpec((None, DSA_TQ, N_HEADS_A * HEAD_DIM_A), lambda b, q: (b, q, 0)),
                  qside(qit), qside(wt),
                  pl.BlockSpec((1, lp, LANES), lambda b, q: (b, 0, 0)),
                  pl.BlockSpec((1, lp, LANES), lambda b, q: (b, 0, 0)),
                  pl.BlockSpec((1, lp, IDX_DIM), lambda b, q: (b, 0, 0)),
                  pl.BlockSpec((DSA_NEAR_TILES + 1, N_HEADS_A * DSA_TQ, DSA_KC), lambda b, q: (0, 0, 0)),
                  pl.BlockSpec((N_HEADS_A * DSA_TQ, 2 * LANES), lambda b, q: (0, 0))],
        out_specs=pl.BlockSpec((None, DSA_TQ, N_HEADS_A * HEAD_DIM_A), lambda b, q: (b, q, 0)),
        out_shape=jax.ShapeDtypeStruct((bn, tq_total, N_HEADS_A * HEAD_DIM_A), F32),
        scratch_shapes=[pltpu.VMEM((lp, DSA_TQ), I32), pltpu.VMEM((lp, DSA_TQ), BF16),
                        pltpu.VMEM((1, DSA_TQ), I32), pltpu.VMEM((1, DSA_TQ), I32)],
        compiler_params=_cparams(("arbitrary", "arbitrary")),
        name="dsa",
    )(qa, qit, wt, kx, vx, ki, near, lhs_static)


def _logf_kernel(cf_ref, b_ref, o_ref):
    o_ref[...] = jax.nn.log_sigmoid(cf_ref[...] + b_ref[...])


def _log_forget(cf2d, b_forget):
    m = cf2d.shape[0]
    tm = min(2048, m)
    return pl.pallas_call(
        _logf_kernel,
        grid=(m // tm,),
        in_specs=[pl.BlockSpec((tm, N_HEADS_C), lambda i: (i, 0)), pl.BlockSpec((1, N_HEADS_C), lambda i: (0, 0))],
        out_specs=pl.BlockSpec((tm, N_HEADS_C), lambda i: (i, 0)),
        out_shape=jax.ShapeDtypeStruct((m, N_HEADS_C), F32),
        compiler_params=_cparams(("arbitrary",)),
        name="logf",
    )(cf2d, b_forget.reshape(1, N_HEADS_C))


FOX_EXT_ROWS = 16
FOX_SPLIT = 3


def _split_bf16(x):
    parts = []
    rem = x
    for _ in range(FOX_SPLIT):
        p = rem.astype(BF16).astype(F32)
        parts.append(p)
        rem = rem - p
    return parts


def _cumsum_kernel(lf_ref, dc_ref, kext_ref, carry_ref):
    tm = lf_ref.shape[1]

    @pl.when(pl.program_id(1) == 0)
    def _():
        carry_ref[...] = jnp.zeros_like(carry_ref)

    lf = lf_ref[0]
    tri = (lax.broadcasted_iota(I32, (tm, tm), 1) <= lax.broadcasted_iota(I32, (tm, tm), 0)).astype(F32)
    dc = jnp.dot(tri, lf, preferred_element_type=F32, precision=lax.Precision.HIGHEST) + carry_ref[...]
    dc_ref[0] = dc
    carry_ref[...] = dc[tm - 1:tm, :]
    eye = (lax.broadcasted_iota(I32, (N_HEADS_C, N_HEADS_C), 0)
           == lax.broadcasted_iota(I32, (N_HEADS_C, N_HEADS_C), 1)).astype(F32)
    dct = lax.dot_general(eye, dc, (((1,), (1,)), ((), ())), preferred_element_type=F32,
                          precision=lax.Precision.HIGHEST)
    neg = _split_bf16(-dct)
    row = lax.broadcasted_iota(I32, (FOX_EXT_ROWS, tm), 0)
    for h in range(N_HEADS_C):
        tile = jnp.where(row < FOX_SPLIT, 1.0, 0.0)
        for i in range(FOX_SPLIT):
            tile = jnp.where(row == FOX_SPLIT + i, neg[i][h:h + 1, :], tile)
        kext_ref[0, 0, h] = tile.astype(BF16)


def _forget_cumsum(logf_all, kc):
    bn, lp, _ = logf_all.shape
    return pl.pallas_call(
        _cumsum_kernel,
        grid=(bn, lp // kc),
        in_specs=[pl.BlockSpec((1, kc, N_HEADS_C), lambda b, t: (b, t, 0))],
        out_specs=[pl.BlockSpec((1, kc, N_HEADS_C), lambda b, t: (b, t, 0)),
                   pl.BlockSpec((1, 1, N_HEADS_C, FOX_EXT_ROWS, kc), lambda b, t: (b, t, 0, 0, 0))],
        out_shape=[jax.ShapeDtypeStruct((bn, lp, N_HEADS_C), F32),
                   jax.ShapeDtypeStruct((bn, lp // kc, N_HEADS_C, FOX_EXT_ROWS, kc), BF16)],
        scratch_shapes=[pltpu.VMEM((1, N_HEADS_C), F32)],
        compiler_params=_cparams(("arbitrary", "arbitrary")),
        name="forget_cumsum",
    )(logf_all)


def _fox_kernel(q_ref, dq_ref, kt_ref, kext_ref, vx_ref, o_ref, qx_ref, m_ref, acc_ref, *, tq, kc, q_off):
    qb = pl.program_id(1)
    q_base = q_off + qb * tq
    n_c = (q_base + tq + kc - 1) // kc
    n_full = (q_base + 1) // kc
    lane = lax.broadcasted_iota(I32, (tq, LANES), 1)

    dq = dq_ref[0]
    for h in range(N_HEADS_C):
        pair = q_ref[:, (h // 2) * LANES:(h // 2 + 1) * LANES] * (HEAD_DIM_C ** -0.5)
        if h % 2:
            pair = pltpu.roll(pair, HEAD_DIM_C, axis=1)
        ext = jnp.where((lane >= HEAD_DIM_C + FOX_SPLIT) & (lane < HEAD_DIM_C + 2 * FOX_SPLIT), 1.0, 0.0)
        for i, part in enumerate(_split_bf16(dq[:, h:h + 1])):
            ext = jnp.where(lane == HEAD_DIM_C + i, part, ext)
        qx_ref[h] = jnp.where(lane < HEAD_DIM_C, pair, ext).astype(BF16)
    m_ref[...] = jnp.full(m_ref.shape, -jnp.inf, F32)
    acc_ref[...] = jnp.zeros(acc_ref.shape, F32)

    q_pos = q_base + lax.broadcasted_iota(I32, (tq, kc), 0)
    k_loc = lax.broadcasted_iota(I32, (tq, kc), 1)
    zrows = jnp.zeros((LANES - HEAD_DIM_C - FOX_EXT_ROWS, kc), BF16)

    def chunk(c, masked):
        r0 = pl.multiple_of(c * kc, kc)
        for h in range(N_HEADS_C):
            kx = jnp.concatenate([kt_ref[0, c, h * HEAD_DIM_C:(h + 1) * HEAD_DIM_C, :], kext_ref[0, c, h], zrows],
                                 axis=0)
            s = _mm(qx_ref[h], kx)
            if masked:
                s = jnp.where(r0 + k_loc <= q_pos, s, -jnp.inf)
            m_old = m_ref[h]
            m_new = jnp.maximum(m_old, jnp.broadcast_to(jnp.max(s, axis=1, keepdims=True), (tq, LANES)))
            p = jnp.exp(s - jnp.concatenate([m_new] * (kc // LANES), axis=1))
            pv = _mm(p.astype(BF16), vx_ref[0, pl.ds(r0, kc), h * LANES:(h + 1) * LANES])
            acc_ref[h] = jnp.exp(m_old - m_new) * acc_ref[h] + pv
            m_ref[h] = m_new

    def full_body(c, carry):
        chunk(c, False)
        return carry

    def diag_body(c, carry):
        chunk(c, True)
        return carry

    lax.fori_loop(0, n_full, full_body, 0)
    lax.fori_loop(n_full, n_c, diag_body, 0)

    for j in range(N_HEADS_C // 2):
        even = acc_ref[2 * j] / acc_ref[2 * j][:, HEAD_DIM_C:HEAD_DIM_C + 1]
        odd = acc_ref[2 * j + 1] / acc_ref[2 * j + 1][:, HEAD_DIM_C:HEAD_DIM_C + 1]
        o_ref[:, j * LANES:(j + 1) * LANES] = jnp.where(lane < HEAD_DIM_C, even, pltpu.roll(odd, HEAD_DIM_C, axis=1))


def _fox(cq, dq, kt, kext, vx, *, tq, q_off):
    bn, tq_total, width = cq.shape
    _, n_chunks, _, kc = kt.shape
    lp = n_chunks * kc
    return pl.pallas_call(
        functools.partial(_fox_kernel, tq=tq, kc=kc, q_off=q_off),
        grid=(bn, tq_total // tq),
        in_specs=[pl.BlockSpec((None, tq, width), lambda b, q: (b, q, 0)),
                  pl.BlockSpec((1, tq, N_HEADS_C), lambda b, q: (b, q, 0)),
                  pl.BlockSpec((1, n_chunks, width, kc), lambda b, q: (b, 0, 0, 0)),
                  pl.BlockSpec((1, n_chunks, N_HEADS_C, FOX_EXT_ROWS, kc), lambda b, q: (b, 0, 0, 0, 0)),
                  pl.BlockSpec((1, lp, N_HEADS_C * LANES), lambda b, q: (b, 0, 0))],
        out_specs=pl.BlockSpec((None, tq, width), lambda b, q: (b, q, 0)),
        out_shape=jax.ShapeDtypeStruct((bn, tq_total, width), F32),
        scratch_shapes=[pltpu.VMEM((N_HEADS_C, tq, LANES), BF16),
                        pltpu.VMEM((N_HEADS_C, tq, LANES), F32),
                        pltpu.VMEM((N_HEADS_C, tq, LANES), F32)],
        compiler_params=_cparams(("arbitrary", "arbitrary")),
        name="fox",
    )(cq, dq, kt, kext, vx)


def _merge_kernel(x_ref, ya_ref, yb_ref, yc_ref, g_ref, wa_ref, wb_ref, wc_ref, wo_ref, lng_ref, lnb_ref,
                  o_ref, *, alpha):
    ba = _mm(ya_ref[...].astype(BF16), wa_ref[...])
    bc = _mm(yc_ref[...].astype(BF16), wc_ref[...])
    glu = _mm(jax.nn.gelu(yb_ref[...]).astype(BF16), wb_ref[...])
    bb = glu[:, :D_MODEL] * jax.nn.sigmoid(glu[:, D_MODEL:])
    g = jax.nn.sigmoid(g_ref[...])
    merged = g[:, :D_MODEL] * ba + g[:, D_MODEL:2 * D_MODEL] * bb + g[:, 2 * D_MODEL:] * bc
    out = _mm(merged.astype(BF16), wo_ref[...])
    o_ref[...] = _layer_norm(alpha * x_ref[...] + out, lng_ref[...], lnb_ref[...])


def _merge(x2d, ya, yb, yb_time_major, yc, gates, wa, wb, wc, wo, ln_g, ln_b, alpha, n_time):
    m = x2d.shape[0]
    tm = min(OUT_TM, m)
    row = lambda w: pl.BlockSpec((tm, w), lambda i: (i, 0))
    const = lambda *s: pl.BlockSpec(s, lambda i: (0,) * len(s))
    if yb_time_major:
        n_t = n_time // tm
        yb_spec = pl.BlockSpec((tm, SSM_WIDTH), lambda i: (i % n_t, i // n_t))
    else:
        yb_spec = row(SSM_WIDTH)
    return pl.pallas_call(
        functools.partial(_merge_kernel, alpha=alpha),
        grid=(m // tm,),
        in_specs=[row(D_MODEL), row(512), yb_spec, row(512), row(N_BRANCH * D_MODEL),
                  const(512, D_MODEL), const(SSM_WIDTH, 2 * D_MODEL), const(512, D_MODEL),
                  const(D_MODEL, D_MODEL), const(1, D_MODEL), const(1, D_MODEL)],
        out_specs=row(D_MODEL),
        out_shape=jax.ShapeDtypeStruct((m, D_MODEL), F32),
        compiler_params=_cparams(("arbitrary",)),
        name="merge",
    )(x2d, ya, yb, yc, gates, wa, wb, wc, wo, ln_g.reshape(1, -1), ln_b.reshape(1, -1))


def _ffn_kernel(x_ref, wg_ref, wu_ref, wd_ref, lng_ref, lnb_ref, o_ref, acc_ref, *, alpha):
    c = pl.program_id(1)

    @pl.when(c == 0)
    def _():
        acc_ref[...] = jnp.zeros_like(acc_ref)

    xb = x_ref[...].astype(BF16)
    a = jax.nn.silu(_mm(xb, wg_ref[...])) * _mm(xb, wu_ref[...])
    acc_ref[...] += _mm(a.astype(BF16), wd_ref[...])

    @pl.when(c == pl.num_programs(1) - 1)
    def _():
        o_ref[...] = _layer_norm(alpha * x_ref[...] + acc_ref[...], lng_ref[...], lnb_ref[...])


def _ffn(x2d, w_gu, w_down, ln_g, ln_b, alpha):
    m = x2d.shape[0]
    tm = min(FFN_TM, m)
    d_ff = w_down.shape[0]
    n_c = d_ff // FFN_FC
    return pl.pallas_call(
        functools.partial(_ffn_kernel, alpha=alpha),
        grid=(m // tm, n_c),
        in_specs=[pl.BlockSpec((tm, D_MODEL), lambda i, c: (i, 0)),
                  pl.BlockSpec((D_MODEL, FFN_FC), lambda i, c: (0, c)),
                  pl.BlockSpec((D_MODEL, FFN_FC), lambda i, c: (0, n_c + c)),
                  pl.BlockSpec((FFN_FC, D_MODEL), lambda i, c: (c, 0)),
                  pl.BlockSpec((1, D_MODEL), lambda i, c: (0, 0)),
                  pl.BlockSpec((1, D_MODEL), lambda i, c: (0, 0))],
        out_specs=pl.BlockSpec((tm, D_MODEL), lambda i, c: (i, 0)),
        out_shape=jax.ShapeDtypeStruct((m, D_MODEL), F32),
        scratch_shapes=[pltpu.VMEM((tm, D_MODEL), F32)],
        compiler_params=_cparams(("arbitrary", "arbitrary")),
        name="ffn",
    )(x2d, w_gu, w_gu, w_down, ln_g.reshape(1, -1), ln_b.reshape(1, -1))


MOE_CAP = 384


def _moe_kernel(x_ref, wrt_ref, brt_ref, tri_ref, wgu_ref, wd_ref, lng_ref, lnb_ref, o_ref,
                acc_ref, ind_ref, pos_ref, gate_ref, *, alpha):
    e = pl.program_id(1)
    tm = x_ref.shape[0]
    xb = x_ref[...].astype(BF16)
    row = lax.broadcasted_iota(I32, (N_EXPERTS, tm), 0).astype(F32)

    @pl.when(e == 0)
    def _():
        acc_ref[...] = jnp.zeros_like(acc_ref)
        logits = _mm_nt(wrt_ref[...], xb) + brt_ref[...]
        m1 = jnp.max(logits, axis=0, keepdims=True)
        i1 = jnp.min(jnp.where(logits == m1, row, float(N_EXPERTS)), axis=0, keepdims=True)
        rest = jnp.where(row == i1, -jnp.inf, logits)
        m2 = jnp.max(rest, axis=0, keepdims=True)
        i2 = jnp.min(jnp.where(rest == m2, row, float(N_EXPERTS)), axis=0, keepdims=True)
        ex = jnp.exp(m2 - m1)
        ind = jnp.where((row == i1) | (row == i2), 1.0, 0.0)
        ind_ref[...] = ind
        gate_ref[...] = jnp.where(row == i1, 1.0 / (1.0 + ex), 0.0) + jnp.where(row == i2, ex / (1.0 + ex), 0.0)
        pos_ref[...] = _mm(ind.astype(BF16), tri_ref[...])

    sel = ind_ref[pl.ds(e, 1), :]
    pos = pos_ref[pl.ds(e, 1), :]
    gate = gate_ref[pl.ds(e, 1), :]
    count = jnp.sum(sel)
    slot = lax.broadcasted_iota(I32, (MOE_CAP, tm), 0).astype(F32)

    def block(j, carry):
        pick = (pos == slot + (j * MOE_CAP).astype(F32)) & (sel > 0.5)
        pick_b = jnp.where(pick, 1.0, 0.0).astype(BF16)
        xe = _mm(pick_b, xb).astype(BF16)
        h = _mm(xe, wgu_ref[0])
        a = jax.nn.silu(h[:, :D_FF_EXPERT]) * h[:, D_FF_EXPERT:]
        ge = jnp.sum(jnp.where(pick, gate, 0.0), axis=1, keepdims=True)
        y = (ge * _mm(a.astype(BF16), wd_ref[0])).astype(BF16)
        acc_ref[...] += lax.dot_general(pick_b, y, (((0,), (0,)), ((), ())), preferred_element_type=F32)
        return carry

    n_blocks = (count.astype(I32) + MOE_CAP - 1) // MOE_CAP
    lax.fori_loop(0, n_blocks, block, 0)

    @pl.when(e == pl.num_programs(1) - 1)
    def _():
        o_ref[...] = _layer_norm(alpha * x_ref[...] + acc_ref[...], lng_ref[...], lnb_ref[...])


def _moe(x2d, w_router, b_router, w_exp_gu, w_exp_down, ln_g, ln_b, alpha):
    m = x2d.shape[0]
    tm = min(MOE_TM, m)
    tri = (jnp.arange(tm)[:, None] < jnp.arange(tm)[None, :]).astype(BF16)
    return pl.pallas_call(
        functools.partial(_moe_kernel, alpha=alpha),
        grid=(m // tm, N_EXPERTS),
        in_specs=[pl.BlockSpec((tm, D_MODEL), lambda i, e: (i, 0)),
                  pl.BlockSpec((N_EXPERTS, D_MODEL), lambda i, e: (0, 0)),
                  pl.BlockSpec((N_EXPERTS, 1), lambda i, e: (0, 0)),
                  pl.BlockSpec((tm, tm), lambda i, e: (0, 0)),
                  pl.BlockSpec((1, D_MODEL, 2 * D_FF_EXPERT), lambda i, e: (e, 0, 0)),
                  pl.BlockSpec((1, D_FF_EXPERT, D_MODEL), lambda i, e: (e, 0, 0)),
                  pl.BlockSpec((1, D_MODEL), lambda i, e: (0, 0)),
                  pl.BlockSpec((1, D_MODEL), lambda i, e: (0, 0))],
        out_specs=pl.BlockSpec((tm, D_MODEL), lambda i, e: (i, 0)),
        out_shape=jax.ShapeDtypeStruct((m, D_MODEL), F32),
        scratch_shapes=[pltpu.VMEM((tm, D_MODEL), F32), pltpu.VMEM((N_EXPERTS, tm), F32),
                        pltpu.VMEM((N_EXPERTS, tm), F32), pltpu.VMEM((N_EXPERTS, tm), F32)],
        compiler_params=_cparams(("arbitrary", "arbitrary")),
        name="moe",
    )(x2d, jnp.swapaxes(w_router, 0, 1), b_router.reshape(-1, 1), tri, w_exp_gu, w_exp_down,
      ln_g.reshape(1, -1), ln_b.reshape(1, -1))


def _pad_rows(a, rows):
    if a.shape[1] == rows:
        return a
    pad = jnp.zeros((a.shape[0], rows - a.shape[1]) + a.shape[2:], a.dtype)
    return jnp.concatenate([a, pad], axis=1)


def _round_up(n, mult):
    return -(-n // mult) * mult


def _token_mixer(x, past, lw):
    bn, n_time, _ = x.shape
    m = bn * n_time
    x2d = x.reshape(m, D_MODEL)
    prompt = past is None
    per_batch = lambda a: a.reshape((bn, -1) + a.shape[1:])

    if prompt:
        past_len = 0
        proj = _project_prompt(x2d, lw["w_in"], lw["b_forget"], bn, n_time, lw["prev_ck"], lw["prev_cv"])
        a_k, a_v, a_ki, logf = [per_batch(proj[n]) for n in ("a_k", "a_v", "a_ki", "logf")]
        c_k, c_v = proj["ck"], proj["cv"]
        logf_all = logf
        h0_re = jnp.zeros((bn, SSM_STATES), F32)
        h0_im = jnp.zeros((bn, SSM_STATES), F32)
        u_tm = proj["u"].reshape(n_time, bn, SSM_WIDTH)
        qa, qit, wt = per_batch(proj["qa"]), proj["qit"], proj["wt"]
        kx, vxa, ki16, vxc = [per_batch(proj[n]) for n in ("kx", "vxa", "ki16", "vxc")]
        kt = proj["kt"].reshape(bn, n_time // FOX_KC, 512, FOX_KC)
    else:
        p_ak, p_av, p_aki, p_hr, p_hi, p_ck, p_cv, p_logf = past
        past_len = p_ak.shape[1]
        proj = _project(x2d, lw["w_in"])
        kv, misc = per_batch(proj["kv"]), per_batch(proj["misc"])
        a_k, a_v = kv[..., :HEAD_DIM_A], kv[..., HEAD_DIM_A:]
        a_ki = misc[..., :IDX_DIM]
        a_w = misc[..., MISC_W_LANE:MISC_W_LANE + N_IDX_HEADS]
        c_f = misc[..., MISC_F_LANE:MISC_F_LANE + N_HEADS_C]
        c_k, c_v = per_batch(proj["ck"]), per_batch(proj["cv"])
        logf = _log_forget(c_f.reshape(m, N_HEADS_C), lw["b_forget"]).reshape(bn, n_time, N_HEADS_C)
        logf_all = jnp.concatenate([p_logf, logf], axis=1)
        h0_re = p_hr.reshape(bn, SSM_STATES)
        h0_im = p_hi.reshape(bn, SSM_STATES)
        u_tm = jnp.swapaxes(per_batch(proj["u"]), 0, 1)
        tq_pad = _round_up(n_time, DSA_TQ)
        qa = _pad_rows(per_batch(proj["qa"]), tq_pad)
        qit = jnp.swapaxes(_pad_rows(per_batch(proj["qi"]), tq_pad), 1, 2)
        wt = jnp.swapaxes(_pad_rows(a_w, tq_pad), 1, 2)
        kx, vxa, ki16, kt, vxc = _cache_layouts(
            p_ak, p_av, p_aki, p_ck.reshape(bn, past_len, -1), p_cv.reshape(bn, past_len, -1),
            _pad_rows(kv, FOX_KC), _pad_rows(misc, FOX_KC), _pad_rows(c_k, FOX_KC), _pad_rows(c_v, FOX_KC))
    kv_len = past_len + n_time
    n_top = min(TOPK_MAX, kv_len // 4)

    y_b, h_re, h_im = _s5(u_tm, h0_re, h0_im, lw["ab_re"], lw["ab_im"], lw["bd"], lw["cr"], lw["ci"], lw["ssm_d"])
    if prompt:
        y_b = y_b.reshape(n_time, bn * SSM_WIDTH)
    else:
        y_b = jnp.swapaxes(y_b, 0, 1).reshape(m, SSM_WIDTH)

    y_a = _dsa(qa, qit, wt, kx, vxa, ki16, lw["bias_near"], lw["bias_lhs"],
               q_off=past_len, kv_len=kv_len, n_top=n_top)[:, :n_time].reshape(m, -1)

    lpc = _round_up(kv_len, FOX_KC)
    dcum, kext = _forget_cumsum(_pad_rows(logf_all, lpc), FOX_KC)
    y_c = _fox(per_batch(proj["cq"]), dcum[:, past_len:kv_len], kt, kext, vxc,
               tq=min(FOX_TQ_PROMPT, n_time), q_off=past_len).reshape(m, -1)

    x1 = _merge(x2d, y_a, y_b, prompt, y_c, proj["gates"], lw["w_a_out"], lw["w_b_glu"], lw["w_c_out"], lw["w_o"],
                lw["ln1_g"], lw["ln1_b"], lw["alpha"], n_time)
    if not prompt:
        c_k = c_k.reshape(bn, n_time, N_HEADS_C, HEAD_DIM_C)
        c_v = c_v.reshape(bn, n_time, N_HEADS_C, HEAD_DIM_C)
    new_state = (a_k, a_v, a_ki, h_re.reshape(bn, N_GROUPS, STATE_DIM), h_im.reshape(bn, N_GROUPS, STATE_DIM),
                 c_k, c_v, logf)
    return x1, new_state


def kernel(x_prompt, x_sample, cache_a_k, cache_a_v, cache_a_kidx, state_ssm_re, state_ssm_im, cache_c_k, cache_c_v, cache_c_logf, w_in, b_forget, ssm_lam_re, ssm_lam_im, ssm_log_step, ssm_b_re, ssm_b_im, ssm_c_re, ssm_c_im, ssm_d, w_a_out, w_b_glu, w_c_out, w_o, ln1_g, ln1_b, ln2_g, ln2_b, t5_table, w_ffn_gu, w_ffn_down, w_router, b_router, w_exp_gu, w_exp_down):
    depth = w_in.shape[0]
    alpha = float((2 * depth) ** 0.25)
    bias_near, bias_lhs = _dsa_bias_consts(t5_table)
    xp, xs = x_prompt, x_sample
    rows_p = [[] for _ in range(8)]
    rows_s = [[] for _ in range(8)]
    prev_ck = prev_cv = None
    for layer in range(depth):
        ab_re, ab_im, bb_re, bb_im = _s5_discretize(ssm_lam_re[layer], ssm_lam_im[layer], ssm_log_step[layer],
                                                    ssm_b_re[layer], ssm_b_im[layer])
        lw = dict(
            w_in=_pack_w_in(w_in[layer]), b_forget=b_forget[layer],
            ab_re=ab_re.reshape(1, SSM_STATES), ab_im=ab_im.reshape(1, SSM_STATES),
            bd=jnp.concatenate([_block_diag(bb_re), _block_diag(bb_im)], axis=1).astype(BF16),
            cr=_block_diag(jnp.swapaxes(ssm_c_re[layer], 1, 2)).astype(BF16),
            ci=_block_diag(jnp.swapaxes(ssm_c_im[layer], 1, 2)).astype(BF16),
            ssm_d=ssm_d[layer].reshape(1, SSM_WIDTH),
            w_a_out=w_a_out[layer].astype(BF16), w_b_glu=w_b_glu[layer].astype(BF16),
            w_c_out=w_c_out[layer].astype(BF16), w_o=w_o[layer].astype(BF16),
            ln1_g=ln1_g[layer], ln1_b=ln1_b[layer], bias_near=bias_near, bias_lhs=bias_lhs, alpha=alpha)
        past = (cache_a_k[layer], cache_a_v[layer], cache_a_kidx[layer], state_ssm_re[layer],
                state_ssm_im[layer], cache_c_k[layer], cache_c_v[layer], cache_c_logf[layer])
        xp1, st_p = _token_mixer(xp, None, dict(lw, prev_ck=prev_ck, prev_cv=prev_cv))
        prev_ck, prev_cv = st_p[5], st_p[6]
        xs1, st_s = _token_mixer(xs, past, lw)
        i = layer // 2
        if layer % 2 == 0:
            wgu, wdn = w_ffn_gu[i].astype(BF16), w_ffn_down[i].astype(BF16)
            xp2 = _ffn(xp1, wgu, wdn, ln2_g[layer], ln2_b[layer], alpha)
            xs2 = _ffn(xs1, wgu, wdn, ln2_g[layer], ln2_b[layer], alpha)
        else:
            wr, wgu, wdn = w_router[i].astype(BF16), w_exp_gu[i].astype(BF16), w_exp_down[i].astype(BF16)
            xp2 = _moe(xp1, wr, b_router[i], wgu, wdn, ln2_g[layer], ln2_b[layer], alpha)
            xs2 = _moe(xs1, wr, b_router[i], wgu, wdn, ln2_g[layer], ln2_b[layer], alpha)
        xp = xp2.reshape(x_prompt.shape)
        xs = xs2.reshape(x_sample.shape)
        for j in range(8):
            rows_p[j].append(st_p[j])
            rows_s[j].append(st_s[j])
    bshape = (depth,) + x_prompt.shape[:2] + (N_HEADS_C, HEAD_DIM_C)
    rows_p[5], rows_p[6] = None, None
    (a_k_p, a_v_p, a_kidx_p, ssm_re_p, ssm_im_p, _, _, c_logf_p) = [None if r is None else jnp.stack(r) for r in rows_p]
    c_k_p, c_v_p = prev_ck.reshape(bshape), prev_cv.reshape(bshape)
    (a_k_s, a_v_s, a_kidx_s, ssm_re_s, ssm_im_s, c_k_s, c_v_s, c_logf_s) = [jnp.stack(r) for r in rows_s]
    return (xp, xs, a_k_p, a_k_s, a_v_p, a_v_s, a_kidx_p, a_kidx_s, ssm_re_p, ssm_re_s,
            ssm_im_p, ssm_im_s, c_k_p, c_k_s, c_v_p, c_v_s, c_logf_p, c_logf_s)
```

```python
import functools
import math

import jax
import jax.numpy as jnp
import numpy as np
from jax import lax
from jax.experimental import pallas as pl
from jax.experimental.pallas import tpu as pltpu

F32 = jnp.float32
BF16 = jnp.bfloat16
I32 = jnp.int32

D_MODEL = 1024
CHUNK = 64
N_HEADS_A = 8
HEAD_DIM_A = 64
N_IDX_HEADS = 8
IDX_DIM = 32
TOPK_MAX = 256
NUM_BUCKETS = 32
MAX_DISTANCE = 128
SSM_WIDTH = 512
GROUP_SIZE = 16
N_GROUPS = SSM_WIDTH // GROUP_SIZE
STATE_DIM = 64
SSM_STATES = N_GROUPS * STATE_DIM
N_HEADS_C = 8
HEAD_DIM_C = 64
N_BRANCH = 3
D_FF = 2816
N_EXPERTS = 8
D_FF_EXPERT = 1408
LN_EPS = 1e-5
PROJ_SIZES = (512, 64, 64, 256, 32, 8, 512, 512, 512, 512, 8, 3072)

LANES = 128
SUBLANES = 8
VMEM_LIMIT_BYTES = 56 * 1024 * 1024

PROJ_TM = 256
S5_ROWS = 512
S5_CW = 256
S5_SLAB_GROUPS = 8
DSA_TQ = 128
DSA_KC = 256
FOX_TQ_PROMPT = 256
FOX_KC = 256
OUT_TM = 256
FFN_TM = 1024
FFN_FC = 1408
MOE_TM = 1024

INT_MIN = -(2 ** 31)
SKEY_NEG_INF = -2139095041


def _cparams(sem):
    return pltpu.CompilerParams(dimension_semantics=sem, vmem_limit_bytes=VMEM_LIMIT_BYTES)


def _mm(a, b):
    return jnp.dot(a, b, preferred_element_type=F32)


def _mm_nt(a, b):
    return lax.dot_general(a, b, (((1,), (1,)), ((), ())), preferred_element_type=F32)


def _layer_norm(z, g, b):
    mu = jnp.mean(z, axis=-1, keepdims=True)
    zc = z - mu
    var = jnp.mean(zc * zc, axis=-1, keepdims=True)
    return zc * lax.rsqrt(var + LN_EPS) * g + b


def _pack_w_in(w):
    offs = np.cumsum((0,) + PROJ_SIZES)
    a_q, a_k, a_v, a_qi, a_ki, a_w, b_u, c_q, c_k, c_v, c_f, gates = [
        w[:, offs[i]:offs[i + 1]] for i in range(len(PROJ_SIZES))]
    pad = jnp.zeros((w.shape[0], LANES - IDX_DIM - N_IDX_HEADS - N_HEADS_C), w.dtype)
    return jnp.concatenate([a_q, a_k, a_v, a_qi, a_ki, a_w, c_f, pad, b_u, c_q, c_k, c_v, gates],
                           axis=1).astype(BF16)


PROJ_OUT = (("qa", 0, 512), ("kv", 512, 128), ("qi", 640, 256), ("misc", 896, 128), ("u", 1024, 512),
            ("cq", 1536, 512), ("ck", 2048, 512), ("cv", 2560, 512), ("gates", 3072, 3072))
PROJ_COLS_PACKED = 6144


def _proj_kernel(x_ref, w_ref, *out_refs):
    xb = x_ref[...].astype(BF16)
    for (_, lo, width), o_ref in zip(PROJ_OUT, out_refs):
        for c in range(0, width, 512):
            cw = min(512, width - c)
            o_ref[:, c:c + cw] = _mm(xb, w_ref[:, lo + c:lo + c + cw])


def _project(x2d, w_packed):
    m = x2d.shape[0]
    tm = min(PROJ_TM, m)
    outs = pl.pallas_call(
        _proj_kernel,
        grid=(m // tm,),
        in_specs=[pl.BlockSpec((tm, D_MODEL), lambda i: (i, 0)),
                  pl.BlockSpec((D_MODEL, PROJ_COLS_PACKED), lambda i: (0, 0))],
        out_specs=[pl.BlockSpec((tm, width), lambda i: (i, 0)) for _, _, width in PROJ_OUT],
        out_shape=[jax.ShapeDtypeStruct((m, width), F32) for _, _, width in PROJ_OUT],
        compiler_params=_cparams(("arbitrary",)),
        name="proj",
    )(x2d, w_packed)
    return dict(zip([p[0] for p in PROJ_OUT], outs))


MISC_W_LANE = IDX_DIM
MISC_F_LANE = IDX_DIM + N_IDX_HEADS
DSA_K_ONES = 3


def _attn_layouts(kv, misc, ck, cv, kx_ref, vxa_ref, ki16_ref, kt_ref, vxc_ref):
    rows = kv.shape[0]
    lane = lax.broadcasted_iota(I32, (rows, LANES), 1)
    ones_k = jnp.where((lane >= HEAD_DIM_A) & (lane < HEAD_DIM_A + DSA_K_ONES), 1.0, 0.0)
    ones_v = jnp.where(lane == HEAD_DIM_A, 1.0, 0.0)
    kx_ref[...] = jnp.where(lane < HEAD_DIM_A, kv, ones_k).astype(BF16)
    vxa_ref[...] = jnp.where(lane < HEAD_DIM_A, pltpu.roll(kv, HEAD_DIM_A, axis=1), ones_v).astype(BF16)
    ki16_ref[...] = misc[:, :IDX_DIM].astype(BF16)
    kt_ref[...] = ck.T.astype(BF16)
    for h in range(N_HEADS_C):
        pair = cv[:, (h // 2) * LANES:(h // 2 + 1) * LANES]
        if h % 2:
            pair = pltpu.roll(pair, HEAD_DIM_C, axis=1)
        vxc_ref[:, h * LANES:(h + 1) * LANES] = jnp.where(lane < HEAD_DIM_C, pair, ones_v).astype(BF16)


def _proj_prompt_kernel(x_ref, w_ref, bf_ref, *refs, n_prev):
    if n_prev:
        pck_ref, pcv_ref = refs[:2]
        refs = refs[2:]
    (qa_ref, qit_ref, wt_ref, ak_ref, av_ref, aki_ref, logf_ref, u_ref, cq_ref, ck_ref, cv_ref, g_ref,
     kx_ref, vxa_ref, ki16_ref, kt_ref, vxc_ref) = refs
    xb = x_ref[...].astype(BF16)
    cols = {name: (lo, width) for name, lo, width in PROJ_OUT}

    def mm(name, c0=0, cw=None):
        lo, width = cols[name]
        cw = width if cw is None else cw
        return _mm(xb, w_ref[:, lo + c0:lo + c0 + cw])

    qa_ref[...] = mm("qa")
    u_ref[...] = mm("u")
    for c in range(0, cols["gates"][1], 512):
        g_ref[:, c:c + 512] = mm("gates", c, 512)
    small = mm("kv", 0, cols["kv"][1] + cols["qi"][1] + cols["misc"][1])
    kv = small[:, :LANES]
    qi = small[:, LANES:LANES + cols["qi"][1]]
    misc = small[:, LANES + cols["qi"][1]:]
    cqk = mm("cq", 0, cols["cq"][1] + cols["ck"][1])
    cq_ref[...] = cqk[:, :cols["cq"][1]]
    ck = cqk[:, cols["cq"][1]:]
    cv = mm("cv")
    if n_prev:
        ck_ref[:n_prev] = pck_ref[...]
        cv_ref[:n_prev] = pcv_ref[...]
    for h in range(N_HEADS_C):
        pk = ck[:, (h // 2) * LANES:(h // 2 + 1) * LANES]
        pv = cv[:, (h // 2) * LANES:(h // 2 + 1) * LANES]
        if h % 2:
            pk = pltpu.roll(pk, HEAD_DIM_C, axis=1)
            pv = pltpu.roll(pv, HEAD_DIM_C, axis=1)
        ck_ref[n_prev, :, h, :] = pk[:, :HEAD_DIM_C]
        cv_ref[n_prev, :, h, :] = pv[:, :HEAD_DIM_C]
    qit_ref[...] = qi.T
    wt_ref[...] = misc.T[MISC_W_LANE:MISC_W_LANE + N_IDX_HEADS, :]
    ak_ref[...] = kv[:, :HEAD_DIM_A]
    av_ref[...] = pltpu.roll(kv, HEAD_DIM_A, axis=1)[:, :HEAD_DIM_A]
    aki_ref[...] = misc[:, :IDX_DIM]
    lf = jax.nn.log_sigmoid(misc + bf_ref[...])
    logf_ref[...] = pltpu.roll(lf, LANES - MISC_F_LANE, axis=1)[:, :N_HEADS_C]
    _attn_layouts(kv, misc, ck, cv, kx_ref, vxa_ref, ki16_ref, kt_ref.at[0], vxc_ref)


def _project_prompt(x2d, w_packed, b_forget, n_batch, n_time, prev_ck, prev_cv):
    m = x2d.shape[0]
    n_prev = 0 if prev_ck is None else prev_ck.shape[0]
    tm = FOX_KC
    n_t = n_time // tm
    bf = jnp.zeros((1, LANES), F32).at[0, MISC_F_LANE:MISC_F_LANE + N_HEADS_C].set(b_forget)
    row = lambda w, dt=F32: (jax.ShapeDtypeStruct((m, w), dt), pl.BlockSpec((tm, w), lambda i: (i, 0)))
    col = lambda r: (jax.ShapeDtypeStruct((r, m), F32), pl.BlockSpec((r, tm), lambda i: (0, i)))
    head_shape = (N_HEADS_C, HEAD_DIM_C)
    stacked = (jax.ShapeDtypeStruct((n_prev + 1, m) + head_shape, F32),
               pl.BlockSpec((n_prev + 1, tm) + head_shape, lambda i: (0, i, 0, 0)))
    prev_specs = [pl.BlockSpec((n_prev, tm) + head_shape, lambda i: (0, i, 0, 0))] * 2 if n_prev else []
    prev_args = [prev_ck, prev_cv] if n_prev else []
    outs = dict(
        qa=row(512), qit=col(N_IDX_HEADS * IDX_DIM), wt=col(N_IDX_HEADS),
        a_k=row(HEAD_DIM_A), a_v=row(HEAD_DIM_A), a_ki=row(IDX_DIM), logf=row(N_HEADS_C),
        u=(jax.ShapeDtypeStruct((n_time, n_batch * SSM_WIDTH), F32),
           pl.BlockSpec((tm, SSM_WIDTH), lambda i: (i % n_t, i // n_t))),
        cq=row(512), ck=stacked, cv=stacked, gates=row(N_BRANCH * D_MODEL),
        kx=row(LANES, BF16), vxa=row(LANES, BF16), ki16=row(IDX_DIM, BF16),
        kt=(jax.ShapeDtypeStruct((m // tm, 512, tm), BF16), pl.BlockSpec((1, 512, tm), lambda i: (i, 0, 0))),
        vxc=row(N_HEADS_C * LANES, BF16))
    res = pl.pallas_call(
        functools.partial(_proj_prompt_kernel, n_prev=n_prev),
        grid=(m // tm,),
        in_specs=[pl.BlockSpec((tm, D_MODEL), lambda i: (i, 0)),
                  pl.BlockSpec((D_MODEL, PROJ_COLS_PACKED), lambda i: (0, 0)),
                  pl.BlockSpec((1, LANES), lambda i: (0, 0))] + prev_specs,
        out_specs=[v[1] for v in outs.values()],
        out_shape=[v[0] for v in outs.values()],
        compiler_params=_cparams(("arbitrary",)),
        name="proj_prompt",
    )(x2d, w_packed, bf, *prev_args)
    return dict(zip(outs.keys(), res))


def _cache_layout_kernel(pk_ref, pv_ref, pki_ref, pck_ref, pcv_ref, nkv_ref, nmisc_ref, nck_ref, ncv_ref,
                         kx_ref, vxa_ref, ki16_ref, kt_ref, vxc_ref, *, n_past):
    is_new = pl.program_id(1) >= n_past
    rows = nkv_ref.shape[1]
    lane = lax.broadcasted_iota(I32, (rows, LANES), 1)
    zeros_k = jnp.zeros((rows, LANES - HEAD_DIM_A), F32)
    past_kv = jnp.where(lane < HEAD_DIM_A, jnp.concatenate([pk_ref[0], zeros_k], axis=1),
                        pltpu.roll(jnp.concatenate([pv_ref[0], zeros_k], axis=1), HEAD_DIM_A, axis=1))
    past_misc = jnp.concatenate([pki_ref[0], jnp.zeros((rows, LANES - IDX_DIM), F32)], axis=1)
    kv = jnp.where(is_new, nkv_ref[0], past_kv)
    misc = jnp.where(is_new, nmisc_ref[0], past_misc)
    ck = jnp.where(is_new, nck_ref[0], pck_ref[0])
    cv = jnp.where(is_new, ncv_ref[0], pcv_ref[0])
    _attn_layouts(kv, misc, ck, cv, kx_ref.at[0], vxa_ref.at[0], ki16_ref.at[0], kt_ref.at[0, 0], vxc_ref.at[0])


def _cache_layouts(p_k, p_v, p_ki, p_ck, p_cv, n_kv, n_misc, n_ck, n_cv):
    bn, n_rows, _ = p_k.shape
    kc = FOX_KC
    n_past = n_rows // kc
    n_chunks = n_past + 1
    lp = n_chunks * kc
    past = lambda w: pl.BlockSpec((1, kc, w), lambda b, c: (b, jnp.minimum(c, n_past - 1), 0))
    new = lambda w: pl.BlockSpec((1, kc, w), lambda b, c: (b, 0, 0))
    out = lambda w: pl.BlockSpec((1, kc, w), lambda b, c: (b, c, 0))
    return pl.pallas_call(
        functools.partial(_cache_layout_kernel, n_past=n_past),
        grid=(bn, n_chunks),
        in_specs=[past(HEAD_DIM_A), past(HEAD_DIM_A), past(IDX_DIM), past(512), past(512),
                  new(LANES), new(LANES), new(512), new(512)],
        out_specs=[out(LANES), out(LANES), out(IDX_DIM),
                   pl.BlockSpec((1, 1, 512, kc), lambda b, c: (b, c, 0, 0)), out(N_HEADS_C * LANES)],
        out_shape=[jax.ShapeDtypeStruct((bn, lp, LANES), BF16), jax.ShapeDtypeStruct((bn, lp, LANES), BF16),
                   jax.ShapeDtypeStruct((bn, lp, IDX_DIM), BF16),
                   jax.ShapeDtypeStruct((bn, n_chunks, 512, kc), BF16),
                   jax.ShapeDtypeStruct((bn, lp, N_HEADS_C * LANES), BF16)],
        compiler_params=_cparams(("arbitrary", "arbitrary")),
        name="cache_layouts",
    )(p_k, p_v, p_ki, p_ck, p_cv, n_kv, n_misc, n_ck, n_cv)


def _s5_disc_kernel(lr_ref, li_ref, ls_ref, br_ref, bi_ref, ar_ref, ai_ref, bbr_ref, bbi_ref):
    lr, li = lr_ref[...], li_ref[...]
    dt = jnp.exp(ls_ref[...])
    mag = jnp.exp(lr * dt)
    ab_re = mag * jnp.cos(li * dt)
    ab_im = mag * jnp.sin(li * dt)
    den = lr * lr + li * li
    fr = ((ab_re - 1.0) * lr + ab_im * li) / den
    fi = (ab_im * lr - (ab_re - 1.0) * li) / den
    ar_ref[...] = ab_re
    ai_ref[...] = ab_im
    br, bi = br_ref[...], bi_ref[...]
    bbr_ref[...] = fr[:, None, :] * br - fi[:, None, :] * bi
    bbi_ref[...] = fr[:, None, :] * bi + fi[:, None, :] * br


def _s5_discretize(lam_re, lam_im, log_step, b_re, b_im):
    g, p, gs = b_re.shape
    shp = lambda *s: jax.ShapeDtypeStruct(s, F32)
    return pl.pallas_call(
        _s5_disc_kernel,
        out_shape=[shp(g, p), shp(g, p), shp(g, gs, p), shp(g, gs, p)],
        name="s5_disc",
    )(lam_re, lam_im, log_step.reshape(g, 1), jnp.swapaxes(b_re, 1, 2), jnp.swapaxes(b_im, 1, 2))


def _block_diag(blocks):
    g, r, c = blocks.shape
    eye = jnp.eye(g, dtype=bool)
    return jnp.where(eye[:, None, :, None], blocks[:, :, None, :], 0.0).reshape(g * r, g * c)


def _s5_kernel(u_ref, h0r_ref, h0i_ref, ar_ref, ai_ref, bd_ref, cr_ref, ci_ref, d_ref,
               y_ref, hr_ref, hi_ref, bur_ref, bui_ref, *, tc, bn):
    rows = tc * bn

    @pl.when(pl.program_id(0) == 0)
    def _():
        hr_ref[...] = h0r_ref[...]
        hi_ref[...] = h0i_ref[...]

    u = u_ref[...].reshape(rows, SSM_WIDTH)
    ub = u.astype(BF16)
    n_slab = N_GROUPS // S5_SLAB_GROUPS
    cw = S5_SLAB_GROUPS * GROUP_SIZE
    sw = S5_SLAB_GROUPS * STATE_DIM
    for k in range(n_slab):
        uk = ub[:, k * cw:(k + 1) * cw]
        bur_ref[:, k * sw:(k + 1) * sw] = _mm(uk, bd_ref[k * cw:(k + 1) * cw, k * sw:(k + 1) * sw])
        bui_ref[:, k * sw:(k + 1) * sw] = _mm(
            uk, bd_ref[k * cw:(k + 1) * cw, SSM_STATES + k * sw:SSM_STATES + (k + 1) * sw])

    for cc in range(SSM_STATES // S5_CW):
        sl = slice(cc * S5_CW, (cc + 1) * S5_CW)
        ar = jnp.broadcast_to(ar_ref[:, sl], (bn, S5_CW))
        ai = jnp.broadcast_to(ai_ref[:, sl], (bn, S5_CW))

        def step(t, carry, sl=sl, ar=ar, ai=ai):
            hr, hi = carry
            r0 = pl.multiple_of(t * bn, bn)
            nr = ar * hr - ai * hi + bur_ref[pl.ds(r0, bn), sl]
            ni = ar * hi + ai * hr + bui_ref[pl.ds(r0, bn), sl]
            bur_ref[pl.ds(r0, bn), sl] = nr
            bui_ref[pl.ds(r0, bn), sl] = ni
            return nr, ni

        hr, hi = lax.fori_loop(0, tc, step, (hr_ref[:, sl], hi_ref[:, sl]))
        hr_ref[:, sl] = hr
        hi_ref[:, sl] = hi

    for k in range(n_slab):
        hr = bur_ref[:, k * sw:(k + 1) * sw].astype(BF16)
        hi = bui_ref[:, k * sw:(k + 1) * sw].astype(BF16)
        yk = (_mm(hr, cr_ref[k * sw:(k + 1) * sw, k * cw:(k + 1) * cw])
              - _mm(hi, ci_ref[k * sw:(k + 1) * sw, k * cw:(k + 1) * cw])
              + d_ref[:, k * cw:(k + 1) * cw] * u[:, k * cw:(k + 1) * cw])
        y_ref[:, :, k * cw:(k + 1) * cw] = yk.reshape(tc, bn, cw)


def _s5(u_tm, h0_re, h0_im, ab_re, ab_im, bd, cr, ci, d_skip):
    n_time, bn, _ = u_tm.shape
    tc = min(S5_ROWS // bn, n_time)
    const = lambda *s: pl.BlockSpec(s, lambda i: (0,) * len(s))
    y, hr, hi = pl.pallas_call(
        functools.partial(_s5_kernel, tc=tc, bn=bn),
        grid=(n_time // tc,),
        in_specs=[pl.BlockSpec((tc, bn, SSM_WIDTH), lambda i: (i, 0, 0)),
                  const(bn, SSM_STATES), const(bn, SSM_STATES),
                  const(1, SSM_STATES), const(1, SSM_STATES),
                  const(SSM_WIDTH, 2 * SSM_STATES),
                  const(SSM_STATES, SSM_WIDTH), const(SSM_STATES, SSM_WIDTH),
                  const(1, SSM_WIDTH)],
        out_specs=[pl.BlockSpec((tc, bn, SSM_WIDTH), lambda i: (i, 0, 0)),
                   const(bn, SSM_STATES), const(bn, SSM_STATES)],
        out_shape=[jax.ShapeDtypeStruct((n_time, bn, SSM_WIDTH), F32),
                   jax.ShapeDtypeStruct((bn, SSM_STATES), F32),
                   jax.ShapeDtypeStruct((bn, SSM_STATES), F32)],
        scratch_shapes=[pltpu.VMEM((tc * bn, SSM_STATES), F32), pltpu.VMEM((tc * bn, SSM_STATES), F32)],
        compiler_params=_cparams(("arbitrary",)),
        name="s5",
    )(u_tm, h0_re, h0_im, ab_re, ab_im, bd, cr, ci, d_skip)
    return y, hr, hi


def _t5_bucket(rel):
    half = NUM_BUCKETS // 2
    max_exact = half // 2
    n = jnp.abs(rel)
    large = max_exact + (jnp.log(jnp.maximum(n, 1).astype(F32) / max_exact)
                         / math.log(MAX_DISTANCE / max_exact) * (half - max_exact)).astype(I32)
    large = jnp.minimum(large, half - 1)
    return jnp.where(rel > 0, half, 0) + jnp.where(n < max_exact, n, large)


DSA_NEAR_TILES = 3
DSA_SPLIT = 3
MASK_OFF = -(2.0 ** 100)


def _split3(x):
    parts, rem = [], x
    for _ in range(DSA_SPLIT):
        p = rem.astype(BF16).astype(F32)
        parts.append(p)
        rem = rem - p
    return parts


def _dsa_bias_consts(t5_table):
    q = jnp.arange(DSA_TQ, dtype=I32)[:, None]
    k = jnp.arange(DSA_KC, dtype=I32)[None, :]
    def lookup(bucket):
        out = jnp.zeros((N_HEADS_A,) + bucket.shape, F32)
        for b in range(NUM_BUCKETS):
            out = out + jnp.where(bucket[None] == b, t5_table[b].astype(F32).reshape((-1,) + (1,) * bucket.ndim), 0.0)
        return out

    far = lookup(_t5_bucket(jnp.full((), -(1 << 20), I32)))
    near = []
    for e in range(DSA_NEAR_TILES):
        near.append(lookup(_t5_bucket(k - q - LANES * e)) - far[:, None, None])
    near.append(jnp.zeros_like(near[0]))
    near = jnp.stack(near).reshape(DSA_NEAR_TILES + 1, N_HEADS_A * DSA_TQ, DSA_KC)
    lane = jnp.arange(LANES)[None, None, :]
    left = jnp.zeros((N_HEADS_A, DSA_TQ, LANES), F32)
    for i, part in enumerate(_split3(far)):
        left = jnp.where(lane == HEAD_DIM_A + i, part[:, None, None], left)
    eye = jnp.broadcast_to(jnp.eye(DSA_TQ, dtype=F32)[None], (N_HEADS_A, DSA_TQ, DSA_TQ))
    lhs_static = jnp.concatenate([left, eye], axis=-1).reshape(N_HEADS_A * DSA_TQ, 2 * LANES).astype(BF16)
    return near, lhs_static


def _sort_key(s):
    bits = lax.bitcast_convert_type(s, I32)
    return bits ^ (lax.shift_right_arithmetic(bits, 31) & 0x7FFFFFFF)


def _dsa_kernel(qa_ref, qit_ref, wt_ref, kx_ref, vx_ref, ki_ref, near_ref, lhs_ref, o_ref,
                skey_ref, mask_ref, thr_ref, tie_ref, lhs_ref2, s_ref, p_ref, m_ref, acc_ref,
                *, q_off, kv_len, n_top, idx_bits, chunk_counts):
    tq, kc = DSA_TQ, DSA_KC
    qb = pl.program_id(1)
    q_base = q_off + qb * tq
    adm_end = jnp.minimum(((q_base + tq - 1) // CHUNK + 1) * CHUNK, kv_len)
    n_c = (adm_end + kc - 1) // kc

    q_pos_t = q_base + lax.broadcasted_iota(I32, (kc, tq), 1)
    k_loc_t = lax.broadcasted_iota(I32, (kc, tq), 0)

    qit = qit_ref[...].astype(BF16)
    w = wt_ref[...]
    rhs_pairs = [jnp.concatenate([qit[(2 * j) * IDX_DIM:(2 * j + 1) * IDX_DIM, :],
                                  qit[(2 * j + 1) * IDX_DIM:(2 * j + 2) * IDX_DIM, :]], axis=1)
                 for j in range(N_IDX_HEADS // 2)]

    def score_chunk(c, _):
        r0 = pl.multiple_of(c * kc, kc)
        kic = ki_ref[0, pl.ds(r0, kc), :]
        acc = jnp.zeros((kc, tq), F32)
        for j in range(N_IDX_HEADS // 2):
            d = jnp.maximum(_mm(kic, rhs_pairs[j]), 0.0)
            acc = acc + w[2 * j:2 * j + 1, :] * d[:, :tq] + w[2 * j + 1:2 * j + 2, :] * d[:, tq:]
        acc = jnp.where(acc == 0.0, 0.0, acc)
        k_pos = r0 + k_loc_t
        adm = ((k_pos // CHUNK) <= (q_pos_t // CHUNK)) & (k_pos < kv_len)
        skey_ref[pl.ds(r0, kc), :] = _sort_key(jnp.where(adm, acc, -jnp.inf))
        return 0

    lax.fori_loop(0, n_c, score_chunk, 0)

    def fold(ind):
        return jnp.sum(ind.reshape(kc // SUBLANES, SUBLANES, tq), axis=0)

    for nc in chunk_counts:
        @pl.when(n_c == nc)
        def _(nc=nc):
            def value_bit(i, t):
                cand = t ^ lax.shift_left(jnp.int32(1), 31 - i)
                part = jnp.zeros((SUBLANES, tq), F32)
                for c in range(nc):
                    part = part + fold(jnp.where(skey_ref[c * kc:(c + 1) * kc, :] >= cand, 1.0, 0.0))
                cnt = jnp.sum(part, axis=0, keepdims=True)
                return jnp.where(cnt >= n_top, cand, t)
            thr_ref[...] = lax.fori_loop(0, 32, value_bit, jnp.full((1, tq), INT_MIN, I32))

    thr = thr_ref[...]

    def count(pred_fn):
        def body(c, part):
            r0 = pl.multiple_of(c * kc, kc)
            return part + fold(jnp.where(pred_fn(skey_ref[pl.ds(r0, kc), :], r0 + k_loc_t), 1.0, 0.0))
        return jnp.sum(lax.fori_loop(0, n_c, body, jnp.zeros((SUBLANES, tq), F32)), axis=0, keepdims=True)

    cnt_gt = count(lambda sk, _: sk > thr)
    cnt_ge = count(lambda sk, _: sk >= thr)
    need = n_top - cnt_gt

    tie_ref[...] = jnp.full((1, tq), 1 << 30, I32)
    has_tie = (cnt_ge > n_top) & (thr > SKEY_NEG_INF)

    @pl.when(jnp.max(jnp.where(has_tie, 1.0, 0.0)) > 0.5)
    def _():
        def index_bit(i, j):
            cand = j | lax.shift_left(jnp.int32(1), idx_bits - 1 - i)
            cnt = count(lambda sk, kp: (sk == thr) & (kp < cand))
            return jnp.where(cnt <= need - 1.0, cand, j)
        tie_ref[...] = lax.fori_loop(0, idx_bits, index_bit, jnp.zeros((1, tq), I32))

    tie_idx = tie_ref[...]

    def mask_chunk(c, _):
        r0 = pl.multiple_of(c * kc, kc)
        sk = skey_ref[pl.ds(r0, kc), :]
        sel = ((sk > thr) | ((sk == thr) & (r0 + k_loc_t <= tie_idx))) & (sk > SKEY_NEG_INF)
        mask_ref[pl.ds(r0, kc), :] = jnp.where(sel, 0.0, MASK_OFF).astype(BF16)
        return 0

    lax.fori_loop(0, n_c, mask_chunk, 0)

    lane = lax.broadcasted_iota(I32, (tq, LANES), 1)
    q_rows = []
    for h in range(N_HEADS_A):
        pair = qa_ref[:, (h // 2) * LANES:(h // 2 + 1) * LANES] * (HEAD_DIM_A ** -0.5)
        if h % 2:
            pair = pltpu.roll(pair, HEAD_DIM_A, axis=1)
        q_rows.append(pair)
    q_left = jnp.concatenate(q_rows, axis=0).astype(BF16)
    lane_all = lax.broadcasted_iota(I32, (N_HEADS_A * tq, LANES), 1)
    lhs = jnp.concatenate([jnp.where(lane_all < HEAD_DIM_A, q_left, lhs_ref[:, :LANES]), lhs_ref[:, LANES:]],
                          axis=1)

    def logits_into(slot, c, near):
        r0 = pl.multiple_of(c * kc, kc)
        rhs = jnp.concatenate([kx_ref[0, pl.ds(r0, kc), :], mask_ref[pl.ds(r0, kc), :]], axis=1)
        s = _mm_nt(lhs_ref2[...], rhs)
        if near:
            s = s + near_ref[jnp.clip(q_base // LANES - c * (kc // LANES), 0, DSA_NEAR_TILES)]
        s_ref[slot] = s

    lhs_ref2[...] = lhs
    m_ref[...] = jnp.full(m_ref.shape, -jnp.inf, F32)
    acc_ref[...] = jnp.zeros(acc_ref.shape, F32)
    n_far = jnp.clip((q_base // LANES - DSA_NEAR_TILES + 2) // (kc // LANES), 0, n_c)

    def attend(c, near_next):
        cur = c % 2
        r0 = pl.multiple_of(c * kc, kc)
        for h in range(N_HEADS_A):
            rs = slice(h * tq, (h + 1) * tq)
            s = s_ref[cur, rs, :]
            m_old = m_ref[rs, :]
            m_new = jnp.maximum(m_old, jnp.broadcast_to(jnp.max(s, axis=1, keepdims=True), (tq, LANES)))
            p_ref[rs, :] = jnp.exp(s - jnp.concatenate([m_new] * (kc // LANES), axis=1)).astype(BF16)
            acc_ref[rs, :] = jnp.exp(m_old - m_new) * acc_ref[rs, :]
            m_ref[rs, :] = m_new
        acc_ref[...] += _mm(p_ref[...], vx_ref[0, pl.ds(r0, kc), :])
        logits_into(1 - cur, jnp.minimum(c + 1, n_c - 1), near_next)

    logits_into(0, 0, True)
    split = jnp.maximum(n_far - 1, 0)
    lax.fori_loop(0, split, lambda c, cr: (attend(c, False), cr)[1], 0)
    lax.fori_loop(split, n_c, lambda c, cr: (attend(c, True), cr)[1], 0)
    acc = acc_ref[...]
    out = acc / acc[:, HEAD_DIM_A:HEAD_DIM_A + 1]
    for j in range(N_HEADS_A // 2):
        even = out[(2 * j) * tq:(2 * j + 1) * tq]
        odd = pltpu.roll(out[(2 * j + 1) * tq:(2 * j + 2) * tq], HEAD_DIM_A, axis=1)
        o_ref[:, j * LANES:(j + 1) * LANES] = jnp.where(lane < HEAD_DIM_A, even, odd)


def _dsa(qa, qit, wt, kx, vx, ki, near, lhs_static, *, q_off, kv_len, n_top):
    bn, tq_total, _ = qa.shape
    lp = kx.shape[1]
    nq = tq_total // DSA_TQ
    idx_bits = max(1, int(lp - 1).bit_length())
    counts = sorted({-(-min(((q_off + (qb + 1) * DSA_TQ - 1) // CHUNK + 1) * CHUNK, kv_len) // DSA_KC)
                     for qb in range(nq)})

    def qside(a):
        if a.ndim == 3:
            return pl.BlockSpec((None, a.shape[1], DSA_TQ), lambda b, q: (b, 0, q))
        return pl.BlockSpec((a.shape[0], DSA_TQ), lambda b, q: (0, b * nq + q))

    return pl.pallas_call(
        functools.partial(_dsa_kernel, q_off=q_off, kv_len=kv_len, n_top=float(n_top), idx_bits=idx_bits,
                          chunk_counts=tuple(counts)),
        grid=(bn, nq),
        in_specs=[pl.BlockSpec((None, DSA_TQ, N_HEADS_A * HEAD_DIM_A), lambda b, q: (b, q, 0)),
                  qside(qit), qside(wt),
                  pl.BlockSpec((1, lp, LANES), lambda b, q: (b, 0, 0)),
                  pl.BlockSpec((1, lp, LANES), lambda b, q: (b, 0, 0)),
                  pl.BlockSpec((1, lp, IDX_DIM), lambda b, q: (b, 0, 0)),
                  pl.BlockSpec((DSA_NEAR_TILES + 1, N_HEADS_A * DSA_TQ, DSA_KC), lambda b, q: (0, 0, 0)),
                  pl.BlockSpec((N_HEADS_A * DSA_TQ, 2 * LANES), lambda b, q: (0, 0))],
        out_specs=pl.BlockSpec((None, DSA_TQ, N_HEADS_A * HEAD_DIM_A), lambda b, q: (b, q, 0)),
        out_shape=jax.ShapeDtypeStruct((bn, tq_total, N_HEADS_A * HEAD_DIM_A), F32),
        scratch_shapes=[pltpu.VMEM((lp, DSA_TQ), I32), pltpu.VMEM((lp, DSA_TQ), BF16),
                        pltpu.VMEM((1, DSA_TQ), I32), pltpu.VMEM((1, DSA_TQ), I32),
                        pltpu.VMEM((N_HEADS_A * DSA_TQ, 2 * LANES), BF16),
                        pltpu.VMEM((2, N_HEADS_A * DSA_TQ, DSA_KC), F32),
                        pltpu.VMEM((N_HEADS_A * DSA_TQ, DSA_KC), BF16),
                        pltpu.VMEM((N_HEADS_A * DSA_TQ, LANES), F32),
                        pltpu.VMEM((N_HEADS_A * DSA_TQ, LANES), F32)],
        compiler_params=_cparams(("arbitrary", "arbitrary")),
        name="dsa",
    )(qa, qit, wt, kx, vx, ki, near, lhs_static)


def _logf_kernel(cf_ref, b_ref, o_ref):
    o_ref[...] = jax.nn.log_sigmoid(cf_ref[...] + b_ref[...])


def _log_forget(cf2d, b_forget):
    m = cf2d.shape[0]
    tm = min(2048, m)
    return pl.pallas_call(
        _logf_kernel,
        grid=(m // tm,),
        in_specs=[pl.BlockSpec((tm, N_HEADS_C), lambda i: (i, 0)), pl.BlockSpec((1, N_HEADS_C), lambda i: (0, 0))],
        out_specs=pl.BlockSpec((tm, N_HEADS_C), lambda i: (i, 0)),
        out_shape=jax.ShapeDtypeStruct((m, N_HEADS_C), F32),
        compiler_params=_cparams(("arbitrary",)),
        name="logf",
    )(cf2d, b_forget.reshape(1, N_HEADS_C))


FOX_EXT_ROWS = 16
FOX_SPLIT = 3


def _split_bf16(x):
    parts = []
    rem = x
    for _ in range(FOX_SPLIT):
        p = rem.astype(BF16).astype(F32)
        parts.append(p)
        rem = rem - p
    return parts


def _cumsum_kernel(lf_ref, dc_ref, kext_ref, carry_ref):
    tm = lf_ref.shape[1]

    @pl.when(pl.program_id(1) == 0)
    def _():
        carry_ref[...] = jnp.zeros_like(carry_ref)

    lf = lf_ref[0]
    tri = (lax.broadcasted_iota(I32, (tm, tm), 1) <= lax.broadcasted_iota(I32, (tm, tm), 0)).astype(F32)
    dc = jnp.dot(tri, lf, preferred_element_type=F32, precision=lax.Precision.HIGHEST) + carry_ref[...]
    dc_ref[0] = dc
    carry_ref[...] = dc[tm - 1:tm, :]
    eye = (lax.broadcasted_iota(I32, (N_HEADS_C, N_HEADS_C), 0)
           == lax.broadcasted_iota(I32, (N_HEADS_C, N_HEADS_C), 1)).astype(F32)
    dct = lax.dot_general(eye, dc, (((1,), (1,)), ((), ())), preferred_element_type=F32,
                          precision=lax.Precision.HIGHEST)
    neg = _split_bf16(-dct)
    row = lax.broadcasted_iota(I32, (FOX_EXT_ROWS, tm), 0)
    for h in range(N_HEADS_C):
        tile = jnp.where(row < FOX_SPLIT, 1.0, 0.0)
        for i in range(FOX_SPLIT):
            tile = jnp.where(row == FOX_SPLIT + i, neg[i][h:h + 1, :], tile)
        kext_ref[0, 0, h] = tile.astype(BF16)


def _forget_cumsum(logf_all, kc):
    bn, lp, _ = logf_all.shape
    return pl.pallas_call(
        _cumsum_kernel,
        grid=(bn, lp // kc),
        in_specs=[pl.BlockSpec((1, kc, N_HEADS_C), lambda b, t: (b, t, 0))],
        out_specs=[pl.BlockSpec((1, kc, N_HEADS_C), lambda b, t: (b, t, 0)),
                   pl.BlockSpec((1, 1, N_HEADS_C, FOX_EXT_ROWS, kc), lambda b, t: (b, t, 0, 0, 0))],
        out_shape=[jax.ShapeDtypeStruct((bn, lp, N_HEADS_C), F32),
                   jax.ShapeDtypeStruct((bn, lp // kc, N_HEADS_C, FOX_EXT_ROWS, kc), BF16)],
        scratch_shapes=[pltpu.VMEM((1, N_HEADS_C), F32)],
        compiler_params=_cparams(("arbitrary", "arbitrary")),
        name="forget_cumsum",
    )(logf_all)


def _fox_kernel(q_ref, dq_ref, kt_ref, kext_ref, vx_ref, o_ref, qx_ref, m_ref, acc_ref, *, tq, kc, q_off):
    qb = pl.program_id(1)
    q_base = q_off + qb * tq
    n_c = (q_base + tq + kc - 1) // kc
    n_full = (q_base + 1) // kc
    lane = lax.broadcasted_iota(I32, (tq, LANES), 1)

    dq = dq_ref[0]
    for h in range(N_HEADS_C):
        pair = q_ref[:, (h // 2) * LANES:(h // 2 + 1) * LANES] * (HEAD_DIM_C ** -0.5)
        if h % 2:
            pair = pltpu.roll(pair, HEAD_DIM_C, axis=1)
        ext = jnp.where((lane >= HEAD_DIM_C + FOX_SPLIT) & (lane < HEAD_DIM_C + 2 * FOX_SPLIT), 1.0, 0.0)
        for i, part in enumerate(_split_bf16(dq[:, h:h + 1])):
            ext = jnp.where(lane == HEAD_DIM_C + i, part, ext)
        qx_ref[h] = jnp.where(lane < HEAD_DIM_C, pair, ext).astype(BF16)
    m_ref[...] = jnp.full(m_ref.shape, -jnp.inf, F32)
    acc_ref[...] = jnp.zeros(acc_ref.shape, F32)

    q_pos = q_base + lax.broadcasted_iota(I32, (tq, kc), 0)
    k_loc = lax.broadcasted_iota(I32, (tq, kc), 1)
    zrows = jnp.zeros((LANES - HEAD_DIM_C - FOX_EXT_ROWS, kc), BF16)

    def chunk(c, masked):
        r0 = pl.multiple_of(c * kc, kc)
        for h in range(N_HEADS_C):
            kx = jnp.concatenate([kt_ref[0, c, h * HEAD_DIM_C:(h + 1) * HEAD_DIM_C, :], kext_ref[0, c, h], zrows],
                                 axis=0)
            s = _mm(qx_ref[h], kx)
            if masked:
                s = jnp.where(r0 + k_loc <= q_pos, s, -jnp.inf)
            m_old = m_ref[h]
            m_new = jnp.maximum(m_old, jnp.broadcast_to(jnp.max(s, axis=1, keepdims=True), (tq, LANES)))
            p = jnp.exp(s - jnp.concatenate([m_new] * (kc // LANES), axis=1))
            pv = _mm(p.astype(BF16), vx_ref[0, pl.ds(r0, kc), h * LANES:(h + 1) * LANES])
            acc_ref[h] = jnp.exp(m_old - m_new) * acc_ref[h] + pv
            m_ref[h] = m_new

    def full_body(c, carry):
        chunk(c, False)
        return carry

    def diag_body(c, carry):
        chunk(c, True)
        return carry

    lax.fori_loop(0, n_full, full_body, 0)
    lax.fori_loop(n_full, n_c, diag_body, 0)

    for j in range(N_HEADS_C // 2):
        even = acc_ref[2 * j] / acc_ref[2 * j][:, HEAD_DIM_C:HEAD_DIM_C + 1]
        odd = acc_ref[2 * j + 1] / acc_ref[2 * j + 1][:, HEAD_DIM_C:HEAD_DIM_C + 1]
        o_ref[:, j * LANES:(j + 1) * LANES] = jnp.where(lane < HEAD_DIM_C, even, pltpu.roll(odd, HEAD_DIM_C, axis=1))


def _fox(cq, dq, kt, kext, vx, *, tq, q_off):
    bn, tq_total, width = cq.shape
    _, n_chunks, _, kc = kt.shape
    lp = n_chunks * kc
    return pl.pallas_call(
        functools.partial(_fox_kernel, tq=tq, kc=kc, q_off=q_off),
        grid=(bn, tq_total // tq),
        in_specs=[pl.BlockSpec((None, tq, width), lambda b, q: (b, q, 0)),
                  pl.BlockSpec((1, tq, N_HEADS_C), lambda b, q: (b, q, 0)),
                  pl.BlockSpec((1, n_chunks, width, kc), lambda b, q: (b, 0, 0, 0)),
                  pl.BlockSpec((1, n_chunks, N_HEADS_C, FOX_EXT_ROWS, kc), lambda b, q: (b, 0, 0, 0, 0)),
                  pl.BlockSpec((1, lp, N_HEADS_C * LANES), lambda b, q: (b, 0, 0))],
        out_specs=pl.BlockSpec((None, tq, width), lambda b, q: (b, q, 0)),
        out_shape=jax.ShapeDtypeStruct((bn, tq_total, width), F32),
        scratch_shapes=[pltpu.VMEM((N_HEADS_C, tq, LANES), BF16),
                        pltpu.VMEM((N_HEADS_C, tq, LANES), F32),
                        pltpu.VMEM((N_HEADS_C, tq, LANES), F32)],
        compiler_params=_cparams(("arbitrary", "arbitrary")),
        name="fox",
    )(cq, dq, kt, kext, vx)


def _merge_kernel(x_ref, ya_ref, yb_ref, yc_ref, g_ref, wa_ref, wb_ref, wc_ref, wo_ref, lng_ref, lnb_ref,
                  o_ref, *, alpha):
    ba = _mm(ya_ref[...].astype(BF16), wa_ref[...])
    bc = _mm(yc_ref[...].astype(BF16), wc_ref[...])
    glu = _mm(jax.nn.gelu(yb_ref[...]).astype(BF16), wb_ref[...])
    bb = glu[:, :D_MODEL] * jax.nn.sigmoid(glu[:, D_MODEL:])
    g = jax.nn.sigmoid(g_ref[...])
    merged = g[:, :D_MODEL] * ba + g[:, D_MODEL:2 * D_MODEL] * bb + g[:, 2 * D_MODEL:] * bc
    out = _mm(merged.astype(BF16), wo_ref[...])
    o_ref[...] = _layer_norm(alpha * x_ref[...] + out, lng_ref[...], lnb_ref[...])


def _merge(x2d, ya, yb, yb_time_major, yc, gates, wa, wb, wc, wo, ln_g, ln_b, alpha, n_time):
    m = x2d.shape[0]
    tm = min(OUT_TM, m)
    row = lambda w: pl.BlockSpec((tm, w), lambda i: (i, 0))
    const = lambda *s: pl.BlockSpec(s, lambda i: (0,) * len(s))
    if yb_time_major:
        n_t = n_time // tm
        yb_spec = pl.BlockSpec((tm, SSM_WIDTH), lambda i: (i % n_t, i // n_t))
    else:
        yb_spec = row(SSM_WIDTH)
    return pl.pallas_call(
        functools.partial(_merge_kernel, alpha=alpha),
        grid=(m // tm,),
        in_specs=[row(D_MODEL), row(512), yb_spec, row(512), row(N_BRANCH * D_MODEL),
                  const(512, D_MODEL), const(SSM_WIDTH, 2 * D_MODEL), const(512, D_MODEL),
                  const(D_MODEL, D_MODEL), const(1, D_MODEL), const(1, D_MODEL)],
        out_specs=row(D_MODEL),
        out_shape=jax.ShapeDtypeStruct((m, D_MODEL), F32),
        compiler_params=_cparams(("arbitrary",)),
        name="merge",
    )(x2d, ya, yb, yc, gates, wa, wb, wc, wo, ln_g.reshape(1, -1), ln_b.reshape(1, -1))


def _ffn_kernel(x_ref, wg_ref, wu_ref, wd_ref, lng_ref, lnb_ref, o_ref, acc_ref, *, alpha):
    c = pl.program_id(1)

    @pl.when(c == 0)
    def _():
        acc_ref[...] = jnp.zeros_like(acc_ref)

    xb = x_ref[...].astype(BF16)
    a = jax.nn.silu(_mm(xb, wg_ref[...])) * _mm(xb, wu_ref[...])
    acc_ref[...] += _mm(a.astype(BF16), wd_ref[...])

    @pl.when(c == pl.num_programs(1) - 1)
    def _():
        o_ref[...] = _layer_norm(alpha * x_ref[...] + acc_ref[...], lng_ref[...], lnb_ref[...])


def _ffn(x2d, w_gu, w_down, ln_g, ln_b, alpha):
    m = x2d.shape[0]
    tm = min(FFN_TM, m)
    d_ff = w_down.shape[0]
    n_c = d_ff // FFN_FC
    return pl.pallas_call(
        functools.partial(_ffn_kernel, alpha=alpha),
        grid=(m // tm, n_c),
        in_specs=[pl.BlockSpec((tm, D_MODEL), lambda i, c: (i, 0)),
                  pl.BlockSpec((D_MODEL, FFN_FC), lambda i, c: (0, c)),
                  pl.BlockSpec((D_MODEL, FFN_FC), lambda i, c: (0, n_c + c)),
                  pl.BlockSpec((FFN_FC, D_MODEL), lambda i, c: (c, 0)),
                  pl.BlockSpec((1, D_MODEL), lambda i, c: (0, 0)),
                  pl.BlockSpec((1, D_MODEL), lambda i, c: (0, 0))],
        out_specs=pl.BlockSpec((tm, D_MODEL), lambda i, c: (i, 0)),
        out_shape=jax.ShapeDtypeStruct((m, D_MODEL), F32),
        scratch_shapes=[pltpu.VMEM((tm, D_MODEL), F32)],
        compiler_params=_cparams(("arbitrary", "arbitrary")),
        name="ffn",
    )(x2d, w_gu, w_gu, w_down, ln_g.reshape(1, -1), ln_b.reshape(1, -1))


MOE_CAP = 320


def _moe_kernel(x_ref, wrt_ref, brt_ref, tri_ref, wgu_ref, wd_ref, lng_ref, lnb_ref, o_ref,
                acc_ref, ind_ref, pos_ref, gate_ref, *, alpha):
    e = pl.program_id(1)
    tm = x_ref.shape[0]
    xb = x_ref[...].astype(BF16)
    row = lax.broadcasted_iota(I32, (N_EXPERTS, tm), 0).astype(F32)

    @pl.when(e == 0)
    def _():
        acc_ref[...] = jnp.zeros_like(acc_ref)
        logits = _mm_nt(wrt_ref[...], xb) + brt_ref[...]
        m1 = jnp.max(logits, axis=0, keepdims=True)
        i1 = jnp.min(jnp.where(logits == m1, row, float(N_EXPERTS)), axis=0, keepdims=True)
        rest = jnp.where(row == i1, -jnp.inf, logits)
        m2 = jnp.max(rest, axis=0, keepdims=True)
        i2 = jnp.min(jnp.where(rest == m2, row, float(N_EXPERTS)), axis=0, keepdims=True)
        ex = jnp.exp(m2 - m1)
        ind = jnp.where((row == i1) | (row == i2), 1.0, 0.0)
        ind_ref[...] = ind
        gate_ref[...] = jnp.where(row == i1, 1.0 / (1.0 + ex), 0.0) + jnp.where(row == i2, ex / (1.0 + ex), 0.0)
        pos_ref[...] = _mm(ind.astype(BF16), tri_ref[...])

    sel = ind_ref[pl.ds(e, 1), :]
    pos = pos_ref[pl.ds(e, 1), :]
    gate = gate_ref[pl.ds(e, 1), :]
    count = jnp.sum(sel)
    slot = lax.broadcasted_iota(I32, (MOE_CAP, tm), 0).astype(F32)

    def block(j, carry):
        pick = (pos == slot + (j * MOE_CAP).astype(F32)) & (sel > 0.5)
        pick_b = jnp.where(pick, 1.0, 0.0).astype(BF16)
        xe = _mm(pick_b, xb).astype(BF16)
        h = _mm(xe, wgu_ref[0])
        a = jax.nn.silu(h[:, :D_FF_EXPERT]) * h[:, D_FF_EXPERT:]
        ge = jnp.sum(jnp.where(pick, gate, 0.0), axis=1, keepdims=True)
        y = (ge * _mm(a.astype(BF16), wd_ref[0])).astype(BF16)
        acc_ref[...] += lax.dot_general(pick_b, y, (((0,), (0,)), ((), ())), preferred_element_type=F32)
        return carry

    n_blocks = (count.astype(I32) + MOE_CAP - 1) // MOE_CAP
    lax.fori_loop(0, n_blocks, block, 0)

    @pl.when(e == pl.num_programs(1) - 1)
    def _():
        o_ref[...] = _layer_norm(alpha * x_ref[...] + acc_ref[...], lng_ref[...], lnb_ref[...])


def _moe(x2d, w_router, b_router, w_exp_gu, w_exp_down, ln_g, ln_b, alpha):
    m = x2d.shape[0]
    tm = min(MOE_TM, m)
    tri = (jnp.arange(tm)[:, None] < jnp.arange(tm)[None, :]).astype(BF16)
    return pl.pallas_call(
        functools.partial(_moe_kernel, alpha=alpha),
        grid=(m // tm, N_EXPERTS),
        in_specs=[pl.BlockSpec((tm, D_MODEL), lambda i, e: (i, 0)),
                  pl.BlockSpec((N_EXPERTS, D_MODEL), lambda i, e: (0, 0)),
                  pl.BlockSpec((N_EXPERTS, 1), lambda i, e: (0, 0)),
                  pl.BlockSpec((tm, tm), lambda i, e: (0, 0)),
                  pl.BlockSpec((1, D_MODEL, 2 * D_FF_EXPERT), lambda i, e: (e, 0, 0)),
                  pl.BlockSpec((1, D_FF_EXPERT, D_MODEL), lambda i, e: (e, 0, 0)),
                  pl.BlockSpec((1, D_MODEL), lambda i, e: (0, 0)),
                  pl.BlockSpec((1, D_MODEL), lambda i, e: (0, 0))],
        out_specs=pl.BlockSpec((tm, D_MODEL), lambda i, e: (i, 0)),
        out_shape=jax.ShapeDtypeStruct((m, D_MODEL), F32),
        scratch_shapes=[pltpu.VMEM((tm, D_MODEL), F32), pltpu.VMEM((N_EXPERTS, tm), F32),
                        pltpu.VMEM((N_EXPERTS, tm), F32), pltpu.VMEM((N_EXPERTS, tm), F32)],
        compiler_params=_cparams(("arbitrary", "arbitrary")),
        name="moe",
    )(x2d, jnp.swapaxes(w_router, 0, 1), b_router.reshape(-1, 1), tri, w_exp_gu, w_exp_down,
      ln_g.reshape(1, -1), ln_b.reshape(1, -1))


def _pad_rows(a, rows):
    if a.shape[1] == rows:
        return a
    pad = jnp.zeros((a.shape[0], rows - a.shape[1]) + a.shape[2:], a.dtype)
    return jnp.concatenate([a, pad], axis=1)


def _round_up(n, mult):
    return -(-n // mult) * mult


def _token_mixer(x, past, lw):
    bn, n_time, _ = x.shape
    m = bn * n_time
    x2d = x.reshape(m, D_MODEL)
    prompt = past is None
    per_batch = lambda a: a.reshape((bn, -1) + a.shape[1:])

    if prompt:
        past_len = 0
        proj = _project_prompt(x2d, lw["w_in"], lw["b_forget"], bn, n_time, lw["prev_ck"], lw["prev_cv"])
        a_k, a_v, a_ki, logf = [per_batch(proj[n]) for n in ("a_k", "a_v", "a_ki", "logf")]
        c_k, c_v = proj["ck"], proj["cv"]
        logf_all = logf
        h0_re = jnp.zeros((bn, SSM_STATES), F32)
        h0_im = jnp.zeros((bn, SSM_STATES), F32)
        u_tm = proj["u"].reshape(n_time, bn, SSM_WIDTH)
        qa, qit, wt = per_batch(proj["qa"]), proj["qit"], proj["wt"]
        kx, vxa, ki16, vxc = [per_batch(proj[n]) for n in ("kx", "vxa", "ki16", "vxc")]
        kt = proj["kt"].reshape(bn, n_time // FOX_KC, 512, FOX_KC)
    else:
        p_ak, p_av, p_aki, p_hr, p_hi, p_ck, p_cv, p_logf = past
        past_len = p_ak.shape[1]
        proj = _project(x2d, lw["w_in"])
        kv, misc = per_batch(proj["kv"]), per_batch(proj["misc"])
        a_k, a_v = kv[..., :HEAD_DIM_A], kv[..., HEAD_DIM_A:]
        a_ki = misc[..., :IDX_DIM]
        a_w = misc[..., MISC_W_LANE:MISC_W_LANE + N_IDX_HEADS]
        c_f = misc[..., MISC_F_LANE:MISC_F_LANE + N_HEADS_C]
        c_k, c_v = per_batch(proj["ck"]), per_batch(proj["cv"])
        logf = _log_forget(c_f.reshape(m, N_HEADS_C), lw["b_forget"]).reshape(bn, n_time, N_HEADS_C)
        logf_all = jnp.concatenate([p_logf, logf], axis=1)
        h0_re = p_hr.reshape(bn, SSM_STATES)
        h0_im = p_hi.reshape(bn, SSM_STATES)
        u_tm = jnp.swapaxes(per_batch(proj["u"]), 0, 1)
        tq_pad = _round_up(n_time, DSA_TQ)
        qa = _pad_rows(per_batch(proj["qa"]), tq_pad)
        qit = jnp.swapaxes(_pad_rows(per_batch(proj["qi"]), tq_pad), 1, 2)
        wt = jnp.swapaxes(_pad_rows(a_w, tq_pad), 1, 2)
        kx, vxa, ki16, kt, vxc = _cache_layouts(
            p_ak, p_av, p_aki, p_ck.reshape(bn, past_len, -1), p_cv.reshape(bn, past_len, -1),
            _pad_rows(kv, FOX_KC), _pad_rows(misc, FOX_KC), _pad_rows(c_k, FOX_KC), _pad_rows(c_v, FOX_KC))
    kv_len = past_len + n_time
    n_top = min(TOPK_MAX, kv_len // 4)

    y_b, h_re, h_im = _s5(u_tm, h0_re, h0_im, lw["ab_re"], lw["ab_im"], lw["bd"], lw["cr"], lw["ci"], lw["ssm_d"])
    if prompt:
        y_b = y_b.reshape(n_time, bn * SSM_WIDTH)
    else:
        y_b = jnp.swapaxes(y_b, 0, 1).reshape(m, SSM_WIDTH)

    y_a = _dsa(qa, qit, wt, kx, vxa, ki16, lw["bias_near"], lw["bias_lhs"],
               q_off=past_len, kv_len=kv_len, n_top=n_top)[:, :n_time].reshape(m, -1)

    lpc = _round_up(kv_len, FOX_KC)
    dcum, kext = _forget_cumsum(_pad_rows(logf_all, lpc), FOX_KC)
    y_c = _fox(per_batch(proj["cq"]), dcum[:, past_len:kv_len], kt, kext, vxc,
               tq=min(FOX_TQ_PROMPT, n_time), q_off=past_len).reshape(m, -1)

    x1 = _merge(x2d, y_a, y_b, prompt, y_c, proj["gates"], lw["w_a_out"], lw["w_b_glu"], lw["w_c_out"], lw["w_o"],
                lw["ln1_g"], lw["ln1_b"], lw["alpha"], n_time)
    if not prompt:
        c_k = c_k.reshape(bn, n_time, N_HEADS_C, HEAD_DIM_C)
        c_v = c_v.reshape(bn, n_time, N_HEADS_C, HEAD_DIM_C)
    new_state = (a_k, a_v, a_ki, h_re.reshape(bn, N_GROUPS, STATE_DIM), h_im.reshape(bn, N_GROUPS, STATE_DIM),
                 c_k, c_v, logf)
    return x1, new_state


def kernel(x_prompt, x_sample, cache_a_k, cache_a_v, cache_a_kidx, state_ssm_re, state_ssm_im, cache_c_k, cache_c_v, cache_c_logf, w_in, b_forget, ssm_lam_re, ssm_lam_im, ssm_log_step, ssm_b_re, ssm_b_im, ssm_c_re, ssm_c_im, ssm_d, w_a_out, w_b_glu, w_c_out, w_o, ln1_g, ln1_b, ln2_g, ln2_b, t5_table, w_ffn_gu, w_ffn_down, w_router, b_router, w_exp_gu, w_exp_down):
    depth = w_in.shape[0]
    alpha = float((2 * depth) ** 0.25)
    bias_near, bias_lhs = _dsa_bias_consts(t5_table)
    xp, xs = x_prompt, x_sample
    rows_p = [[] for _ in range(8)]
    rows_s = [[] for _ in range(8)]
    prev_ck = prev_cv = None
    for layer in range(depth):
        ab_re, ab_im, bb_re, bb_im = _s5_discretize(ssm_lam_re[layer], ssm_lam_im[layer], ssm_log_step[layer],
                                                    ssm_b_re[layer], ssm_b_im[layer])
        lw = dict(
            w_in=_pack_w_in(w_in[layer]), b_forget=b_forget[layer],
            ab_re=ab_re.reshape(1, SSM_STATES), ab_im=ab_im.reshape(1, SSM_STATES),
            bd=jnp.concatenate([_block_diag(bb_re), _block_diag(bb_im)], axis=1).astype(BF16),
            cr=_block_diag(jnp.swapaxes(ssm_c_re[layer], 1, 2)).astype(BF16),
            ci=_block_diag(jnp.swapaxes(ssm_c_im[layer], 1, 2)).astype(BF16),
            ssm_d=ssm_d[layer].reshape(1, SSM_WIDTH),
            w_a_out=w_a_out[layer].astype(BF16), w_b_glu=w_b_glu[layer].astype(BF16),
            w_c_out=w_c_out[layer].astype(BF16), w_o=w_o[layer].astype(BF16),
            ln1_g=ln1_g[layer], ln1_b=ln1_b[layer], bias_near=bias_near, bias_lhs=bias_lhs, alpha=alpha)
        past = (cache_a_k[layer], cache_a_v[layer], cache_a_kidx[layer], state_ssm_re[layer],
                state_ssm_im[layer], cache_c_k[layer], cache_c_v[layer], cache_c_logf[layer])
        xp1, st_p = _token_mixer(xp, None, dict(lw, prev_ck=prev_ck, prev_cv=prev_cv))
        prev_ck, prev_cv = st_p[5], st_p[6]
        xs1, st_s = _token_mixer(xs, past, lw)
        i = layer // 2
        if layer % 2 == 0:
            wgu, wdn = w_ffn_gu[i].astype(BF16), w_ffn_down[i].astype(BF16)
            xp2 = _ffn(xp1, wgu, wdn, ln2_g[layer], ln2_b[layer], alpha)
            xs2 = _ffn(xs1, wgu, wdn, ln2_g[layer], ln2_b[layer], alpha)
        else:
            wr, wgu, wdn = w_router[i].astype(BF16), w_exp_gu[i].astype(BF16), w_exp_down[i].astype(BF16)
            xp2 = _moe(xp1, wr, b_router[i], wgu, wdn, ln2_g[layer], ln2_b[layer], alpha)
            xs2 = _moe(xs1, wr, b_router[i], wgu, wdn, ln2_g[layer], ln2_b[layer], alpha)
        xp = xp2.reshape(x_prompt.shape)
        xs = xs2.reshape(x_sample.shape)
        for j in range(8):
            rows_p[j].append(st_p[j])
            rows_s[j].append(st_s[j])
    bshape = (depth,) + x_prompt.shape[:2] + (N_HEADS_C, HEAD_DIM_C)
    rows_p[5], rows_p[6] = None, None
    (a_k_p, a_v_p, a_kidx_p, ssm_re_p, ssm_im_p, _, _, c_logf_p) = [None if r is None else jnp.stack(r) for r in rows_p]
    c_k_p, c_v_p = prev_ck.reshape(bshape), prev_cv.reshape(bshape)
    (a_k_s, a_v_s, a_kidx_s, ssm_re_s, ssm_im_s, c_k_s, c_v_s, c_logf_s) = [jnp.stack(r) for r in rows_s]
    return (xp, xs, a_k_p, a_k_s, a_v_p, a_v_s, a_kidx_p, a_kidx_s, ssm_re_p, ssm_re_s,
            ssm_im_p, ssm_im_s, c_k_p, c_k_s, c_v_p, c_v_s, c_logf_p, c_logf_s)
```

```python
import functools
import math

import jax
import jax.numpy as jnp
import numpy as np
from jax import lax
from jax.experimental import pallas as pl
from jax.experimental.pallas import tpu as pltpu

F32 = jnp.float32
BF16 = jnp.bfloat16
I32 = jnp.int32

D_MODEL = 1024
CHUNK = 64
N_HEADS_A = 8
HEAD_DIM_A = 64
N_IDX_HEADS = 8
IDX_DIM = 32
TOPK_MAX = 256
NUM_BUCKETS = 32
MAX_DISTANCE = 128
SSM_WIDTH = 512
GROUP_SIZE = 16
N_GROUPS = SSM_WIDTH // GROUP_SIZE
STATE_DIM = 64
SSM_STATES = N_GROUPS * STATE_DIM
N_HEADS_C = 8
HEAD_DIM_C = 64
N_BRANCH = 3
D_FF = 2816
N_EXPERTS = 8
D_FF_EXPERT = 1408
LN_EPS = 1e-5
PROJ_SIZES = (512, 64, 64, 256, 32, 8, 512, 512, 512, 512, 8, 3072)

LANES = 128
SUBLANES = 8
VMEM_LIMIT_BYTES = 56 * 1024 * 1024

PROJ_TM = 256
S5_ROWS = 512
S5_CW = 256
S5_SLAB_GROUPS = 8
DSA_TQ = 128
DSA_KC = 256
FOX_TQ_PROMPT = 256
FOX_KC = 256
OUT_TM = 256
FFN_TM = 1024
FFN_FC = 1408
MOE_TM = 1024

INT_MIN = -(2 ** 31)
SKEY_NEG_INF = -2139095041


def _cparams(sem):
    return pltpu.CompilerParams(dimension_semantics=sem, vmem_limit_bytes=VMEM_LIMIT_BYTES)


def _mm(a, b):
    return jnp.dot(a, b, preferred_element_type=F32)


def _mm_nt(a, b):
    return lax.dot_general(a, b, (((1,), (1,)), ((), ())), preferred_element_type=F32)


def _layer_norm(z, g, b):
    mu = jnp.mean(z, axis=-1, keepdims=True)
    zc = z - mu
    var = jnp.mean(zc * zc, axis=-1, keepdims=True)
    return zc * lax.rsqrt(var + LN_EPS) * g + b


def _pack_w_in(w):
    offs = np.cumsum((0,) + PROJ_SIZES)
    a_q, a_k, a_v, a_qi, a_ki, a_w, b_u, c_q, c_k, c_v, c_f, gates = [
        w[:, offs[i]:offs[i + 1]] for i in range(len(PROJ_SIZES))]
    pad = jnp.zeros((w.shape[0], LANES - IDX_DIM - N_IDX_HEADS - N_HEADS_C), w.dtype)
    return jnp.concatenate([a_q, a_k, a_v, a_qi, a_ki, a_w, c_f, pad, b_u, c_q, c_k, c_v, gates],
                           axis=1).astype(BF16)


PROJ_OUT = (("qa", 0, 512), ("kv", 512, 128), ("qi", 640, 256), ("misc", 896, 128), ("u", 1024, 512),
            ("cq", 1536, 512), ("ck", 2048, 512), ("cv", 2560, 512), ("gates", 3072, 3072))
PROJ_COLS_PACKED = 6144


def _proj_kernel(x_ref, w_ref, *out_refs):
    xb = x_ref[...].astype(BF16)
    for (_, lo, width), o_ref in zip(PROJ_OUT, out_refs):
        for c in range(0, width, 512):
            cw = min(512, width - c)
            o_ref[:, c:c + cw] = _mm(xb, w_ref[:, lo + c:lo + c + cw])


def _project(x2d, w_packed):
    m = x2d.shape[0]
    tm = min(PROJ_TM, m)
    outs = pl.pallas_call(
        _proj_kernel,
        grid=(m // tm,),
        in_specs=[pl.BlockSpec((tm, D_MODEL), lambda i: (i, 0)),
                  pl.BlockSpec((D_MODEL, PROJ_COLS_PACKED), lambda i: (0, 0))],
        out_specs=[pl.BlockSpec((tm, width), lambda i: (i, 0)) for _, _, width in PROJ_OUT],
        out_shape=[jax.ShapeDtypeStruct((m, width), F32) for _, _, width in PROJ_OUT],
        compiler_params=_cparams(("arbitrary",)),
        name="proj",
    )(x2d, w_packed)
    return dict(zip([p[0] for p in PROJ_OUT], outs))


MISC_W_LANE = IDX_DIM
MISC_F_LANE = IDX_DIM + N_IDX_HEADS
DSA_K_ONES = 3


def _attn_layouts(kv, misc, ck, cv, kx_ref, vxa_ref, ki16_ref, kt_ref, vxc_ref):
    rows = kv.shape[0]
    lane = lax.broadcasted_iota(I32, (rows, LANES), 1)
    ones_k = jnp.where((lane >= HEAD_DIM_A) & (lane < HEAD_DIM_A + DSA_K_ONES), 1.0, 0.0)
    ones_v = jnp.where(lane == HEAD_DIM_A, 1.0, 0.0)
    kx_ref[...] = jnp.where(lane < HEAD_DIM_A, kv, ones_k).astype(BF16)
    vxa_ref[...] = jnp.where(lane < HEAD_DIM_A, pltpu.roll(kv, HEAD_DIM_A, axis=1), ones_v).astype(BF16)
    ki16_ref[...] = misc[:, :IDX_DIM].astype(BF16)
    kt_ref[...] = ck.T.astype(BF16)
    for h in range(N_HEADS_C):
        pair = cv[:, (h // 2) * LANES:(h // 2 + 1) * LANES]
        if h % 2:
            pair = pltpu.roll(pair, HEAD_DIM_C, axis=1)
        vxc_ref[:, h * LANES:(h + 1) * LANES] = jnp.where(lane < HEAD_DIM_C, pair, ones_v).astype(BF16)


def _proj_prompt_kernel(x_ref, w_ref, bf_ref, *refs, n_prev):
    if n_prev:
        pck_ref, pcv_ref = refs[:2]
        refs = refs[2:]
    (qa_ref, qit_ref, wt_ref, ak_ref, av_ref, aki_ref, logf_ref, u_ref, cq_ref, ck_ref, cv_ref, g_ref,
     kx_ref, vxa_ref, ki16_ref, kt_ref, vxc_ref) = refs
    xb = x_ref[...].astype(BF16)
    cols = {name: (lo, width) for name, lo, width in PROJ_OUT}

    def mm(name, c0=0, cw=None):
        lo, width = cols[name]
        cw = width if cw is None else cw
        return _mm(xb, w_ref[:, lo + c0:lo + c0 + cw])

    qa_ref[...] = mm("qa")
    u_ref[...] = mm("u")
    for c in range(0, cols["gates"][1], 512):
        g_ref[:, c:c + 512] = mm("gates", c, 512)
    small = mm("kv", 0, cols["kv"][1] + cols["qi"][1] + cols["misc"][1])
    kv = small[:, :LANES]
    qi = small[:, LANES:LANES + cols["qi"][1]]
    misc = small[:, LANES + cols["qi"][1]:]
    cqk = mm("cq", 0, cols["cq"][1] + cols["ck"][1])
    cq_ref[...] = cqk[:, :cols["cq"][1]]
    ck = cqk[:, cols["cq"][1]:]
    cv = mm("cv")
    if n_prev:
        ck_ref[:n_prev] = pck_ref[...]
        cv_ref[:n_prev] = pcv_ref[...]
    for h in range(N_HEADS_C):
        pk = ck[:, (h // 2) * LANES:(h // 2 + 1) * LANES]
        pv = cv[:, (h // 2) * LANES:(h // 2 + 1) * LANES]
        if h % 2:
            pk = pltpu.roll(pk, HEAD_DIM_C, axis=1)
            pv = pltpu.roll(pv, HEAD_DIM_C, axis=1)
        ck_ref[n_prev, :, h, :] = pk[:, :HEAD_DIM_C]
        cv_ref[n_prev, :, h, :] = pv[:, :HEAD_DIM_C]
    qit_ref[...] = qi.T
    wt_ref[...] = misc.T[MISC_W_LANE:MISC_W_LANE + N_IDX_HEADS, :]
    ak_ref[...] = kv[:, :HEAD_DIM_A]
    av_ref[...] = pltpu.roll(kv, HEAD_DIM_A, axis=1)[:, :HEAD_DIM_A]
    aki_ref[...] = misc[:, :IDX_DIM]
    lf = jax.nn.log_sigmoid(misc + bf_ref[...])
    logf_ref[...] = pltpu.roll(lf, LANES - MISC_F_LANE, axis=1)[:, :N_HEADS_C]
    _attn_layouts(kv, misc, ck, cv, kx_ref, vxa_ref, ki16_ref, kt_ref.at[0], vxc_ref)


def _project_prompt(x2d, w_packed, b_forget, n_batch, n_time, prev_ck, prev_cv):
    m = x2d.shape[0]
    n_prev = 0 if prev_ck is None else prev_ck.shape[0]
    tm = FOX_KC
    n_t = n_time // tm
    bf = jnp.zeros((1, LANES), F32).at[0, MISC_F_LANE:MISC_F_LANE + N_HEADS_C].set(b_forget)
    row = lambda w, dt=F32: (jax.ShapeDtypeStruct((m, w), dt), pl.BlockSpec((tm, w), lambda i: (i, 0)))
    col = lambda r: (jax.ShapeDtypeStruct((r, m), F32), pl.BlockSpec((r, tm), lambda i: (0, i)))
    head_shape = (N_HEADS_C, HEAD_DIM_C)
    stacked = (jax.ShapeDtypeStruct((n_prev + 1, m) + head_shape, F32),
               pl.BlockSpec((n_prev + 1, tm) + head_shape, lambda i: (0, i, 0, 0)))
    prev_specs = [pl.BlockSpec((n_prev, tm) + head_shape, lambda i: (0, i, 0, 0))] * 2 if n_prev else []
    prev_args = [prev_ck, prev_cv] if n_prev else []
    outs = dict(
        qa=row(512), qit=col(N_IDX_HEADS * IDX_DIM), wt=col(N_IDX_HEADS),
        a_k=row(HEAD_DIM_A), a_v=row(HEAD_DIM_A), a_ki=row(IDX_DIM), logf=row(N_HEADS_C),
        u=(jax.ShapeDtypeStruct((n_time, n_batch * SSM_WIDTH), F32),
           pl.BlockSpec((tm, SSM_WIDTH), lambda i: (i % n_t, i // n_t))),
        cq=row(512), ck=stacked, cv=stacked, gates=row(N_BRANCH * D_MODEL),
        kx=row(LANES, BF16), vxa=row(LANES, BF16), ki16=row(IDX_DIM, BF16),
        kt=(jax.ShapeDtypeStruct((m // tm, 512, tm), BF16), pl.BlockSpec((1, 512, tm), lambda i: (i, 0, 0))),
        vxc=row(N_HEADS_C * LANES, BF16))
    res = pl.pallas_call(
        functools.partial(_proj_prompt_kernel, n_prev=n_prev),
        grid=(m // tm,),
        in_specs=[pl.BlockSpec((tm, D_MODEL), lambda i: (i, 0)),
                  pl.BlockSpec((D_MODEL, PROJ_COLS_PACKED), lambda i: (0, 0)),
                  pl.BlockSpec((1, LANES), lambda i: (0, 0))] + prev_specs,
        out_specs=[v[1] for v in outs.values()],
        out_shape=[v[0] for v in outs.values()],
        compiler_params=_cparams(("arbitrary",)),
        name="proj_prompt",
    )(x2d, w_packed, bf, *prev_args)
    return dict(zip(outs.keys(), res))


def _cache_layout_kernel(pk_ref, pv_ref, pki_ref, pck_ref, pcv_ref, nkv_ref, nmisc_ref, nck_ref, ncv_ref,
                         kx_ref, vxa_ref, ki16_ref, kt_ref, vxc_ref, *, n_past):
    is_new = pl.program_id(1) >= n_past
    rows = nkv_ref.shape[1]
    lane = lax.broadcasted_iota(I32, (rows, LANES), 1)
    zeros_k = jnp.zeros((rows, LANES - HEAD_DIM_A), F32)
    past_kv = jnp.where(lane < HEAD_DIM_A, jnp.concatenate([pk_ref[0], zeros_k], axis=1),
                        pltpu.roll(jnp.concatenate([pv_ref[0], zeros_k], axis=1), HEAD_DIM_A, axis=1))
    past_misc = jnp.concatenate([pki_ref[0], jnp.zeros((rows, LANES - IDX_DIM), F32)], axis=1)
    kv = jnp.where(is_new, nkv_ref[0], past_kv)
    misc = jnp.where(is_new, nmisc_ref[0], past_misc)
    ck = jnp.where(is_new, nck_ref[0], pck_ref[0])
    cv = jnp.where(is_new, ncv_ref[0], pcv_ref[0])
    _attn_layouts(kv, misc, ck, cv, kx_ref.at[0], vxa_ref.at[0], ki16_ref.at[0], kt_ref.at[0, 0], vxc_ref.at[0])


def _cache_layouts(p_k, p_v, p_ki, p_ck, p_cv, n_kv, n_misc, n_ck, n_cv):
    bn, n_rows, _ = p_k.shape
    kc = FOX_KC
    n_past = n_rows // kc
    n_chunks = n_past + 1
    lp = n_chunks * kc
    past = lambda w: pl.BlockSpec((1, kc, w), lambda b, c: (b, jnp.minimum(c, n_past - 1), 0))
    new = lambda w: pl.BlockSpec((1, kc, w), lambda b, c: (b, 0, 0))
    out = lambda w: pl.BlockSpec((1, kc, w), lambda b, c: (b, c, 0))
    return pl.pallas_call(
        functools.partial(_cache_layout_kernel, n_past=n_past),
        grid=(bn, n_chunks),
        in_specs=[past(HEAD_DIM_A), past(HEAD_DIM_A), past(IDX_DIM), past(512), past(512),
                  new(LANES), new(LANES), new(512), new(512)],
        out_specs=[out(LANES), out(LANES), out(IDX_DIM),
                   pl.BlockSpec((1, 1, 512, kc), lambda b, c: (b, c, 0, 0)), out(N_HEADS_C * LANES)],
        out_shape=[jax.ShapeDtypeStruct((bn, lp, LANES), BF16), jax.ShapeDtypeStruct((bn, lp, LANES), BF16),
                   jax.ShapeDtypeStruct((bn, lp, IDX_DIM), BF16),
                   jax.ShapeDtypeStruct((bn, n_chunks, 512, kc), BF16),
                   jax.ShapeDtypeStruct((bn, lp, N_HEADS_C * LANES), BF16)],
        compiler_params=_cparams(("arbitrary", "arbitrary")),
        name="cache_layouts",
    )(p_k, p_v, p_ki, p_ck, p_cv, n_kv, n_misc, n_ck, n_cv)


def _s5_disc_kernel(lr_ref, li_ref, ls_ref, br_ref, bi_ref, ar_ref, ai_ref, bbr_ref, bbi_ref):
    lr, li = lr_ref[...], li_ref[...]
    dt = jnp.exp(ls_ref[...])
    mag = jnp.exp(lr * dt)
    ab_re = mag * jnp.cos(li * dt)
    ab_im = mag * jnp.sin(li * dt)
    den = lr * lr + li * li
    fr = ((ab_re - 1.0) * lr + ab_im * li) / den
    fi = (ab_im * lr - (ab_re - 1.0) * li) / den
    ar_ref[...] = ab_re
    ai_ref[...] = ab_im
    br, bi = br_ref[...], bi_ref[...]
    bbr_ref[...] = fr[:, None, :] * br - fi[:, None, :] * bi
    bbi_ref[...] = fr[:, None, :] * bi + fi[:, None, :] * br


def _s5_discretize(lam_re, lam_im, log_step, b_re, b_im):
    g, p, gs = b_re.shape
    shp = lambda *s: jax.ShapeDtypeStruct(s, F32)
    return pl.pallas_call(
        _s5_disc_kernel,
        out_shape=[shp(g, p), shp(g, p), shp(g, gs, p), shp(g, gs, p)],
        name="s5_disc",
    )(lam_re, lam_im, log_step.reshape(g, 1), jnp.swapaxes(b_re, 1, 2), jnp.swapaxes(b_im, 1, 2))


def _block_diag(blocks):
    g, r, c = blocks.shape
    eye = jnp.eye(g, dtype=bool)
    return jnp.where(eye[:, None, :, None], blocks[:, :, None, :], 0.0).reshape(g * r, g * c)


def _s5_kernel(u_ref, h0r_ref, h0i_ref, ar_ref, ai_ref, bd_ref, cr_ref, ci_ref, d_ref,
               y_ref, hr_ref, hi_ref, bur_ref, bui_ref, *, tc, bn):
    rows = tc * bn

    @pl.when(pl.program_id(0) == 0)
    def _():
        hr_ref[...] = h0r_ref[...]
        hi_ref[...] = h0i_ref[...]

    u = u_ref[...].reshape(rows, SSM_WIDTH)
    ub = u.astype(BF16)
    n_slab = N_GROUPS // S5_SLAB_GROUPS
    cw = S5_SLAB_GROUPS * GROUP_SIZE
    sw = S5_SLAB_GROUPS * STATE_DIM
    for k in range(n_slab):
        uk = ub[:, k * cw:(k + 1) * cw]
        bur_ref[:, k * sw:(k + 1) * sw] = _mm(uk, bd_ref[k * cw:(k + 1) * cw, k * sw:(k + 1) * sw])
        bui_ref[:, k * sw:(k + 1) * sw] = _mm(
            uk, bd_ref[k * cw:(k + 1) * cw, SSM_STATES + k * sw:SSM_STATES + (k + 1) * sw])

    for cc in range(SSM_STATES // S5_CW):
        sl = slice(cc * S5_CW, (cc + 1) * S5_CW)
        ar = jnp.broadcast_to(ar_ref[:, sl], (bn, S5_CW))
        ai = jnp.broadcast_to(ai_ref[:, sl], (bn, S5_CW))

        def step(t, carry, sl=sl, ar=ar, ai=ai):
            hr, hi = carry
            r0 = pl.multiple_of(t * bn, bn)
            nr = ar * hr - ai * hi + bur_ref[pl.ds(r0, bn), sl]
            ni = ar * hi + ai * hr + bui_ref[pl.ds(r0, bn), sl]
            bur_ref[pl.ds(r0, bn), sl] = nr
            bui_ref[pl.ds(r0, bn), sl] = ni
            return nr, ni

        hr, hi = lax.fori_loop(0, tc, step, (hr_ref[:, sl], hi_ref[:, sl]))
        hr_ref[:, sl] = hr
        hi_ref[:, sl] = hi

    for k in range(n_slab):
        hr = bur_ref[:, k * sw:(k + 1) * sw].astype(BF16)
        hi = bui_ref[:, k * sw:(k + 1) * sw].astype(BF16)
        yk = (_mm(hr, cr_ref[k * sw:(k + 1) * sw, k * cw:(k + 1) * cw])
              - _mm(hi, ci_ref[k * sw:(k + 1) * sw, k * cw:(k + 1) * cw])
              + d_ref[:, k * cw:(k + 1) * cw] * u[:, k * cw:(k + 1) * cw])
        y_ref[:, :, k * cw:(k + 1) * cw] = yk.reshape(tc, bn, cw)


def _s5(u_tm, h0_re, h0_im, ab_re, ab_im, bd, cr, ci, d_skip):
    n_time, bn, _ = u_tm.shape
    tc = min(S5_ROWS // bn, n_time)
    const = lambda *s: pl.BlockSpec(s, lambda i: (0,) * len(s))
    y, hr, hi = pl.pallas_call(
        functools.partial(_s5_kernel, tc=tc, bn=bn),
        grid=(n_time // tc,),
        in_specs=[pl.BlockSpec((tc, bn, SSM_WIDTH), lambda i: (i, 0, 0)),
                  const(bn, SSM_STATES), const(bn, SSM_STATES),
                  const(1, SSM_STATES), const(1, SSM_STATES),
                  const(SSM_WIDTH, 2 * SSM_STATES),
                  const(SSM_STATES, SSM_WIDTH), const(SSM_STATES, SSM_WIDTH),
                  const(1, SSM_WIDTH)],
        out_specs=[pl.BlockSpec((tc, bn, SSM_WIDTH), lambda i: (i, 0, 0)),
                   const(bn, SSM_STATES), const(bn, SSM_STATES)],
        out_shape=[jax.ShapeDtypeStruct((n_time, bn, SSM_WIDTH), F32),
                   jax.ShapeDtypeStruct((bn, SSM_STATES), F32),
                   jax.ShapeDtypeStruct((bn, SSM_STATES), F32)],
        scratch_shapes=[pltpu.VMEM((tc * bn, SSM_STATES), F32), pltpu.VMEM((tc * bn, SSM_STATES), F32)],
        compiler_params=_cparams(("arbitrary",)),
        name="s5",
    )(u_tm, h0_re, h0_im, ab_re, ab_im, bd, cr, ci, d_skip)
    return y, hr, hi


def _t5_bucket(rel):
    half = NUM_BUCKETS // 2
    max_exact = half // 2
    n = jnp.abs(rel)
    large = max_exact + (jnp.log(jnp.maximum(n, 1).astype(F32) / max_exact)
                         / math.log(MAX_DISTANCE / max_exact) * (half - max_exact)).astype(I32)
    large = jnp.minimum(large, half - 1)
    return jnp.where(rel > 0, half, 0) + jnp.where(n < max_exact, n, large)


DSA_NEAR_TILES = 3
DSA_SPLIT = 3
MASK_OFF = -(2.0 ** 100)


def _split3(x):
    parts, rem = [], x
    for _ in range(DSA_SPLIT):
        p = rem.astype(BF16).astype(F32)
        parts.append(p)
        rem = rem - p
    return parts


def _dsa_bias_consts(t5_table):
    q = jnp.arange(DSA_TQ, dtype=I32)[:, None]
    k = jnp.arange(DSA_KC, dtype=I32)[None, :]
    def lookup(bucket):
        out = jnp.zeros((N_HEADS_A,) + bucket.shape, F32)
        for b in range(NUM_BUCKETS):
            out = out + jnp.where(bucket[None] == b, t5_table[b].astype(F32).reshape((-1,) + (1,) * bucket.ndim), 0.0)
        return out

    far = lookup(_t5_bucket(jnp.full((), -(1 << 20), I32)))
    near = []
    for e in range(DSA_NEAR_TILES):
        near.append(lookup(_t5_bucket(k - q - LANES * e)) - far[:, None, None])
    near.append(jnp.zeros_like(near[0]))
    near = jnp.stack(near).reshape(DSA_NEAR_TILES + 1, N_HEADS_A * DSA_TQ, DSA_KC)
    lane = jnp.arange(LANES)[None, None, :]
    left = jnp.zeros((N_HEADS_A, DSA_TQ, LANES), F32)
    for i, part in enumerate(_split3(far)):
        left = jnp.where(lane == HEAD_DIM_A + i, part[:, None, None], left)
    eye = jnp.broadcast_to(jnp.eye(DSA_TQ, dtype=F32)[None], (N_HEADS_A, DSA_TQ, DSA_TQ))
    lhs_static = jnp.concatenate([left, eye], axis=-1).reshape(N_HEADS_A * DSA_TQ, 2 * LANES).astype(BF16)
    return near, lhs_static


def _sort_key(s):
    bits = lax.bitcast_convert_type(s, I32)
    return bits ^ (lax.shift_right_arithmetic(bits, 31) & 0x7FFFFFFF)


def _dsa_kernel(qa_ref, qit_ref, wt_ref, kx_ref, vx_ref, ki_ref, near_ref, lhs_ref, o_ref,
                skey_ref, mask_ref, thr_ref, tie_ref, lhs_ref2, s_ref, p_ref, m_ref, acc_ref,
                *, q_off, kv_len, n_top, idx_bits, chunk_counts):
    tq, kc = DSA_TQ, DSA_KC
    qb = pl.program_id(1)
    q_base = q_off + qb * tq
    adm_end = jnp.minimum(((q_base + tq - 1) // CHUNK + 1) * CHUNK, kv_len)
    n_c = (adm_end + kc - 1) // kc

    q_pos_t = q_base + lax.broadcasted_iota(I32, (kc, tq), 1)
    k_loc_t = lax.broadcasted_iota(I32, (kc, tq), 0)

    qit = qit_ref[...].astype(BF16)
    w = wt_ref[...]
    rhs_pairs = [jnp.concatenate([qit[(2 * j) * IDX_DIM:(2 * j + 1) * IDX_DIM, :],
                                  qit[(2 * j + 1) * IDX_DIM:(2 * j + 2) * IDX_DIM, :]], axis=1)
                 for j in range(N_IDX_HEADS // 2)]

    def score_chunk(c, _, masked):
        r0 = pl.multiple_of(c * kc, kc)
        kic = ki_ref[0, pl.ds(r0, kc), :]
        acc = jnp.zeros((kc, tq), F32)
        for j in range(N_IDX_HEADS // 2):
            d = jnp.maximum(_mm(kic, rhs_pairs[j]), 0.0)
            acc = acc + w[2 * j:2 * j + 1, :] * d[:, :tq] + w[2 * j + 1:2 * j + 2, :] * d[:, tq:]
        acc = jnp.where(acc == 0.0, 0.0, acc)
        if masked:
            k_pos = r0 + k_loc_t
            adm = ((k_pos // CHUNK) <= (q_pos_t // CHUNK)) & (k_pos < kv_len)
            acc = jnp.where(adm, acc, -jnp.inf)
        skey_ref[pl.ds(r0, kc), :] = _sort_key(acc)
        return 0

    n_open = jnp.minimum((q_base // CHUNK) * CHUNK, kv_len) // kc
    lax.fori_loop(0, n_open, functools.partial(score_chunk, masked=False), 0)
    lax.fori_loop(n_open, n_c, functools.partial(score_chunk, masked=True), 0)

    def fold(ind):
        return jnp.sum(ind.reshape(kc // SUBLANES, SUBLANES, tq), axis=0)

    for nc in chunk_counts:
        @pl.when(n_c == nc)
        def _(nc=nc):
            def value_bit(i, t):
                cand = t ^ lax.shift_left(jnp.int32(1), 31 - i)
                part = jnp.zeros((SUBLANES, tq), F32)
                for c in range(nc):
                    part = part + fold(jnp.where(skey_ref[c * kc:(c + 1) * kc, :] >= cand, 1.0, 0.0))
                cnt = jnp.sum(part, axis=0, keepdims=True)
                return jnp.where(cnt >= n_top, cand, t)
            thr_ref[...] = lax.fori_loop(0, 32, value_bit, jnp.full((1, tq), INT_MIN, I32))

    thr = thr_ref[...]

    def count(pred_fn):
        def body(c, part):
            r0 = pl.multiple_of(c * kc, kc)
            return part + fold(jnp.where(pred_fn(skey_ref[pl.ds(r0, kc), :], r0 + k_loc_t), 1.0, 0.0))
        return jnp.sum(lax.fori_loop(0, n_c, body, jnp.zeros((SUBLANES, tq), F32)), axis=0, keepdims=True)

    cnt_gt = count(lambda sk, _: sk > thr)
    cnt_ge = count(lambda sk, _: sk >= thr)
    need = n_top - cnt_gt

    tie_ref[...] = jnp.full((1, tq), 1 << 30, I32)
    has_tie = (cnt_ge > n_top) & (thr > SKEY_NEG_INF)

    @pl.when(jnp.max(jnp.where(has_tie, 1.0, 0.0)) > 0.5)
    def _():
        def index_bit(i, j):
            cand = j | lax.shift_left(jnp.int32(1), idx_bits - 1 - i)
            cnt = count(lambda sk, kp: (sk == thr) & (kp < cand))
            return jnp.where(cnt <= need - 1.0, cand, j)
        tie_ref[...] = lax.fori_loop(0, idx_bits, index_bit, jnp.zeros((1, tq), I32))

    tie_idx = tie_ref[...]

    def mask_chunk(c, _):
        r0 = pl.multiple_of(c * kc, kc)
        sk = skey_ref[pl.ds(r0, kc), :]
        sel = ((sk > thr) | ((sk == thr) & (r0 + k_loc_t <= tie_idx))) & (sk > SKEY_NEG_INF)
        mask_ref[pl.ds(r0, kc), :] = jnp.where(sel, 0.0, MASK_OFF).astype(BF16)
        return 0

    lax.fori_loop(0, n_c, mask_chunk, 0)

    lane = lax.broadcasted_iota(I32, (tq, LANES), 1)
    q_rows = []
    for h in range(N_HEADS_A):
        pair = qa_ref[:, (h // 2) * LANES:(h // 2 + 1) * LANES] * (HEAD_DIM_A ** -0.5)
        if h % 2:
            pair = pltpu.roll(pair, HEAD_DIM_A, axis=1)
        q_rows.append(pair)
    q_left = jnp.concatenate(q_rows, axis=0).astype(BF16)
    lane_all = lax.broadcasted_iota(I32, (N_HEADS_A * tq, LANES), 1)
    lhs = jnp.concatenate([jnp.where(lane_all < HEAD_DIM_A, q_left, lhs_ref[:, :LANES]), lhs_ref[:, LANES:]],
                          axis=1)

    def logits_into(slot, c, near):
        r0 = pl.multiple_of(c * kc, kc)
        rhs = jnp.concatenate([kx_ref[0, pl.ds(r0, kc), :], mask_ref[pl.ds(r0, kc), :]], axis=1)
        s = _mm_nt(lhs_ref2[...], rhs)
        if near:
            s = s + near_ref[jnp.clip(q_base // LANES - c * (kc // LANES), 0, DSA_NEAR_TILES)]
        s_ref[slot] = s

    lhs_ref2[...] = lhs
    m_ref[...] = jnp.full(m_ref.shape, -jnp.inf, F32)
    acc_ref[...] = jnp.zeros(acc_ref.shape, F32)
    n_far = jnp.clip((q_base // LANES - DSA_NEAR_TILES + 2) // (kc // LANES), 0, n_c)

    def attend(c, near_next):
        cur = c % 2
        r0 = pl.multiple_of(c * kc, kc)
        for h in range(N_HEADS_A):
            rs = slice(h * tq, (h + 1) * tq)
            s = s_ref[cur, rs, :]
            m_old = m_ref[rs, :]
            m_new = jnp.maximum(m_old, jnp.broadcast_to(jnp.max(s, axis=1, keepdims=True), (tq, LANES)))
            p_ref[rs, :] = jnp.exp(s - jnp.concatenate([m_new] * (kc // LANES), axis=1)).astype(BF16)
            acc_ref[rs, :] = jnp.exp(m_old - m_new) * acc_ref[rs, :]
            m_ref[rs, :] = m_new
        acc_ref[...] += _mm(p_ref[...], vx_ref[0, pl.ds(r0, kc), :])
        logits_into(1 - cur, jnp.minimum(c + 1, n_c - 1), near_next)

    logits_into(0, 0, True)
    split = jnp.maximum(n_far - 1, 0)
    lax.fori_loop(0, split, lambda c, cr: (attend(c, False), cr)[1], 0)
    lax.fori_loop(split, n_c, lambda c, cr: (attend(c, True), cr)[1], 0)
    acc = acc_ref[...]
    out = acc / acc[:, HEAD_DIM_A:HEAD_DIM_A + 1]
    for j in range(N_HEADS_A // 2):
        even = out[(2 * j) * tq:(2 * j + 1) * tq]
        odd = pltpu.roll(out[(2 * j + 1) * tq:(2 * j + 2) * tq], HEAD_DIM_A, axis=1)
        o_ref[:, j * LANES:(j + 1) * LANES] = jnp.where(lane < HEAD_DIM_A, even, odd)


def _dsa(qa, qit, wt, kx, vx, ki, near, lhs_static, *, q_off, kv_len, n_top):
    bn, tq_total, _ = qa.shape
    lp = kx.shape[1]
    nq = tq_total // DSA_TQ
    idx_bits = max(1, int(lp - 1).bit_length())
    counts = sorted({-(-min(((q_off + (qb + 1) * DSA_TQ - 1) // CHUNK + 1) * CHUNK, kv_len) // DSA_KC)
                     for qb in range(nq)})

    def qside(a):
        if a.ndim == 3:
            return pl.BlockSpec((None, a.shape[1], DSA_TQ), lambda b, q: (b, 0, q))
        return pl.BlockSpec((a.shape[0], DSA_TQ), lambda b, q: (0, b * nq + q))

    return pl.pallas_call(
        functools.partial(_dsa_kernel, q_off=q_off, kv_len=kv_len, n_top=float(n_top), idx_bits=idx_bits,
                          chunk_counts=tuple(counts)),
        grid=(bn, nq),
        in_specs=[pl.BlockSpec((None, DSA_TQ, N_HEADS_A * HEAD_DIM_A), lambda b, q: (b, q, 0)),
                  qside(qit), qside(wt),
                  pl.BlockSpec((1, lp, LANES), lambda b, q: (b, 0, 0)),
                  pl.BlockSpec((1, lp, LANES), lambda b, q: (b, 0, 0)),
                  pl.BlockSpec((1, lp, IDX_DIM), lambda b, q: (b, 0, 0)),
                  pl.BlockSpec((DSA_NEAR_TILES + 1, N_HEADS_A * DSA_TQ, DSA_KC), lambda b, q: (0, 0, 0)),
                  pl.BlockSpec((N_HEADS_A * DSA_TQ, 2 * LANES), lambda b, q: (0, 0))],
        out_specs=pl.BlockSpec((None, DSA_TQ, N_HEADS_A * HEAD_DIM_A), lambda b, q: (b, q, 0)),
        out_shape=jax.ShapeDtypeStruct((bn, tq_total, N_HEADS_A * HEAD_DIM_A), F32),
        scratch_shapes=[pltpu.VMEM((lp, DSA_TQ), I32), pltpu.VMEM((lp, DSA_TQ), BF16),
                        pltpu.VMEM((1, DSA_TQ), I32), pltpu.VMEM((1, DSA_TQ), I32),
                        pltpu.VMEM((N_HEADS_A * DSA_TQ, 2 * LANES), BF16),
                        pltpu.VMEM((2, N_HEADS_A * DSA_TQ, DSA_KC), F32),
                        pltpu.VMEM((N_HEADS_A * DSA_TQ, DSA_KC), BF16),
                        pltpu.VMEM((N_HEADS_A * DSA_TQ, LANES), F32),
                        pltpu.VMEM((N_HEADS_A * DSA_TQ, LANES), F32)],
        compiler_params=_cparams(("arbitrary", "arbitrary")),
        name="dsa",
    )(qa, qit, wt, kx, vx, ki, near, lhs_static)


def _logf_kernel(cf_ref, b_ref, o_ref):
    o_ref[...] = jax.nn.log_sigmoid(cf_ref[...] + b_ref[...])


def _log_forget(cf2d, b_forget):
    m = cf2d.shape[0]
    tm = min(2048, m)
    return pl.pallas_call(
        _logf_kernel,
        grid=(m // tm,),
        in_specs=[pl.BlockSpec((tm, N_HEADS_C), lambda i: (i, 0)), pl.BlockSpec((1, N_HEADS_C), lambda i: (0, 0))],
        out_specs=pl.BlockSpec((tm, N_HEADS_C), lambda i: (i, 0)),
        out_shape=jax.ShapeDtypeStruct((m, N_HEADS_C), F32),
        compiler_params=_cparams(("arbitrary",)),
        name="logf",
    )(cf2d, b_forget.reshape(1, N_HEADS_C))


FOX_EXT_ROWS = 16
FOX_SPLIT = 3


def _split_bf16(x):
    parts = []
    rem = x
    for _ in range(FOX_SPLIT):
        p = rem.astype(BF16).astype(F32)
        parts.append(p)
        rem = rem - p
    return parts


def _cumsum_kernel(lf_ref, dc_ref, kext_ref, carry_ref, *, kc, group):
    @pl.when(pl.program_id(1) == 0)
    def _():
        carry_ref[...] = jnp.zeros_like(carry_ref)

    tri = (lax.broadcasted_iota(I32, (kc, kc), 1) <= lax.broadcasted_iota(I32, (kc, kc), 0)).astype(F32)
    eye = (lax.broadcasted_iota(I32, (N_HEADS_C, N_HEADS_C), 0)
           == lax.broadcasted_iota(I32, (N_HEADS_C, N_HEADS_C), 1)).astype(F32)
    row = lax.broadcasted_iota(I32, (FOX_EXT_ROWS, kc), 0)
    carry = carry_ref[...]
    for g in range(group):
        lf = lf_ref[0, g * kc:(g + 1) * kc, :]
        dc = jnp.dot(tri, lf, preferred_element_type=F32, precision=lax.Precision.HIGHEST) + carry
        dc_ref[0, g * kc:(g + 1) * kc, :] = dc
        carry = dc[kc - 1:kc, :]
        dct = lax.dot_general(eye, dc, (((1,), (1,)), ((), ())), preferred_element_type=F32,
                              precision=lax.Precision.HIGHEST)
        neg = _split_bf16(-dct)
        for h in range(N_HEADS_C):
            tile = jnp.where(row < FOX_SPLIT, 1.0, 0.0)
            for i in range(FOX_SPLIT):
                tile = jnp.where(row == FOX_SPLIT + i, neg[i][h:h + 1, :], tile)
            kext_ref[0, g, h] = tile.astype(BF16)
    carry_ref[...] = carry


def _forget_cumsum(logf_all, kc):
    bn, lp, _ = logf_all.shape
    n_chunks = lp // kc
    group = max(g for g in (1, 2, 3, 4) if n_chunks % g == 0)
    return pl.pallas_call(
        functools.partial(_cumsum_kernel, kc=kc, group=group),
        grid=(bn, n_chunks // group),
        in_specs=[pl.BlockSpec((1, kc * group, N_HEADS_C), lambda b, t: (b, t, 0))],
        out_specs=[pl.BlockSpec((1, kc * group, N_HEADS_C), lambda b, t: (b, t, 0)),
                   pl.BlockSpec((1, group, N_HEADS_C, FOX_EXT_ROWS, kc), lambda b, t: (b, t, 0, 0, 0))],
        out_shape=[jax.ShapeDtypeStruct((bn, lp, N_HEADS_C), F32),
                   jax.ShapeDtypeStruct((bn, n_chunks, N_HEADS_C, FOX_EXT_ROWS, kc), BF16)],
        scratch_shapes=[pltpu.VMEM((1, N_HEADS_C), F32)],
        compiler_params=_cparams(("arbitrary", "arbitrary")),
        name="forget_cumsum",
    )(logf_all)


def _fox_kernel(q_ref, dq_ref, kt_ref, kext_ref, vx_ref, o_ref, qx_ref, m_ref, acc_ref, *, tq, kc, q_off):
    qb = pl.program_id(1)
    q_base = q_off + qb * tq
    n_c = (q_base + tq + kc - 1) // kc
    n_full = (q_base + 1) // kc
    lane = lax.broadcasted_iota(I32, (tq, LANES), 1)

    dq = dq_ref[0]
    for h in range(N_HEADS_C):
        pair = q_ref[:, (h // 2) * LANES:(h // 2 + 1) * LANES] * (HEAD_DIM_C ** -0.5)
        if h % 2:
            pair = pltpu.roll(pair, HEAD_DIM_C, axis=1)
        ext = jnp.where((lane >= HEAD_DIM_C + FOX_SPLIT) & (lane < HEAD_DIM_C + 2 * FOX_SPLIT), 1.0, 0.0)
        for i, part in enumerate(_split_bf16(dq[:, h:h + 1])):
            ext = jnp.where(lane == HEAD_DIM_C + i, part, ext)
        qx_ref[h] = jnp.where(lane < HEAD_DIM_C, pair, ext).astype(BF16)
    m_ref[...] = jnp.full(m_ref.shape, -jnp.inf, F32)
    acc_ref[...] = jnp.zeros(acc_ref.shape, F32)

    q_pos = q_base + lax.broadcasted_iota(I32, (tq, kc), 0)
    k_loc = lax.broadcasted_iota(I32, (tq, kc), 1)
    zrows = jnp.zeros((LANES - HEAD_DIM_C - FOX_EXT_ROWS, kc), BF16)

    def chunk(c, masked):
        r0 = pl.multiple_of(c * kc, kc)
        for h in range(N_HEADS_C):
            kx = jnp.concatenate([kt_ref[0, c, h * HEAD_DIM_C:(h + 1) * HEAD_DIM_C, :], kext_ref[0, c, h], zrows],
                                 axis=0)
            s = _mm(qx_ref[h], kx)
            if masked:
                s = jnp.where(r0 + k_loc <= q_pos, s, -jnp.inf)
            m_old = m_ref[h]
            m_new = jnp.maximum(m_old, jnp.broadcast_to(jnp.max(s, axis=1, keepdims=True), (tq, LANES)))
            p = jnp.exp(s - jnp.concatenate([m_new] * (kc // LANES), axis=1))
            pv = _mm(p.astype(BF16), vx_ref[0, pl.ds(r0, kc), h * LANES:(h + 1) * LANES])
            acc_ref[h] = jnp.exp(m_old - m_new) * acc_ref[h] + pv
            m_ref[h] = m_new

    def full_body(c, carry):
        chunk(c, False)
        return carry

    def diag_body(c, carry):
        chunk(c, True)
        return carry

    lax.fori_loop(0, n_full, full_body, 0)
    lax.fori_loop(n_full, n_c, diag_body, 0)

    for j in range(N_HEADS_C // 2):
        even = acc_ref[2 * j] / acc_ref[2 * j][:, HEAD_DIM_C:HEAD_DIM_C + 1]
        odd = acc_ref[2 * j + 1] / acc_ref[2 * j + 1][:, HEAD_DIM_C:HEAD_DIM_C + 1]
        o_ref[:, j * LANES:(j + 1) * LANES] = jnp.where(lane < HEAD_DIM_C, even, pltpu.roll(odd, HEAD_DIM_C, axis=1))


def _fox(cq, dq, kt, kext, vx, *, tq, q_off):
    bn, tq_total, width = cq.shape
    _, n_chunks, _, kc = kt.shape
    lp = n_chunks * kc
    return pl.pallas_call(
        functools.partial(_fox_kernel, tq=tq, kc=kc, q_off=q_off),
        grid=(bn, tq_total // tq),
        in_specs=[pl.BlockSpec((None, tq, width), lambda b, q: (b, q, 0)),
                  pl.BlockSpec((1, tq, N_HEADS_C), lambda b, q: (b, q, 0)),
                  pl.BlockSpec((1, n_chunks, width, kc), lambda b, q: (b, 0, 0, 0)),
                  pl.BlockSpec((1, n_chunks, N_HEADS_C, FOX_EXT_ROWS, kc), lambda b, q: (b, 0, 0, 0, 0)),
                  pl.BlockSpec((1, lp, N_HEADS_C * LANES), lambda b, q: (b, 0, 0))],
        out_specs=pl.BlockSpec((None, tq, width), lambda b, q: (b, q, 0)),
        out_shape=jax.ShapeDtypeStruct((bn, tq_total, width), F32),
        scratch_shapes=[pltpu.VMEM((N_HEADS_C, tq, LANES), BF16),
                        pltpu.VMEM((N_HEADS_C, tq, LANES), F32),
                        pltpu.VMEM((N_HEADS_C, tq, LANES), F32)],
        compiler_params=_cparams(("arbitrary", "arbitrary")),
        name="fox",
    )(cq, dq, kt, kext, vx)


def _merge_kernel(x_ref, ya_ref, yb_ref, yc_ref, g_ref, wa_ref, wb_ref, wc_ref, wo_ref, lng_ref, lnb_ref,
                  o_ref, *, alpha):
    ba = _mm(ya_ref[...].astype(BF16), wa_ref[...])
    bc = _mm(yc_ref[...].astype(BF16), wc_ref[...])
    glu = _mm(jax.nn.gelu(yb_ref[...]).astype(BF16), wb_ref[...])
    bb = glu[:, :D_MODEL] * jax.nn.sigmoid(glu[:, D_MODEL:])
    g = jax.nn.sigmoid(g_ref[...])
    merged = g[:, :D_MODEL] * ba + g[:, D_MODEL:2 * D_MODEL] * bb + g[:, 2 * D_MODEL:] * bc
    out = _mm(merged.astype(BF16), wo_ref[...])
    o_ref[...] = _layer_norm(alpha * x_ref[...] + out, lng_ref[...], lnb_ref[...])


def _merge(x2d, ya, yb, yb_time_major, yc, gates, wa, wb, wc, wo, ln_g, ln_b, alpha, n_time):
    m = x2d.shape[0]
    tm = min(OUT_TM, m)
    row = lambda w: pl.BlockSpec((tm, w), lambda i: (i, 0))
    const = lambda *s: pl.BlockSpec(s, lambda i: (0,) * len(s))
    if yb_time_major:
        n_t = n_time // tm
        yb_spec = pl.BlockSpec((tm, SSM_WIDTH), lambda i: (i % n_t, i // n_t))
    else:
        yb_spec = row(SSM_WIDTH)
    return pl.pallas_call(
        functools.partial(_merge_kernel, alpha=alpha),
        grid=(m // tm,),
        in_specs=[row(D_MODEL), row(512), yb_spec, row(512), row(N_BRANCH * D_MODEL),
                  const(512, D_MODEL), const(SSM_WIDTH, 2 * D_MODEL), const(512, D_MODEL),
                  const(D_MODEL, D_MODEL), const(1, D_MODEL), const(1, D_MODEL)],
        out_specs=row(D_MODEL),
        out_shape=jax.ShapeDtypeStruct((m, D_MODEL), F32),
        compiler_params=_cparams(("arbitrary",)),
        name="merge",
    )(x2d, ya, yb, yc, gates, wa, wb, wc, wo, ln_g.reshape(1, -1), ln_b.reshape(1, -1))


def _ffn_kernel(x_ref, wg_ref, wu_ref, wd_ref, lng_ref, lnb_ref, o_ref, acc_ref, *, alpha):
    c = pl.program_id(1)

    @pl.when(c == 0)
    def _():
        acc_ref[...] = jnp.zeros_like(acc_ref)

    xb = x_ref[...].astype(BF16)
    a = jax.nn.silu(_mm(xb, wg_ref[...])) * _mm(xb, wu_ref[...])
    acc_ref[...] += _mm(a.astype(BF16), wd_ref[...])

    @pl.when(c == pl.num_programs(1) - 1)
    def _():
        o_ref[...] = _layer_norm(alpha * x_ref[...] + acc_ref[...], lng_ref[...], lnb_ref[...])


def _ffn(x2d, w_gu, w_down, ln_g, ln_b, alpha):
    m = x2d.shape[0]
    tm = min(FFN_TM, m)
    d_ff = w_down.shape[0]
    n_c = d_ff // FFN_FC
    return pl.pallas_call(
        functools.partial(_ffn_kernel, alpha=alpha),
        grid=(m // tm, n_c),
        in_specs=[pl.BlockSpec((tm, D_MODEL), lambda i, c: (i, 0)),
                  pl.BlockSpec((D_MODEL, FFN_FC), lambda i, c: (0, c)),
                  pl.BlockSpec((D_MODEL, FFN_FC), lambda i, c: (0, n_c + c)),
                  pl.BlockSpec((FFN_FC, D_MODEL), lambda i, c: (c, 0)),
                  pl.BlockSpec((1, D_MODEL), lambda i, c: (0, 0)),
                  pl.BlockSpec((1, D_MODEL), lambda i, c: (0, 0))],
        out_specs=pl.BlockSpec((tm, D_MODEL), lambda i, c: (i, 0)),
        out_shape=jax.ShapeDtypeStruct((m, D_MODEL), F32),
        scratch_shapes=[pltpu.VMEM((tm, D_MODEL), F32)],
        compiler_params=_cparams(("arbitrary", "arbitrary")),
        name="ffn",
    )(x2d, w_gu, w_gu, w_down, ln_g.reshape(1, -1), ln_b.reshape(1, -1))


MOE_CAP = 320


def _moe_kernel(x_ref, wrt_ref, brt_ref, tri_ref, wgu_ref, wd_ref, lng_ref, lnb_ref, o_ref,
                acc_ref, ind_ref, pos_ref, gate_ref, *, alpha):
    e = pl.program_id(1)
    tm = x_ref.shape[0]
    xb = x_ref[...].astype(BF16)
    row = lax.broadcasted_iota(I32, (N_EXPERTS, tm), 0).astype(F32)

    @pl.when(e == 0)
    def _():
        acc_ref[...] = jnp.zeros_like(acc_ref)
        logits = _mm_nt(wrt_ref[...], xb) + brt_ref[...]
        m1 = jnp.max(logits, axis=0, keepdims=True)
        i1 = jnp.min(jnp.where(logits == m1, row, float(N_EXPERTS)), axis=0, keepdims=True)
        rest = jnp.where(row == i1, -jnp.inf, logits)
        m2 = jnp.max(rest, axis=0, keepdims=True)
        i2 = jnp.min(jnp.where(rest == m2, row, float(N_EXPERTS)), axis=0, keepdims=True)
        ex = jnp.exp(m2 - m1)
        ind = jnp.where((row == i1) | (row == i2), 1.0, 0.0)
        ind_ref[...] = ind
        gate_ref[...] = jnp.where(row == i1, 1.0 / (1.0 + ex), 0.0) + jnp.where(row == i2, ex / (1.0 + ex), 0.0)
        pos_ref[...] = _mm(ind.astype(BF16), tri_ref[...])

    sel = ind_ref[pl.ds(e, 1), :]
    pos = pos_ref[pl.ds(e, 1), :]
    gate = gate_ref[pl.ds(e, 1), :]
    count = jnp.sum(sel)
    slot = lax.broadcasted_iota(I32, (MOE_CAP, tm), 0).astype(F32)

    def block(j, carry):
        pick = (pos == slot + (j * MOE_CAP).astype(F32)) & (sel > 0.5)
        pick_b = jnp.where(pick, 1.0, 0.0).astype(BF16)
        xe = _mm(pick_b, xb).astype(BF16)
        h = _mm(xe, wgu_ref[0])
        a = jax.nn.silu(h[:, :D_FF_EXPERT]) * h[:, D_FF_EXPERT:]
        ge = jnp.sum(jnp.where(pick, gate, 0.0), axis=1, keepdims=True)
        y = (ge * _mm(a.astype(BF16), wd_ref[0])).astype(BF16)
        acc_ref[...] += lax.dot_general(pick_b, y, (((0,), (0,)), ((), ())), preferred_element_type=F32)
        return carry

    n_blocks = (count.astype(I32) + MOE_CAP - 1) // MOE_CAP
    lax.fori_loop(0, n_blocks, block, 0)

    @pl.when(e == pl.num_programs(1) - 1)
    def _():
        o_ref[...] = _layer_norm(alpha * x_ref[...] + acc_ref[...], lng_ref[...], lnb_ref[...])


def _moe(x2d, w_router, b_router, w_exp_gu, w_exp_down, ln_g, ln_b, alpha):
    m = x2d.shape[0]
    tm = min(MOE_TM, m)
    tri = (jnp.arange(tm)[:, None] < jnp.arange(tm)[None, :]).astype(BF16)
    return pl.pallas_call(
        functools.partial(_moe_kernel, alpha=alpha),
        grid=(m // tm, N_EXPERTS),
        in_specs=[pl.BlockSpec((tm, D_MODEL), lambda i, e: (i, 0)),
                  pl.BlockSpec((N_EXPERTS, D_MODEL), lambda i, e: (0, 0)),
                  pl.BlockSpec((N_EXPERTS, 1), lambda i, e: (0, 0)),
                  pl.BlockSpec((tm, tm), lambda i, e: (0, 0)),
                  pl.BlockSpec((1, D_MODEL, 2 * D_FF_EXPERT), lambda i, e: (e, 0, 0)),
                  pl.BlockSpec((1, D_FF_EXPERT, D_MODEL), lambda i, e: (e, 0, 0)),
                  pl.BlockSpec((1, D_MODEL), lambda i, e: (0, 0)),
                  pl.BlockSpec((1, D_MODEL), lambda i, e: (0, 0))],
        out_specs=pl.BlockSpec((tm, D_MODEL), lambda i, e: (i, 0)),
        out_shape=jax.ShapeDtypeStruct((m, D_MODEL), F32),
        scratch_shapes=[pltpu.VMEM((tm, D_MODEL), F32), pltpu.VMEM((N_EXPERTS, tm), F32),
                        pltpu.VMEM((N_EXPERTS, tm), F32), pltpu.VMEM((N_EXPERTS, tm), F32)],
        compiler_params=_cparams(("arbitrary", "arbitrary")),
        name="moe",
    )(x2d, jnp.swapaxes(w_router, 0, 1), b_router.reshape(-1, 1), tri, w_exp_gu, w_exp_down,
      ln_g.reshape(1, -1), ln_b.reshape(1, -1))


def _pad_rows(a, rows):
    if a.shape[1] == rows:
        return a
    pad = jnp.zeros((a.shape[0], rows - a.shape[1]) + a.shape[2:], a.dtype)
    return jnp.concatenate([a, pad], axis=1)


def _round_up(n, mult):
    return -(-n // mult) * mult


def _token_mixer(x, past, lw):
    bn, n_time, _ = x.shape
    m = bn * n_time
    x2d = x.reshape(m, D_MODEL)
    prompt = past is None
    per_batch = lambda a: a.reshape((bn, -1) + a.shape[1:])

    if prompt:
        past_len = 0
        proj = _project_prompt(x2d, lw["w_in"], lw["b_forget"], bn, n_time, lw["prev_ck"], lw["prev_cv"])
        a_k, a_v, a_ki, logf = [per_batch(proj[n]) for n in ("a_k", "a_v", "a_ki", "logf")]
        c_k, c_v = proj["ck"], proj["cv"]
        logf_all = logf
        h0_re = jnp.zeros((bn, SSM_STATES), F32)
        h0_im = jnp.zeros((bn, SSM_STATES), F32)
        u_tm = proj["u"].reshape(n_time, bn, SSM_WIDTH)
        qa, qit, wt = per_batch(proj["qa"]), proj["qit"], proj["wt"]
        kx, vxa, ki16, vxc = [per_batch(proj[n]) for n in ("kx", "vxa", "ki16", "vxc")]
        kt = proj["kt"].reshape(bn, n_time // FOX_KC, 512, FOX_KC)
    else:
        p_ak, p_av, p_aki, p_hr, p_hi, p_ck, p_cv, p_logf = past
        past_len = p_ak.shape[1]
        proj = _project(x2d, lw["w_in"])
        kv, misc = per_batch(proj["kv"]), per_batch(proj["misc"])
        a_k, a_v = kv[..., :HEAD_DIM_A], kv[..., HEAD_DIM_A:]
        a_ki = misc[..., :IDX_DIM]
        a_w = misc[..., MISC_W_LANE:MISC_W_LANE + N_IDX_HEADS]
        c_f = misc[..., MISC_F_LANE:MISC_F_LANE + N_HEADS_C]
        c_k, c_v = per_batch(proj["ck"]), per_batch(proj["cv"])
        logf = _log_forget(c_f.reshape(m, N_HEADS_C), lw["b_forget"]).reshape(bn, n_time, N_HEADS_C)
        logf_all = jnp.concatenate([p_logf, logf], axis=1)
        h0_re = p_hr.reshape(bn, SSM_STATES)
        h0_im = p_hi.reshape(bn, SSM_STATES)
        u_tm = jnp.swapaxes(per_batch(proj["u"]), 0, 1)
        tq_pad = _round_up(n_time, DSA_TQ)
        qa = _pad_rows(per_batch(proj["qa"]), tq_pad)
        qit = jnp.swapaxes(_pad_rows(per_batch(proj["qi"]), tq_pad), 1, 2)
        wt = jnp.swapaxes(_pad_rows(a_w, tq_pad), 1, 2)
        kx, vxa, ki16, kt, vxc = _cache_layouts(
            p_ak, p_av, p_aki, p_ck.reshape(bn, past_len, -1), p_cv.reshape(bn, past_len, -1),
            _pad_rows(kv, FOX_KC), _pad_rows(misc, FOX_KC), _pad_rows(c_k, FOX_KC), _pad_rows(c_v, FOX_KC))
    kv_len = past_len + n_time
    n_top = min(TOPK_MAX, kv_len // 4)

    y_b, h_re, h_im = _s5(u_tm, h0_re, h0_im, lw["ab_re"], lw["ab_im"], lw["bd"], lw["cr"], lw["ci"], lw["ssm_d"])
    if prompt:
        y_b = y_b.reshape(n_time, bn * SSM_WIDTH)
    else:
        y_b = jnp.swapaxes(y_b, 0, 1).reshape(m, SSM_WIDTH)

    y_a = _dsa(qa, qit, wt, kx, vxa, ki16, lw["bias_near"], lw["bias_lhs"],
               q_off=past_len, kv_len=kv_len, n_top=n_top)[:, :n_time].reshape(m, -1)

    lpc = _round_up(kv_len, FOX_KC)
    dcum, kext = _forget_cumsum(_pad_rows(logf_all, lpc), FOX_KC)
    y_c = _fox(per_batch(proj["cq"]), dcum[:, past_len:kv_len], kt, kext, vxc,
               tq=min(FOX_TQ_PROMPT, n_time), q_off=past_len).reshape(m, -1)

    x1 = _merge(x2d, y_a, y_b, prompt, y_c, proj["gates"], lw["w_a_out"], lw["w_b_glu"], lw["w_c_out"], lw["w_o"],
                lw["ln1_g"], lw["ln1_b"], lw["alpha"], n_time)
    if not prompt:
        c_k = c_k.reshape(bn, n_time, N_HEADS_C, HEAD_DIM_C)
        c_v = c_v.reshape(bn, n_time, N_HEADS_C, HEAD_DIM_C)
    new_state = (a_k, a_v, a_ki, h_re.reshape(bn, N_GROUPS, STATE_DIM), h_im.reshape(bn, N_GROUPS, STATE_DIM),
                 c_k, c_v, logf)
    return x1, new_state


def kernel(x_prompt, x_sample, cache_a_k, cache_a_v, cache_a_kidx, state_ssm_re, state_ssm_im, cache_c_k, cache_c_v, cache_c_logf, w_in, b_forget, ssm_lam_re, ssm_lam_im, ssm_log_step, ssm_b_re, ssm_b_im, ssm_c_re, ssm_c_im, ssm_d, w_a_out, w_b_glu, w_c_out, w_o, ln1_g, ln1_b, ln2_g, ln2_b, t5_table, w_ffn_gu, w_ffn_down, w_router, b_router, w_exp_gu, w_exp_down):
    depth = w_in.shape[0]
    alpha = float((2 * depth) ** 0.25)
    bias_near, bias_lhs = _dsa_bias_consts(t5_table)
    xp, xs = x_prompt, x_sample
    rows_p = [[] for _ in range(8)]
    rows_s = [[] for _ in range(8)]
    prev_ck = prev_cv = None
    for layer in range(depth):
        ab_re, ab_im, bb_re, bb_im = _s5_discretize(ssm_lam_re[layer], ssm_lam_im[layer], ssm_log_step[layer],
                                                    ssm_b_re[layer], ssm_b_im[layer])
        lw = dict(
            w_in=_pack_w_in(w_in[layer]), b_forget=b_forget[layer],
            ab_re=ab_re.reshape(1, SSM_STATES), ab_im=ab_im.reshape(1, SSM_STATES),
            bd=jnp.concatenate([_block_diag(bb_re), _block_diag(bb_im)], axis=1).astype(BF16),
            cr=_block_diag(jnp.swapaxes(ssm_c_re[layer], 1, 2)).astype(BF16),
            ci=_block_diag(jnp.swapaxes(ssm_c_im[layer], 1, 2)).astype(BF16),
            ssm_d=ssm_d[layer].reshape(1, SSM_WIDTH),
            w_a_out=w_a_out[layer].astype(BF16), w_b_glu=w_b_glu[layer].astype(BF16),
            w_c_out=w_c_out[layer].astype(BF16), w_o=w_o[layer].astype(BF16),
            ln1_g=ln1_g[layer], ln1_b=ln1_b[layer], bias_near=bias_near, bias_lhs=bias_lhs, alpha=alpha)
        past = (cache_a_k[layer], cache_a_v[layer], cache_a_kidx[layer], state_ssm_re[layer],
                state_ssm_im[layer], cache_c_k[layer], cache_c_v[layer], cache_c_logf[layer])
        xp1, st_p = _token_mixer(xp, None, dict(lw, prev_ck=prev_ck, prev_cv=prev_cv))
        prev_ck, prev_cv = st_p[5], st_p[6]
        xs1, st_s = _token_mixer(xs, past, lw)
        i = layer // 2
        if layer % 2 == 0:
            wgu, wdn = w_ffn_gu[i].astype(BF16), w_ffn_down[i].astype(BF16)
            xp2 = _ffn(xp1, wgu, wdn, ln2_g[layer], ln2_b[layer], alpha)
            xs2 = _ffn(xs1, wgu, wdn, ln2_g[layer], ln2_b[layer], alpha)
        else:
            wr, wgu, wdn = w_router[i].astype(BF16), w_exp_gu[i].astype(BF16), w_exp_down[i].astype(BF16)
            xp2 = _moe(xp1, wr, b_router[i], wgu, wdn, ln2_g[layer], ln2_b[layer], alpha)
            xs2 = _moe(xs1, wr, b_router[i], wgu, wdn, ln2_g[layer], ln2_b[layer], alpha)
        xp = xp2.reshape(x_prompt.shape)
        xs = xs2.reshape(x_sample.shape)
        for j in range(8):
            rows_p[j].append(st_p[j])
            rows_s[j].append(st_s[j])
    bshape = (depth,) + x_prompt.shape[:2] + (N_HEADS_C, HEAD_DIM_C)
    rows_p[5], rows_p[6] = None, None
    (a_k_p, a_v_p, a_kidx_p, ssm_re_p, ssm_im_p, _, _, c_logf_p) = [None if r is None else jnp.stack(r) for r in rows_p]
    c_k_p, c_v_p = prev_ck.reshape(bshape), prev_cv.reshape(bshape)
    (a_k_s, a_v_s, a_kidx_s, ssm_re_s, ssm_im_s, c_k_s, c_v_s, c_logf_s) = [jnp.stack(r) for r in rows_s]
    return (xp, xs, a_k_p, a_k_s, a_v_p, a_v_s, a_kidx_p, a_kidx_s, ssm_re_p, ssm_re_s,
            ssm_im_p, ssm_im_s, c_k_p, c_k_s, c_v_p, c_v_s, c_logf_p, c_logf_s)
```
